```python
import jax, jax.numpy as jnp
from jax import lax
import numpy as np

D_MODEL = 4096
BATCH = 4
SEQ = 2048
DEPTH = 1
DEC_BATCH = 1
DEC_SEQ = 8192
PAST_LEN = 128

ATT_PATTERNS = ((128, 1), (512, 4), (2048, 16))
N_GROUPS = 3
ATT_HEAD_DIM = 128
ATT_HEADS = D_MODEL // 512
ATT_WIDTH = ATT_HEADS * ATT_HEAD_DIM
GLA_HEADS = 4
GLA_KEY = D_MODEL // 4
GLA_VAL = D_MODEL // 2
GLA_DK = GLA_KEY // GLA_HEADS
GLA_DV = GLA_VAL // GLA_HEADS
GLA_RANK = 16
GLA_TAU = 16.0
GLA_CHUNK = 64
NORM_EPS = 1e-6
NEG_INF = -1e30

IN_SPLITS = (3 * N_GROUPS * ATT_WIDTH,
             ATT_WIDTH,
             GLA_KEY, GLA_KEY, GLA_VAL,
             GLA_VAL,
             GLA_RANK, GLA_RANK,
             D_MODEL, D_MODEL)
IN_COLS = sum(IN_SPLITS)

kernel_name = "hybrid_dilated_gla_encoder"


def rms_norm(x, g):
    xf = x.astype(jnp.float32)
    y = xf * lax.rsqrt(jnp.mean(xf * xf, axis=-1, keepdims=True) + NORM_EPS)
    return (y * g.astype(jnp.float32)).astype(x.dtype)


def alibi_slopes(n_heads):
    return jnp.exp2(-8.0 * (jnp.arange(n_heads, dtype=jnp.float32) + 1.0) / n_heads)


def dilated_group_attention(q, k, v, window, dilation, slopes):
    B, S, H, hd = q.shape
    R = window // (2 * dilation)
    L = S // dilation
    nb = -(-L // R)
    Lp = nb * R

    def classes(t):
        return t.reshape(B, L, dilation, H, hd).transpose(0, 2, 1, 3, 4)

    qc, kc, vc = classes(q), classes(k), classes(v)
    qb = jnp.pad(qc, ((0, 0), (0, 0), (0, Lp - L), (0, 0), (0, 0))).reshape(B, dilation, nb, R, H, hd)
    kv_pad = ((0, 0), (0, 0), (R, Lp - L + R), (0, 0), (0, 0))
    kp = jnp.pad(kc, kv_pad).reshape(B, dilation, nb + 2, R, H, hd)
    vp = jnp.pad(vc, kv_pad).reshape(B, dilation, nb + 2, R, H, hd)
    kn = jnp.concatenate([kp[:, :, 0:nb], kp[:, :, 1:nb + 1], kp[:, :, 2:nb + 2]], axis=3)
    vn = jnp.concatenate([vp[:, :, 0:nb], vp[:, :, 1:nb + 1], vp[:, :, 2:nb + 2]], axis=3)

    s = jnp.einsum('bdnqhc,bdnkhc->bdnhqk', qb, kn,
                   preferred_element_type=jnp.float32) * (hd ** -0.5)
    qi = jnp.arange(R)[:, None]
    ki = jnp.arange(3 * R)[None, :]
    delta = ki - R - qi
    kpos = jnp.arange(nb)[:, None, None] * R - R + ki[None]
    valid = (jnp.abs(delta) <= R)[None] & (kpos >= 0) & (kpos < L)
    dist = (jnp.abs(delta) * dilation).astype(jnp.float32)
    bias = -slopes[:, None, None] * dist[None]
    s = s + bias[None, None, None]
    s = jnp.where(valid[None, None, :, None], s, NEG_INF)
    lse = jax.nn.logsumexp(s, axis=-1)
    p = jnp.exp(s - lse[..., None])
    o = jnp.einsum('bdnhqk,bdnkhc->bdnqhc', p, vn.astype(jnp.float32))
    o = o.reshape(B, dilation, Lp, H, hd)[:, :, :L].transpose(0, 2, 1, 3, 4).reshape(B, S, H, hd)
    lse = lse.transpose(0, 1, 2, 4, 3).reshape(B, dilation, Lp, H)[:, :, :L]
    lse = lse.transpose(0, 2, 1, 3).reshape(B, S, H)
    return o, lse


def dilated_mixture(qkv, B, S):
    qkv = qkv.reshape(B, S, N_GROUPS, 3, ATT_HEADS, ATT_HEAD_DIM)
    slopes = alibi_slopes(ATT_HEADS)
    outs, lses = [], []
    for g, (window, dilation) in enumerate(ATT_PATTERNS):
        o, l = dilated_group_attention(qkv[:, :, g, 0], qkv[:, :, g, 1], qkv[:, :, g, 2],
                                       window, dilation, slopes)
        outs.append(o)
        lses.append(l)
    o = jnp.stack(outs, axis=0)
    l = jnp.stack(lses, axis=0)
    wts = jax.nn.softmax(l, axis=0)
    return jnp.sum(wts[..., None] * o, axis=0).reshape(B, S, ATT_WIDTH)


def gla_direction(q, k, v, log_a, strict):
    B, S, H, dk = q.shape
    dv = v.shape[-1]
    C = GLA_CHUNK
    n = S // C

    def r(t):
        return t.reshape(B, n, C, H, t.shape[-1])

    q, k, v, log_a = r(q), r(k), r(v), r(log_a)
    b = jnp.cumsum(log_a, axis=2)
    qe = q * jnp.exp(b)
    ke = k * jnp.exp(-b)
    b_last = b[:, :, -1]
    kd = k * jnp.exp(b_last[:, :, None] - b)
    A = jnp.einsum('bnqhc,bnkhc->bnhqk', qe, ke)
    mask = jnp.tril(jnp.ones((C, C), dtype=bool), -1 if strict else 0)
    A = jnp.where(mask, A, 0.0)
    o_intra = jnp.einsum('bnhqk,bnkhv->bnqhv', A, v)

    def step(state, inp):
        qe_c, kd_c, v_c, bl = inp
        o = jnp.einsum('bqhc,bhcv->bqhv', qe_c, state)
        state = jnp.exp(bl)[..., None] * state + jnp.einsum('bkhc,bkhv->bhcv', kd_c, v_c)
        return state, o

    xs = (jnp.moveaxis(qe, 1, 0), jnp.moveaxis(kd, 1, 0), jnp.moveaxis(v, 1, 0),
          jnp.moveaxis(b_last, 1, 0))
    s0 = jnp.zeros((B, H, dk, dv), jnp.float32)
    _, o_inter = lax.scan(step, s0, xs)
    o = o_intra + jnp.moveaxis(o_inter, 0, 1)
    return o.reshape(B, S, H, dv)


def gla_mixer(q, k, v, z_f, z_b, w2_f, b_f, w2_b, b_b, out_g, B, S):
    f32 = jnp.float32
    q = q.astype(f32).reshape(B, S, GLA_HEADS, GLA_DK) * (GLA_DK ** -0.5)
    k = k.astype(f32).reshape(B, S, GLA_HEADS, GLA_DK)
    v = v.astype(f32).reshape(B, S, GLA_HEADS, GLA_DV)
    la_f = jax.nn.log_sigmoid(z_f.astype(f32) @ w2_f.astype(f32) + b_f.astype(f32)) / GLA_TAU
    la_b = jax.nn.log_sigmoid(z_b.astype(f32) @ w2_b.astype(f32) + b_b.astype(f32)) / GLA_TAU
    la_f = la_f.reshape(B, S, GLA_HEADS, GLA_DK)
    la_b = la_b.reshape(B, S, GLA_HEADS, GLA_DK)
    o_f = gla_direction(q, k, v, la_f, False)
    flip = lambda t: jnp.flip(t, axis=1)
    o_b = flip(gla_direction(flip(q), flip(k), flip(v), flip(la_b), True))
    o = o_f + o_b
    o = o * lax.rsqrt(jnp.mean(o * o, axis=-1, keepdims=True) + NORM_EPS) * out_g.astype(f32)
    return o.reshape(B, S, GLA_VAL)


def encoder_layer(x, norm_g, w_in, gla_w2_f, gla_b_f, gla_w2_b, gla_b_b, gla_norm_g,
                  w_up_a, w_up_b, w_o):
    B, S, _ = x.shape
    h = rms_norm(x, norm_g)
    proj = jnp.einsum('bsd,de->bse', h, w_in)
    splits = [int(i) for i in np.cumsum(IN_SPLITS)[:-1]]
    (att_qkv, att_z, gq, gk, gv, gz, lr_f, lr_b, mg_a, mg_b) = jnp.split(proj, splits, axis=-1)
    ya = dilated_mixture(att_qkv, B, S) * jax.nn.silu(att_z.astype(jnp.float32))
    yb = gla_mixer(gq, gk, gv, lr_f, lr_b, gla_w2_f, gla_b_f, gla_w2_b, gla_b_b,
                   gla_norm_g, B, S) * jax.nn.silu(gz.astype(jnp.float32))
    ua = jnp.einsum('bse,ed->bsd', ya.astype(x.dtype), w_up_a)
    ub = jnp.einsum('bse,ed->bsd', yb.astype(x.dtype), w_up_b)
    merged = jax.nn.sigmoid(mg_a) * ua + jax.nn.sigmoid(mg_b) * ub
    return x + jnp.einsum('bsd,de->bse', merged, w_o)


def setup_inputs(seed: int = 0) -> dict:
    key = jax.random.key(seed)
    ks = jax.random.split(key, 16)
    f32 = jnp.float32
    nrm = lambda k, shape, scale: jax.random.normal(k, shape, f32) * scale
    return {
        "x_prompt": jax.random.normal(ks[0], (BATCH, SEQ, D_MODEL), f32),
        "x_sample": jax.random.normal(ks[1], (DEC_BATCH, DEC_SEQ, D_MODEL), f32),
        "norm_g": 1.0 + nrm(ks[2], (DEPTH, D_MODEL), 0.02),
        "w_in": nrm(ks[3], (DEPTH, D_MODEL, IN_COLS), D_MODEL ** -0.5),
        "gla_w2_f": nrm(ks[4], (DEPTH, GLA_RANK, GLA_KEY), GLA_RANK ** -0.5),
        "gla_b_f": nrm(ks[5], (DEPTH, GLA_KEY), 0.1),
        "gla_w2_b": nrm(ks[6], (DEPTH, GLA_RANK, GLA_KEY), GLA_RANK ** -0.5),
        "gla_b_b": nrm(ks[7], (DEPTH, GLA_KEY), 0.1),
        "gla_norm_g": 1.0 + nrm(ks[8], (DEPTH, GLA_DV), 0.02),
        "w_up_a": nrm(ks[9], (DEPTH, ATT_WIDTH, D_MODEL), ATT_WIDTH ** -0.5),
        "w_up_b": nrm(ks[10], (DEPTH, GLA_VAL, D_MODEL), GLA_VAL ** -0.5),
        "w_o": nrm(ks[11], (DEPTH, D_MODEL, D_MODEL), D_MODEL ** -0.5),
        "final_norm_g": 1.0 + nrm(ks[12], (D_MODEL,), 0.02),
    }


def reference(x_prompt, x_sample, norm_g, w_in, gla_w2_f, gla_b_f, gla_w2_b, gla_b_b,
              gla_norm_g, w_up_a, w_up_b, w_o, final_norm_g):
    def run(x):
        for l in range(DEPTH):
            x = encoder_layer(x, norm_g[l], w_in[l], gla_w2_f[l], gla_b_f[l], gla_w2_b[l],
                              gla_b_b[l], gla_norm_g[l], w_up_a[l], w_up_b[l], w_o[l])
        return rms_norm(x, final_norm_g)

    y_prompt = run(x_prompt)
    y_sample = run(x_sample)
    return (y_prompt, y_sample)
```

```python
import functools

import jax
import jax.numpy as jnp
from jax import lax
from jax.experimental import pallas as pl
from jax.experimental.pallas import tpu as pltpu

F32 = jnp.float32
BF16 = jnp.bfloat16

D_MODEL = 4096
ATT_PATTERNS = ((128, 1), (512, 4), (2048, 16))
N_GROUPS = 3
ATT_HEAD_DIM = 128
ATT_HEADS = 8
ATT_WIDTH = ATT_HEADS * ATT_HEAD_DIM
ATT_RADIUS = 64
GLA_HEADS = 4
GLA_KEY = 1024
GLA_VAL = 2048
GLA_DK = GLA_KEY // GLA_HEADS
GLA_DV = GLA_VAL // GLA_HEADS
GLA_RANK = 16
GLA_TAU = 16.0
GLA_CHUNK = 64
NORM_EPS = 1e-6
NEG_INF = -1e30

OFF_QKV = 0
OFF_AZ = 3 * N_GROUPS * ATT_WIDTH
OFF_GQ = OFF_AZ + ATT_WIDTH
OFF_GK = OFF_GQ + GLA_KEY
OFF_GV = OFF_GK + GLA_KEY
OFF_GZ = OFF_GV + GLA_VAL
OFF_LR = OFF_GZ + GLA_VAL
OFF_MGA = OFF_GZ + GLA_VAL
OFF_MGB = OFF_MGA + D_MODEL
N_MAIN = OFF_MGB + D_MODEL
LR_PAD = 128

VMEM_LIMIT = 56 * 1024 * 1024


def _params(sem):
    return pltpu.CompilerParams(dimension_semantics=sem, vmem_limit_bytes=VMEM_LIMIT)


def _rmsnorm_kernel(x_ref, g_ref, o_ref):
    x = x_ref[...]
    ms = jnp.mean(x * x, axis=-1, keepdims=True)
    o_ref[...] = ((x * lax.rsqrt(ms + NORM_EPS)) * g_ref[...]).astype(o_ref.dtype)


def _rmsnorm(x, g, tm=256):
    m, d = x.shape
    return pl.pallas_call(
        _rmsnorm_kernel,
        grid=(m // tm,),
        in_specs=[pl.BlockSpec((tm, d), lambda i: (i, 0)),
                  pl.BlockSpec((1, d), lambda i: (0, 0))],
        out_specs=pl.BlockSpec((tm, d), lambda i: (i, 0)),
        out_shape=jax.ShapeDtypeStruct((m, d), BF16),
        compiler_params=_params(("parallel",)),
        name="rmsnorm",
    )(x, g.reshape(1, d))


def _matmul_kernel(a_ref, b_ref, o_ref):
    o_ref[...] = jnp.dot(a_ref[...], b_ref[...],
                         preferred_element_type=F32).astype(o_ref.dtype)


def _matmul(a, b, tm, tn, name):
    m, k = a.shape
    n = b.shape[1]
    return pl.pallas_call(
        _matmul_kernel,
        grid=(m // tm, n // tn),
        in_specs=[pl.BlockSpec((tm, k), lambda i, j: (i, 0)),
                  pl.BlockSpec((k, tn), lambda i, j: (0, j))],
        out_specs=pl.BlockSpec((tm, tn), lambda i, j: (i, j)),
        out_shape=jax.ShapeDtypeStruct((m, n), BF16),
        compiler_params=_params(("parallel", "arbitrary")),
        name=name,
    )(a, b)


ATT_QB = 128
ATT_KW = ATT_QB + 2 * ATT_RADIUS


def _attn_kernel(slopes_ref, q_ref, k_ref, v_ref, o_ref, lse_ref, kp_ref, vp_ref, *,
                 seq, dilation):
    h = pl.program_id(1)
    slope = slopes_ref[h] * float(dilation)
    halo = jnp.zeros((ATT_RADIUS, ATT_HEAD_DIM), kp_ref.dtype)
    for ref, src in ((kp_ref, k_ref), (vp_ref, v_ref)):
        ref[0:ATT_RADIUS, :] = halo
        ref[ATT_RADIUS + seq:ATT_RADIUS + seq + ATT_RADIUS, :] = halo
        ref[ATT_RADIUS:ATT_RADIUS + seq, :] = src[0]

    row = lax.broadcasted_iota(jnp.int32, (ATT_QB, ATT_KW), 0)
    col = lax.broadcasted_iota(jnp.int32, (ATT_QB, ATT_KW), 1)
    dist = jnp.abs(col - ATT_RADIUS - row)
    bias = jnp.where(dist <= ATT_RADIUS, -slope * dist.astype(F32), NEG_INF)
    scale = ATT_HEAD_DIM ** -0.5

    def body(i, carry):
        q0 = pl.multiple_of(i * ATT_QB, ATT_QB)
        q = q_ref[0, pl.ds(q0, ATT_QB), :]
        kw = kp_ref[pl.ds(q0, ATT_KW), :]
        vw = vp_ref[pl.ds(q0, ATT_KW), :]
        s = lax.dot_general(q, kw, (((1,), (1,)), ((), ())),
                            preferred_element_type=F32) * scale + bias
        kpos = col + (q0 - ATT_RADIUS)
        s = jnp.where((kpos >= 0) & (kpos < seq), s, NEG_INF)
        m = jnp.max(s, axis=-1, keepdims=True)
        e = jnp.exp(s - m)
        l = jnp.sum(e, axis=-1, keepdims=True)
        o = jnp.dot(e.astype(BF16), vw, preferred_element_type=F32) / l
        o_ref[0, pl.ds(q0, ATT_QB), :] = o
        lse_ref[0, pl.ds(q0, ATT_QB), :] = jnp.broadcast_to(m + jnp.log(l),
                                                            (ATT_QB, ATT_HEAD_DIM))
        return carry

    lax.fori_loop(0, seq // ATT_QB, body, 0)


def _attention(slopes, qkv, col0, dilation):
    nb, seq, _ = qkv.shape

    def spec(off):
        return pl.BlockSpec((1, seq, ATT_HEAD_DIM), lambda n, h: (n, 0, col0 + off + h))

    out_spec = pl.BlockSpec((1, seq, ATT_HEAD_DIM), lambda n, h: (n, 0, h))
    out_shape = jax.ShapeDtypeStruct((nb, seq, ATT_WIDTH), F32)
    return pl.pallas_call(
        functools.partial(_attn_kernel, seq=seq, dilation=dilation),
        grid=(nb, ATT_HEADS),
        in_specs=[pl.BlockSpec(memory_space=pltpu.SMEM),
                  spec(0), spec(ATT_HEADS), spec(2 * ATT_HEADS)],
        out_specs=[out_spec, out_spec],
        out_shape=[out_shape, out_shape],
        scratch_shapes=[pltpu.VMEM((seq + 2 * ATT_RADIUS, ATT_HEAD_DIM), BF16),
                        pltpu.VMEM((seq + 2 * ATT_RADIUS, ATT_HEAD_DIM), BF16)],
        compiler_params=_params(("parallel", "parallel")),
        name=f"attn_d{dilation}",
    )(slopes, qkv, qkv, qkv)


def _mix_kernel(o1, o2, o3, l1, l2, l3, z_ref, y_ref):
    a, b, c = l1[...], l2[...], l3[...]
    m = jnp.maximum(jnp.maximum(a, b), c)
    ea, eb, ec = jnp.exp(a - m), jnp.exp(b - m), jnp.exp(c - m)
    mix = (ea * o1[...] + eb * o2[...] + ec * o3[...]) / (ea + eb + ec)
    z = z_ref[...].astype(F32)
    y_ref[...] = (mix * (z / (1.0 + jnp.exp(-z)))).astype(y_ref.dtype)


def _mix(outs, lses, proj, tm=512):
    m = proj.shape[0]
    blk = pl.BlockSpec((tm, ATT_WIDTH), lambda i: (i, 0))
    return pl.pallas_call(
        _mix_kernel,
        grid=(m // tm,),
        in_specs=[blk] * 6 + [pl.BlockSpec((tm, ATT_WIDTH),
                                           lambda i: (i, OFF_AZ // ATT_WIDTH))],
        out_specs=blk,
        out_shape=jax.ShapeDtypeStruct((m, ATT_WIDTH), BF16),
        compiler_params=_params(("parallel",)),
        name="attn_mix",
    )(*outs, *lses, proj)


GLA_TILE_CHUNKS = 8
GLA_TILE = GLA_TILE_CHUNKS * GLA_CHUNK


def _gla_kernel(*refs, reverse):
    if reverse:
        (q_ref, k_ref, v_ref, z_ref, w2_ref, b2_ref, of_ref, gz_ref, g_ref,
         y_ref, st_ref) = refs
    else:
        q_ref, k_ref, v_ref, z_ref, w2_ref, b2_ref, y_ref, st_ref = refs

    @pl.when(pl.program_id(2) == 0)
    def _():
        st_ref[...] = jnp.zeros_like(st_ref)

    x = jnp.dot(z_ref[0], w2_ref[...], preferred_element_type=F32) + b2_ref[...]
    log_a = (jnp.minimum(x, 0.0) - jnp.log1p(jnp.exp(-jnp.abs(x)))) * (1.0 / GLA_TAU)

    ri = lax.broadcasted_iota(jnp.int32, (GLA_CHUNK, GLA_CHUNK), 0)
    ci = lax.broadcasted_iota(jnp.int32, (GLA_CHUNK, GLA_CHUNK), 1)
    if reverse:
        cum = (ci >= ri).astype(BF16)
        keep = ci > ri
        last = 0
    else:
        cum = (ci <= ri).astype(BF16)
        keep = ci <= ri
        last = GLA_CHUNK - 1
    nt = (((1,), (1,)), ((), ()))

    order = range(GLA_TILE_CHUNKS)
    for c in (reversed(order) if reverse else order):
        rows = slice(c * GLA_CHUNK, (c + 1) * GLA_CHUNK)
        la = log_a[rows]
        hi = la.astype(BF16)
        r1 = la - hi.astype(F32)
        mid = r1.astype(BF16)
        lo = (r1 - mid.astype(F32)).astype(BF16)
        b = (jnp.dot(cum, hi, preferred_element_type=F32)
             + jnp.dot(cum, mid, preferred_element_type=F32)
             + jnp.dot(cum, lo, preferred_element_type=F32))
        b_end = b[last:last + 1, :]
        q = q_ref[0, rows, :].astype(F32) * (GLA_DK ** -0.5)
        k = k_ref[0, rows, :].astype(F32)
        v = v_ref[0, rows, :]
        qe = (q * jnp.exp(b)).astype(BF16)
        ke = (k * jnp.exp(-b)).astype(BF16)
        kd = (k * jnp.exp(b_end - b)).astype(BF16)
        a = lax.dot_general(qe, ke, nt, preferred_element_type=F32)
        a = jnp.where(keep, a, 0.0).astype(BF16)
        st = st_ref[...]
        o = (jnp.dot(a, v, preferred_element_type=F32)
             + lax.dot_general(qe, st.astype(BF16), nt, preferred_element_type=F32))
        vt = v.astype(F32).T.astype(BF16)
        st_ref[...] = st * jnp.exp(b_end) + jnp.dot(vt, kd, preferred_element_type=F32)
        if reverse:
            o = o + of_ref[0, rows, :]
            ms = jnp.mean(o * o, axis=-1, keepdims=True)
            o = o * lax.rsqrt(ms + NORM_EPS) * g_ref[...]
            gz = gz_ref[0, rows, :].astype(F32)
            y_ref[0, rows, :] = (o * (gz / (1.0 + jnp.exp(-gz)))).astype(y_ref.dtype)
        else:
            y_ref[0, rows, :] = o


def _gla_direction(proj3, lr3, w2pad, bias, reverse, o_fwd=None, norm_g=None):
    bs, seq, _ = proj3.shape
    nt = seq // GLA_TILE

    def tile(t):
        return nt - 1 - t if reverse else t

    def spec(width, off):
        return pl.BlockSpec((1, GLA_TILE, width),
                            lambda b, h, t: (b, tile(t), off // width + h))

    in_specs = [spec(GLA_DK, OFF_GQ), spec(GLA_DK, OFF_GK), spec(GLA_DV, OFF_GV),
                pl.BlockSpec((1, GLA_TILE, LR_PAD), lambda b, h, t: (b, tile(t), 0)),
                pl.BlockSpec((LR_PAD, GLA_DK), lambda b, h, t: (0, h)),
                pl.BlockSpec((1, GLA_DK), lambda b, h, t: (0, h))]
    args = [proj3, proj3, proj3, lr3, w2pad, bias]
    out_spec = pl.BlockSpec((1, GLA_TILE, GLA_DV), lambda b, h, t: (b, tile(t), h))
    if reverse:
        in_specs += [out_spec, spec(GLA_DV, OFF_GZ),
                     pl.BlockSpec((1, GLA_DV), lambda b, h, t: (0, 0))]
        args += [o_fwd, proj3, norm_g]
    return pl.pallas_call(
        functools.partial(_gla_kernel, reverse=reverse),
        grid=(bs, GLA_HEADS, nt),
        in_specs=in_specs,
        out_specs=out_spec,
        out_shape=jax.ShapeDtypeStruct((bs, seq, GLA_VAL), BF16 if reverse else F32),
        scratch_shapes=[pltpu.VMEM((GLA_DV, GLA_DK), F32)],
        compiler_params=_params(("parallel", "parallel", "arbitrary")),
        name="gla_bwd" if reverse else "gla_fwd",
    )(*args)


def _merge_kernel(ya_ref, yb_ref, wa_ref, wb_ref, ga_ref, gb_ref, o_ref):
    ua = jnp.dot(ya_ref[...], wa_ref[...], preferred_element_type=F32)
    ub = jnp.dot(yb_ref[...], wb_ref[...], preferred_element_type=F32)
    ga = ga_ref[...].astype(F32)
    gb = gb_ref[...].astype(F32)
    merged = ua / (1.0 + jnp.exp(-ga)) + ub / (1.0 + jnp.exp(-gb))
    o_ref[...] = merged.astype(o_ref.dtype)


def _merge(ya, yb, wa, wb, proj, tm=1024, tn=512):
    m = ya.shape[0]
    return pl.pallas_call(
        _merge_kernel,
        grid=(m // tm, D_MODEL // tn),
        in_specs=[pl.BlockSpec((tm, ATT_WIDTH), lambda i, j: (i, 0)),
                  pl.BlockSpec((tm, GLA_VAL), lambda i, j: (i, 0)),
                  pl.BlockSpec((ATT_WIDTH, tn), lambda i, j: (0, j)),
                  pl.BlockSpec((GLA_VAL, tn), lambda i, j: (0, j)),
                  pl.BlockSpec((tm, tn), lambda i, j: (i, OFF_MGA // tn + j)),
                  pl.BlockSpec((tm, tn), lambda i, j: (i, OFF_MGB // tn + j))],
        out_specs=pl.BlockSpec((tm, tn), lambda i, j: (i, j)),
        out_shape=jax.ShapeDtypeStruct((m, D_MODEL), BF16),
        compiler_params=_params(("parallel", "arbitrary")),
        name="merge",
    )(ya, yb, wa, wb, proj, proj)


def _out_kernel(x_ref, a_ref, w_ref, g_ref, o_ref, acc_ref, *, nj):
    j = pl.program_id(1)
    acc_ref[j] = x_ref[...] + jnp.dot(a_ref[...], w_ref[...], preferred_element_type=F32)

    @pl.when(j == nj - 1)
    def _():
        tn = acc_ref.shape[2]
        ss = jnp.zeros((acc_ref.shape[1], 1), F32)
        for jj in range(nj):
            t = acc_ref[jj]
            ss = ss + jnp.sum(t * t, axis=-1, keepdims=True)
        rs = lax.rsqrt(ss * (1.0 / D_MODEL) + NORM_EPS)
        for jj in range(nj):
            cols = slice(jj * tn, (jj + 1) * tn)
            o_ref[:, cols] = (acc_ref[jj] * rs) * g_ref[:, cols]


def _out_proj(x, merged, w_o, g, tm=512, tn=512):
    m = x.shape[0]
    nj = D_MODEL // tn
    return pl.pallas_call(
        functools.partial(_out_kernel, nj=nj),
        grid=(m // tm, nj),
        in_specs=[pl.BlockSpec((tm, tn), lambda i, j: (i, j)),
                  pl.BlockSpec((tm, D_MODEL), lambda i, j: (i, 0)),
                  pl.BlockSpec((D_MODEL, tn), lambda i, j: (0, j)),
                  pl.BlockSpec((1, D_MODEL), lambda i, j: (0, 0))],
        out_specs=pl.BlockSpec((tm, D_MODEL), lambda i, j: (i, 0)),
        out_shape=jax.ShapeDtypeStruct((m, D_MODEL), F32),
        scratch_shapes=[pltpu.VMEM((nj, tm, tn), F32)],
        compiler_params=_params(("parallel", "arbitrary")),
        name="out_proj",
    )(x, merged, w_o, g.reshape(1, D_MODEL))


def _class_major(t, dilation):
    b, s, c = t.shape
    return (t.reshape(b, s // dilation, dilation, c).transpose(0, 2, 1, 3)
            .reshape(b * dilation, s // dilation, c))


def _token_major(t, dilation, b):
    n, l, c = t.shape
    return (t.reshape(b, dilation, l, c).transpose(0, 2, 1, 3)
            .reshape(b * l * dilation, c))


def _layer(x, w, final_g):
    bs, seq, d = x.shape
    m = bs * seq
    x2 = x.reshape(m, d)
    h = _rmsnorm(x2, w["norm_g"])
    proj = _matmul(h, w["w_main"], 1024, 512, "in_proj")
    lr = _matmul(h, w["w_lr"], 1024, LR_PAD, "in_proj_lr")
    proj3 = proj.reshape(bs, seq, N_MAIN)

    outs, lses = [], []
    for g, (_, dilation) in enumerate(ATT_PATTERNS):
        base = g * 3 * ATT_WIDTH
        if dilation == 1:
            o, l = _attention(w["slopes"], proj3, base // ATT_HEAD_DIM, 1)
        else:
            qkv = _class_major(proj3[:, :, base:base + 3 * ATT_WIDTH], dilation)
            o, l = _attention(w["slopes"], qkv, 0, dilation)
        outs.append(_token_major(o, dilation, bs))
        lses.append(_token_major(l, dilation, bs))
    ya = _mix(outs, lses, proj)

    lr3 = lr.reshape(bs, seq, LR_PAD)
    o_f = _gla_direction(proj3, lr3, w["w2_f"], w["b_f"], False)
    yb = _gla_direction(proj3, lr3, w["w2_b"], w["b_b"], True, o_f, w["gla_norm_g"])

    merged = _merge(ya, yb.reshape(m, GLA_VAL), w["w_up_a"], w["w_up_b"], proj)
    y = _out_proj(x2, merged, w["w_o"], final_g)
    return y.reshape(bs, seq, d)


def _pad_rows(w2, row0):
    return jnp.zeros((LR_PAD, GLA_KEY), BF16).at[row0:row0 + GLA_RANK].set(w2.astype(BF16))


def kernel(x_prompt, x_sample, norm_g, w_in, gla_w2_f, gla_b_f, gla_w2_b, gla_b_b,
           gla_norm_g, w_up_a, w_up_b, w_o, final_norm_g):
    assert norm_g.shape[0] == 1, "single-layer kernel"
    w_in0 = w_in[0]
    w = {
        "norm_g": norm_g[0],
        "w_main": jnp.concatenate([w_in0[:, :OFF_LR], w_in0[:, OFF_LR + 2 * GLA_RANK:]],
                                  axis=1).astype(BF16),
        "w_lr": jnp.pad(w_in0[:, OFF_LR:OFF_LR + 2 * GLA_RANK],
                        ((0, 0), (0, LR_PAD - 2 * GLA_RANK))).astype(BF16),
        "w2_f": _pad_rows(gla_w2_f[0], 0),
        "w2_b": _pad_rows(gla_w2_b[0], GLA_RANK),
        "b_f": gla_b_f[0].reshape(1, GLA_KEY),
        "b_b": gla_b_b[0].reshape(1, GLA_KEY),
        "gla_norm_g": gla_norm_g[0].reshape(1, GLA_DV),
        "w_up_a": w_up_a[0].astype(BF16),
        "w_up_b": w_up_b[0].astype(BF16),
        "w_o": w_o[0].astype(BF16),
        "slopes": jnp.exp2(-8.0 * (jnp.arange(ATT_HEADS, dtype=F32) + 1.0) / ATT_HEADS),
    }
    return (_layer(x_prompt, w, final_norm_g), _layer(x_sample, w, final_norm_g))
```

```python
import functools

import jax
import jax.numpy as jnp
from jax import lax
from jax.experimental import pallas as pl
from jax.experimental.pallas import tpu as pltpu

F32 = jnp.float32
BF16 = jnp.bfloat16

D_MODEL = 4096
ATT_DILATIONS = (1, 4, 16)
N_GROUPS = 3
ATT_HEAD_DIM = 128
ATT_HEADS = 8
ATT_WIDTH = ATT_HEADS * ATT_HEAD_DIM
ATT_RADIUS = 64
GLA_HEADS = 4
GLA_KEY = 1024
GLA_VAL = 2048
GLA_DK = GLA_KEY // GLA_HEADS
GLA_DV = GLA_VAL // GLA_HEADS
GLA_RANK = 16
GLA_TAU = 16.0
GLA_CHUNK = 64
NORM_EPS = 1e-6
NEG_INF = -1e30

GROUP_COLS = 3 * ATT_WIDTH
OFF_AZ = N_GROUPS * GROUP_COLS
OFF_GQ = OFF_AZ + ATT_WIDTH
OFF_GK = OFF_GQ + GLA_KEY
OFF_GV = OFF_GK + GLA_KEY
OFF_GZ = OFF_GV + GLA_VAL
N_PROJ_A = OFF_GZ + GLA_VAL
OFF_LR = N_PROJ_A
OFF_MG = OFF_LR + 2 * GLA_RANK
N_PROJ_B = 2 * D_MODEL
LR_PAD = 128

VMEM_LIMIT = 56 * 1024 * 1024


def _params(sem):
    return pltpu.CompilerParams(dimension_semantics=sem, vmem_limit_bytes=VMEM_LIMIT)


def _rmsnorm_kernel(x_ref, g_ref, wlr_ref, h_ref, lr_ref):
    x = x_ref[...]
    ms = jnp.mean(x * x, axis=-1, keepdims=True)
    h = ((x * lax.rsqrt(ms + NORM_EPS)) * g_ref[...]).astype(h_ref.dtype)
    h_ref[...] = h
    lr_ref[...] = jnp.dot(h, wlr_ref[...], preferred_element_type=F32).astype(lr_ref.dtype)


def _rmsnorm(x, g, w_lr, tm=256):
    m, d = x.shape
    return pl.pallas_call(
        _rmsnorm_kernel,
        grid=(m // tm,),
        in_specs=[pl.BlockSpec((tm, d), lambda i: (i, 0)),
                  pl.BlockSpec((1, d), lambda i: (0, 0)),
                  pl.BlockSpec((d, LR_PAD), lambda i: (0, 0))],
        out_specs=[pl.BlockSpec((tm, d), lambda i: (i, 0)),
                   pl.BlockSpec((tm, LR_PAD), lambda i: (i, 0))],
        out_shape=[jax.ShapeDtypeStruct((m, d), BF16),
                   jax.ShapeDtypeStruct((m, LR_PAD), BF16)],
        compiler_params=_params(("parallel",)),
        name="rmsnorm",
    )(x, g.reshape(1, d), w_lr)


def _matmul_kernel(a_ref, b_ref, o_ref):
    o_ref[...] = jnp.dot(a_ref[...], b_ref[...],
                         preferred_element_type=F32).astype(o_ref.dtype)


def _matmul(a, b, tm, tn, name):
    m, k = a.shape
    n = b.shape[1]
    return pl.pallas_call(
        _matmul_kernel,
        grid=(m // tm, n // tn),
        in_specs=[pl.BlockSpec((tm, k), lambda i, j: (i, 0)),
                  pl.BlockSpec((k, tn), lambda i, j: (0, j))],
        out_specs=pl.BlockSpec((tm, tn), lambda i, j: (i, j)),
        out_shape=jax.ShapeDtypeStruct((m, n), BF16),
        compiler_params=_params(("parallel", "arbitrary")),
        name=name,
    )(a, b)


ATT_TILE = 2048
ATT_QB = 128
ATT_KW = ATT_QB + 2 * ATT_RADIUS
ATT_UNROLL = 4


def _attn_kernel(slopes_ref, q1_ref, k1_ref, v1_ref, q2_ref, k2_ref, v2_ref,
                 q3_ref, k3_ref, v3_ref, z_ref, y_ref, o_scr, l_scr, *, seq):
    h = pl.program_id(1)
    tile = pl.program_id(2)
    slope = slopes_ref[h]
    scale = ATT_HEAD_DIM ** -0.5
    nt = (((1,), (1,)), ((), ()))
    col_minus_row = {
        width: (lax.broadcasted_iota(jnp.int32, (ATT_QB, width), 1)
                - lax.broadcasted_iota(jnp.int32, (ATT_QB, width), 0))
        for width in {min(ATT_KW, seq // d) for d in ATT_DILATIONS}}

    def block(q, load_k, load_v, p0, length, dilation):
        width = min(ATT_KW, length)
        w0 = pl.multiple_of(jnp.clip(p0 - ATT_RADIUS, 0, length - width), ATT_RADIUS)
        s = lax.dot_general(q, load_k(w0, width), nt, preferred_element_type=F32) * scale
        dist = jnp.abs(col_minus_row[width] + (w0 - p0))
        s = jnp.where(dist <= ATT_RADIUS,
                      s - (slope * float(dilation)) * dist.astype(F32), NEG_INF)
        m = jnp.max(s, axis=-1, keepdims=True)
        e = jnp.exp(s - m)
        l = jnp.sum(e, axis=-1, keepdims=True)
        o = jnp.dot(e.astype(BF16), load_v(w0, width), preferred_element_type=F32) / l
        return o, jnp.broadcast_to(m + jnp.log(l), (ATT_QB, ATT_HEAD_DIM))

    def group(g, q_ref, k_ref, v_ref, dilation):
        length = seq // dilation
        per_class = ATT_TILE // dilation // ATT_QB
        n_blocks = dilation * per_class

        def body(it, carry):
            for u in range(ATT_UNROLL):
                idx = it * ATT_UNROLL + u
                r = idx // per_class
                j = idx % per_class
                row0 = pl.multiple_of(j * ATT_QB, ATT_QB)
                q = q_ref[0, r, pl.ds(row0, ATT_QB), :]
                o, lse = block(q,
                               lambda w0, n: k_ref[0, r, pl.ds(w0, n), :],
                               lambda w0, n: v_ref[0, r, pl.ds(w0, n), :],
                               tile * (ATT_TILE // dilation) + row0, length, dilation)
                if dilation == 1:
                    rows = pl.ds(row0, ATT_QB)
                else:
                    rows = pl.ds(row0 * dilation + r, ATT_QB, stride=dilation)
                o_scr[g, rows, :] = o
                l_scr[g, rows, :] = lse
            return carry

        lax.fori_loop(0, n_blocks // ATT_UNROLL, body, 0)

    group(0, q1_ref, k1_ref, v1_ref, ATT_DILATIONS[0])
    group(1, q2_ref, k2_ref, v2_ref, ATT_DILATIONS[1])
    group(2, q3_ref, k3_ref, v3_ref, ATT_DILATIONS[2])

    la, lb, lc = l_scr[0], l_scr[1], l_scr[2]
    m = jnp.maximum(jnp.maximum(la, lb), lc)
    ea, eb, ec = jnp.exp(la - m), jnp.exp(lb - m), jnp.exp(lc - m)
    mix = (ea * o_scr[0] + eb * o_scr[1] + ec * o_scr[2]) / (ea + eb + ec)
    z = z_ref[0].astype(F32)
    y_ref[0] = (mix * (z / (1.0 + jnp.exp(-z)))).astype(y_ref.dtype)


def _attention(slopes, proj3, cls4, cls16):
    bs, seq, _ = proj3.shape
    n_tiles = seq // ATT_TILE
    nat = proj3.reshape(bs, 1, seq, N_PROJ_A)

    def specs(dilation):
        tq = ATT_TILE // dilation
        q = pl.BlockSpec((1, dilation, tq, ATT_HEAD_DIM), lambda b, h, t: (b, 0, t, h))
        k = pl.BlockSpec((1, dilation, seq // dilation, ATT_HEAD_DIM),
                         lambda b, h, t: (b, 0, 0, ATT_HEADS + h))
        v = pl.BlockSpec((1, dilation, seq // dilation, ATT_HEAD_DIM),
                         lambda b, h, t: (b, 0, 0, 2 * ATT_HEADS + h))
        return [q, k, v]

    tok = pl.BlockSpec((1, ATT_TILE, ATT_HEAD_DIM),
                       lambda b, h, t: (b, t, OFF_AZ // ATT_HEAD_DIM + h))
    return pl.pallas_call(
        functools.partial(_attn_kernel, seq=seq),
        grid=(bs, ATT_HEADS, n_tiles),
        in_specs=([pl.BlockSpec(memory_space=pltpu.SMEM)]
                  + specs(1) + specs(4) + specs(16) + [tok]),
        out_specs=pl.BlockSpec((1, ATT_TILE, ATT_HEAD_DIM), lambda b, h, t: (b, t, h)),
        out_shape=jax.ShapeDtypeStruct((bs, seq, ATT_WIDTH), BF16),
        scratch_shapes=[pltpu.VMEM((N_GROUPS, ATT_TILE, ATT_HEAD_DIM), F32),
                        pltpu.VMEM((N_GROUPS, ATT_TILE, ATT_HEAD_DIM), F32)],
        compiler_params=_params(("parallel", "parallel", "arbitrary")),
        name="attention",
    )(slopes, nat, nat, nat, cls4, cls4, cls4, cls16, cls16, cls16, proj3)


GLA_TILE_CHUNKS = 8
GLA_TILE = GLA_TILE_CHUNKS * GLA_CHUNK


def _gla_kernel(*refs, reverse):
    if reverse:
        (q_ref, k_ref, v_ref, z_ref, w2_ref, b2_ref, of_ref, gz_ref, g_ref,
         y_ref, st_ref) = refs
    else:
        q_ref, k_ref, v_ref, z_ref, w2_ref, b2_ref, y_ref, st_ref = refs

    @pl.when(pl.program_id(2) == 0)
    def _():
        st_ref[...] = jnp.zeros_like(st_ref)

    x = jnp.dot(z_ref[0], w2_ref[...], preferred_element_type=F32) + b2_ref[...]
    log_a = (jnp.minimum(x, 0.0) - jnp.log1p(jnp.exp(-jnp.abs(x)))) * (1.0 / GLA_TAU)

    ri = lax.broadcasted_iota(jnp.int32, (GLA_CHUNK, GLA_CHUNK), 0)
    ci = lax.broadcasted_iota(jnp.int32, (GLA_CHUNK, GLA_CHUNK), 1)
    if reverse:
        cum = (ci >= ri).astype(BF16)
        keep = ci > ri
        last = 0
    else:
        cum = (ci <= ri).astype(BF16)
        keep = ci <= ri
        last = GLA_CHUNK - 1
    nt = (((1,), (1,)), ((), ()))

    order = range(GLA_TILE_CHUNKS)
    for c in (reversed(order) if reverse else order):
        rows = slice(c * GLA_CHUNK, (c + 1) * GLA_CHUNK)
        la = log_a[rows]
        hi = la.astype(BF16)
        r1 = la - hi.astype(F32)
        mid = r1.astype(BF16)
        lo = (r1 - mid.astype(F32)).astype(BF16)
        b = (jnp.dot(cum, hi, preferred_element_type=F32)
             + jnp.dot(cum, mid, preferred_element_type=F32)
             + jnp.dot(cum, lo, preferred_element_type=F32))
        b_end = b[last:last + 1, :]
        q = q_ref[0, rows, :].astype(F32) * (GLA_DK ** -0.5)
        k = k_ref[0, rows, :].astype(F32)
        v = v_ref[0, rows, :]
        qe = (q * jnp.exp(b)).astype(BF16)
        ke = (k * jnp.exp(-b)).astype(BF16)
        kd = (k * jnp.exp(b_end - b)).astype(BF16)
        a = lax.dot_general(qe, ke, nt, preferred_element_type=F32)
        a = jnp.where(keep, a, 0.0).astype(BF16)
        st = st_ref[...]
        o = (jnp.dot(a, v, preferred_element_type=F32)
             + lax.dot_general(qe, st.astype(BF16), nt, preferred_element_type=F32))
        vt = v.astype(F32).T.astype(BF16)
        st_ref[...] = st * jnp.exp(b_end) + jnp.dot(vt, kd, preferred_element_type=F32)
        if reverse:
            o = o + of_ref[0, rows, :]
            ms = jnp.mean(o * o, axis=-1, keepdims=True)
            o = o * lax.rsqrt(ms + NORM_EPS) * g_ref[...]
            gz = gz_ref[0, rows, :].astype(F32)
            y_ref[0, rows, :] = (o * (gz / (1.0 + jnp.exp(-gz)))).astype(y_ref.dtype)
        else:
            y_ref[0, rows, :] = o


def _gla_direction(proj3, lr3, w2pad, bias, reverse, o_fwd=None, norm_g=None):
    bs, seq, _ = proj3.shape
    nt = seq // GLA_TILE

    def tile(t):
        return nt - 1 - t if reverse else t

    def spec(width, off):
        return pl.BlockSpec((1, GLA_TILE, width),
                            lambda b, h, t: (b, tile(t), off // width + h))

    in_specs = [spec(GLA_DK, OFF_GQ), spec(GLA_DK, OFF_GK), spec(GLA_DV, OFF_GV),
                pl.BlockSpec((1, GLA_TILE, LR_PAD), lambda b, h, t: (b, tile(t), 0)),
                pl.BlockSpec((LR_PAD, GLA_DK), lambda b, h, t: (0, h)),
                pl.BlockSpec((1, GLA_DK), lambda b, h, t: (0, h))]
    args = [proj3, proj3, proj3, lr3, w2pad, bias]
    out_spec = pl.BlockSpec((1, GLA_TILE, GLA_DV), lambda b, h, t: (b, tile(t), h))
    if reverse:
        in_specs += [out_spec, spec(GLA_DV, OFF_GZ),
                     pl.BlockSpec((1, GLA_DV), lambda b, h, t: (0, 0))]
        args += [o_fwd, proj3, norm_g]
    return pl.pallas_call(
        functools.partial(_gla_kernel, reverse=reverse),
        grid=(bs, GLA_HEADS, nt),
        in_specs=in_specs,
        out_specs=out_spec,
        out_shape=jax.ShapeDtypeStruct((bs, seq, GLA_VAL), BF16 if reverse else F32),
        scratch_shapes=[pltpu.VMEM((GLA_DV, GLA_DK), F32)],
        compiler_params=_params(("parallel", "parallel", "arbitrary")),
        name="gla_bwd" if reverse else "gla_fwd",
    )(*args)


def _merge_kernel(ya_ref, yb_ref, wa_ref, wb_ref, ga_ref, gb_ref, o_ref):
    ua = jnp.dot(ya_ref[...], wa_ref[...], preferred_element_type=F32)
    ub = jnp.dot(yb_ref[...], wb_ref[...], preferred_element_type=F32)
    ga = ga_ref[...].astype(F32)
    gb = gb_ref[...].astype(F32)
    merged = ua / (1.0 + jnp.exp(-ga)) + ub / (1.0 + jnp.exp(-gb))
    o_ref[...] = merged.astype(o_ref.dtype)


def _merge(ya, yb, wa, wb, gates, tm=1024, tn=512):
    m = ya.shape[0]
    return pl.pallas_call(
        _merge_kernel,
        grid=(m // tm, D_MODEL // tn),
        in_specs=[pl.BlockSpec((tm, ATT_WIDTH), lambda i, j: (i, 0)),
                  pl.BlockSpec((tm, GLA_VAL), lambda i, j: (i, 0)),
                  pl.BlockSpec((ATT_WIDTH, tn), lambda i, j: (0, j)),
                  pl.BlockSpec((GLA_VAL, tn), lambda i, j: (0, j)),
                  pl.BlockSpec((tm, tn), lambda i, j: (i, j)),
                  pl.BlockSpec((tm, tn), lambda i, j: (i, D_MODEL // tn + j))],
        out_specs=pl.BlockSpec((tm, tn), lambda i, j: (i, j)),
        out_shape=jax.ShapeDtypeStruct((m, D_MODEL), BF16),
        compiler_params=_params(("parallel", "arbitrary")),
        name="merge",
    )(ya, yb, wa, wb, gates, gates)


def _out_kernel(x_ref, a_ref, w_ref, g_ref, o_ref, acc_ref, *, nj):
    j = pl.program_id(1)
    acc_ref[j] = x_ref[...] + jnp.dot(a_ref[...], w_ref[...], preferred_element_type=F32)

    @pl.when(j == nj - 1)
    def _():
        tn = acc_ref.shape[2]
        ss = jnp.zeros((acc_ref.shape[1], 1), F32)
        for jj in range(nj):
            t = acc_ref[jj]
            ss = ss + jnp.sum(t * t, axis=-1, keepdims=True)
        rs = lax.rsqrt(ss * (1.0 / D_MODEL) + NORM_EPS)
        for jj in range(nj):
            cols = slice(jj * tn, (jj + 1) * tn)
            o_ref[:, cols] = (acc_ref[jj] * rs) * g_ref[:, cols]


def _out_proj(x, merged, w_o, g, tm=512, tn=512):
    m = x.shape[0]
    nj = D_MODEL // tn
    return pl.pallas_call(
        functools.partial(_out_kernel, nj=nj),
        grid=(m // tm, nj),
        in_specs=[pl.BlockSpec((tm, tn), lambda i, j: (i, j)),
                  pl.BlockSpec((tm, D_MODEL), lambda i, j: (i, 0)),
                  pl.BlockSpec((D_MODEL, tn), lambda i, j: (0, j)),
                  pl.BlockSpec((1, D_MODEL), lambda i, j: (0, 0))],
        out_specs=pl.BlockSpec((tm, D_MODEL), lambda i, j: (i, 0)),
        out_shape=jax.ShapeDtypeStruct((m, D_MODEL), F32),
        scratch_shapes=[pltpu.VMEM((nj, tm, tn), F32)],
        compiler_params=_params(("parallel", "arbitrary")),
        name="out_proj",
    )(x, merged, w_o, g.reshape(1, D_MODEL))


def _class_major(t, dilation):
    b, s, c = t.shape
    return t.reshape(b, s // dilation, dilation, c).transpose(0, 2, 1, 3)


def _layer(x, w, final_g):
    bs, seq, d = x.shape
    m = bs * seq
    x2 = x.reshape(m, d)
    h, lr = _rmsnorm(x2, w["norm_g"], w["w_lr"])
    proj = _matmul(h, w["w_a"], 1024, 512, "in_proj_a")
    gates = _matmul(h, w["w_b"], 1024, 512, "in_proj_b")
    proj3 = proj.reshape(bs, seq, N_PROJ_A)

    cls = [_class_major(proj3[:, :, g * GROUP_COLS:(g + 1) * GROUP_COLS], dil)
           for g, dil in enumerate(ATT_DILATIONS) if dil > 1]
    ya = _attention(w["slopes"], proj3, *cls)

    lr3 = lr.reshape(bs, seq, LR_PAD)
    o_f = _gla_direction(proj3, lr3, w["w2_f"], w["b_f"], False)
    yb = _gla_direction(proj3, lr3, w["w2_b"], w["b_b"], True, o_f, w["gla_norm_g"])

    merged = _merge(ya.reshape(m, ATT_WIDTH), yb.reshape(m, GLA_VAL),
                    w["w_up_a"], w["w_up_b"], gates)
    y = _out_proj(x2, merged, w["w_o"], final_g)
    return y.reshape(bs, seq, d)


def _pad_rows(w2, row0):
    return jnp.zeros((LR_PAD, GLA_KEY), BF16).at[row0:row0 + GLA_RANK].set(w2.astype(BF16))


def kernel(x_prompt, x_sample, norm_g, w_in, gla_w2_f, gla_b_f, gla_w2_b, gla_b_b,
           gla_norm_g, w_up_a, w_up_b, w_o, final_norm_g):
    assert norm_g.shape[0] == 1, "single-layer kernel"
    w_in0 = w_in[0]
    w = {
        "norm_g": norm_g[0],
        "w_a": w_in0[:, :N_PROJ_A].astype(BF16),
        "w_b": w_in0[:, OFF_MG:].astype(BF16),
        "w_lr": jnp.pad(w_in0[:, OFF_LR:OFF_MG],
                        ((0, 0), (0, LR_PAD - 2 * GLA_RANK))).astype(BF16),
        "w2_f": _pad_rows(gla_w2_f[0], 0),
        "w2_b": _pad_rows(gla_w2_b[0], GLA_RANK),
        "b_f": gla_b_f[0].reshape(1, GLA_KEY),
        "b_b": gla_b_b[0].reshape(1, GLA_KEY),
        "gla_norm_g": gla_norm_g[0].reshape(1, GLA_DV),
        "w_up_a": w_up_a[0].astype(BF16),
        "w_up_b": w_up_b[0].astype(BF16),
        "w_o": w_o[0].astype(BF16),
        "slopes": jnp.exp2(-8.0 * (jnp.arange(ATT_HEADS, dtype=F32) + 1.0) / ATT_HEADS),
    }
    return (_layer(x_prompt, w, final_norm_g), _layer(x_sample, w, final_norm_g))
```

```python
import functools

import jax
import jax.numpy as jnp
from jax import lax
from jax.experimental import pallas as pl
from jax.experimental.pallas import tpu as pltpu

F32 = jnp.float32
BF16 = jnp.bfloat16

D_MODEL = 4096
ATT_DILATIONS = (1, 4, 16)
N_GROUPS = 3
ATT_HEAD_DIM = 128
ATT_HEADS = 8
ATT_WIDTH = ATT_HEADS * ATT_HEAD_DIM
ATT_RADIUS = 64
GLA_HEADS = 4
GLA_KEY = 1024
GLA_VAL = 2048
GLA_DK = GLA_KEY // GLA_HEADS
GLA_DV = GLA_VAL // GLA_HEADS
GLA_RANK = 16
GLA_TAU = 16.0
GLA_CHUNK = 64
NORM_EPS = 1e-6
NEG_INF = -1e30

GROUP_COLS = 3 * ATT_WIDTH
W_OFF_AZ = N_GROUPS * GROUP_COLS
W_OFF_LR = W_OFF_AZ + ATT_WIDTH + 2 * GLA_KEY + 2 * GLA_VAL
W_OFF_MG = W_OFF_LR + 2 * GLA_RANK
OFF_AZ = GROUP_COLS
OFF_GQ = OFF_AZ + ATT_WIDTH
OFF_GK = OFF_GQ + GLA_KEY
OFF_GV = OFF_GK + GLA_KEY
OFF_GZ = OFF_GV + GLA_VAL
N_PROJ = OFF_GZ + GLA_VAL
LR_PAD = 128

VMEM_LIMIT = 56 * 1024 * 1024


def _params(sem):
    return pltpu.CompilerParams(dimension_semantics=sem, vmem_limit_bytes=VMEM_LIMIT)


LANES = 128
NORM_ROWS = 256
NT_DIMS = (((1,), (1,)), ((), ()))


def _rmsnorm_kernel(x_ref, g_ref, wlr_ref, h_ref, h4_ref, h16_ref, lr_ref, hf_ref):
    x = x_ref[0]
    ms = jnp.mean(x * x, axis=-1, keepdims=True)
    hf = (x * lax.rsqrt(ms + NORM_EPS)) * g_ref[...]
    h = hf.astype(h_ref.dtype)
    h_ref[0] = h
    lr_ref[0] = lax.dot_general(h, wlr_ref[...], NT_DIMS,
                                preferred_element_type=F32).astype(lr_ref.dtype)
    for c in range(hf_ref.shape[0]):
        cols = slice(c * LANES, (c + 1) * LANES)
        hf_ref[c] = hf[:, cols]
        for out_ref, dilation in ((h4_ref, 4), (h16_ref, 16)):
            for r in range(dilation):
                rows = pl.ds(r, NORM_ROWS // dilation, stride=dilation)
                out_ref[0, r, :, cols] = hf_ref[c, rows, :].astype(out_ref.dtype)


def _rmsnorm(x, g, wt_lr):
    bs, seq, d = x.shape
    tm = NORM_ROWS

    def cls_spec(dilation):
        return pl.BlockSpec((1, dilation, tm // dilation, d), lambda b, i: (b, 0, i, 0))

    def cls_shape(dilation):
        return jax.ShapeDtypeStruct((bs, dilation, seq // dilation, d), BF16)

    return pl.pallas_call(
        _rmsnorm_kernel,
        grid=(bs, seq // tm),
        in_specs=[pl.BlockSpec((1, tm, d), lambda b, i: (b, i, 0)),
                  pl.BlockSpec((1, d), lambda b, i: (0, 0)),
                  pl.BlockSpec((LR_PAD, d), lambda b, i: (0, 0))],
        out_specs=[pl.BlockSpec((1, tm, d), lambda b, i: (b, i, 0)),
                   cls_spec(4), cls_spec(16),
                   pl.BlockSpec((1, tm, LR_PAD), lambda b, i: (b, i, 0))],
        out_shape=[jax.ShapeDtypeStruct((bs, seq, d), BF16), cls_shape(4), cls_shape(16),
                   jax.ShapeDtypeStruct((bs, seq, LR_PAD), BF16)],
        scratch_shapes=[pltpu.VMEM((d // LANES, tm, LANES), F32)],
        compiler_params=_params(("parallel", "parallel")),
        name="rmsnorm",
    )(x, g.reshape(1, d), wt_lr)


def _matmul_nt_kernel(a_ref, bt_ref, o_ref):
    o_ref[...] = lax.dot_general(a_ref[...], bt_ref[...], NT_DIMS,
                                 preferred_element_type=F32).astype(o_ref.dtype)


def _matmul_nt(a, bt, n_out, row_block, name, tm=1024, tn=512):
    m, k = a.shape
    return pl.pallas_call(
        _matmul_nt_kernel,
        grid=(m // tm, n_out // tn),
        in_specs=[pl.BlockSpec((tm, k), lambda i, j: (i, 0)),
                  pl.BlockSpec((tn, k), lambda i, j: (row_block(j), 0))],
        out_specs=pl.BlockSpec((tm, tn), lambda i, j: (i, j)),
        out_shape=jax.ShapeDtypeStruct((m, n_out), BF16),
        compiler_params=_params(("parallel", "arbitrary")),
        name=name,
    )(a, bt)


ATT_TILE = 2048
ATT_QB = 128
ATT_KW = ATT_QB + 2 * ATT_RADIUS
ATT_WINDOW_OFFSETS = (-ATT_RADIUS, 0, -2 * ATT_RADIUS)


def _attn_kernel(slopes_ref, q1_ref, k1_ref, v1_ref, q2_ref, k2_ref, v2_ref,
                 q3_ref, k3_ref, v3_ref, z_ref, y_ref, o_scr, l_scr, bias_scr, *, seq):
    h = pl.program_id(1)
    tile = pl.program_id(2)
    slope = slopes_ref[h]
    scale = ATT_HEAD_DIM ** -0.5

    col_minus_row = (lax.broadcasted_iota(jnp.int32, (ATT_QB, ATT_KW), 1)
                     - lax.broadcasted_iota(jnp.int32, (ATT_QB, ATT_KW), 0))
    for g, dilation in enumerate(ATT_DILATIONS):
        for v, off in enumerate(ATT_WINDOW_OFFSETS):
            dist = jnp.abs(col_minus_row + off)
            bias_scr[g * len(ATT_WINDOW_OFFSETS) + v] = jnp.where(
                dist <= ATT_RADIUS, -(slope * float(dilation)) * dist.astype(F32), NEG_INF)

    for g, (q_ref, k_ref, v_ref) in enumerate(((q1_ref, k1_ref, v1_ref),
                                               (q2_ref, k2_ref, v2_ref),
                                               (q3_ref, k3_ref, v3_ref))):
        dilation = ATT_DILATIONS[g]
        length = seq // dilation
        per_class = ATT_TILE // dilation // ATT_QB
        width = min(ATT_KW, length)
        for r in range(dilation):
            for j in range(per_class):
                row0 = j * ATT_QB
                p0 = tile * (ATT_TILE // dilation) + row0
                w0 = pl.multiple_of(jnp.clip(p0 - ATT_RADIUS, 0, length - width),
                                    ATT_RADIUS)
                off = w0 - p0
                variant = jnp.where(off == ATT_WINDOW_OFFSETS[1], 1,
                                    jnp.where(off == ATT_WINDOW_OFFSETS[2], 2, 0))
                bias = bias_scr[g * len(ATT_WINDOW_OFFSETS) + variant][:, :width]
                q = q_ref[0, r, row0:row0 + ATT_QB, :]
                s = lax.dot_general(q, k_ref[0, r, pl.ds(w0, width), :], NT_DIMS,
                                    preferred_element_type=F32) * scale + bias
                m = jnp.max(s, axis=-1, keepdims=True)
                e = jnp.exp(s - m)
                l = jnp.sum(e, axis=-1, keepdims=True)
                o = jnp.dot(e.astype(BF16), v_ref[0, r, pl.ds(w0, width), :],
                            preferred_element_type=F32) / l
                lse = jnp.broadcast_to(m + jnp.log(l), (ATT_QB, ATT_HEAD_DIM))
                if dilation == 1:
                    rows = pl.ds(row0, ATT_QB)
                else:
                    rows = pl.ds(row0 * dilation + r, ATT_QB, stride=dilation)
                o_scr[g, rows, :] = o
                l_scr[g, rows, :] = lse

    la, lb, lc = l_scr[0], l_scr[1], l_scr[2]
    m = jnp.maximum(jnp.maximum(la, lb), lc)
    ea, eb, ec = jnp.exp(la - m), jnp.exp(lb - m), jnp.exp(lc - m)
    mix = (ea * o_scr[0] + eb * o_scr[1] + ec * o_scr[2]) / (ea + eb + ec)
    z = z_ref[0].astype(F32)
    y_ref[0] = (mix * (z / (1.0 + jnp.exp(-z)))).astype(y_ref.dtype)


def _attention(slopes, proj3, cls4, cls16):
    bs, seq, _ = proj3.shape
    n_tiles = seq // ATT_TILE
    nat = proj3.reshape(bs, 1, seq, N_PROJ)

    def specs(dilation):
        tq = ATT_TILE // dilation
        q = pl.BlockSpec((1, dilation, tq, ATT_HEAD_DIM), lambda b, h, t: (b, 0, t, h))
        k = pl.BlockSpec((1, dilation, seq // dilation, ATT_HEAD_DIM),
                         lambda b, h, t: (b, 0, 0, ATT_HEADS + h))
        v = pl.BlockSpec((1, dilation, seq // dilation, ATT_HEAD_DIM),
                         lambda b, h, t: (b, 0, 0, 2 * ATT_HEADS + h))
        return [q, k, v]

    tok = pl.BlockSpec((1, ATT_TILE, ATT_HEAD_DIM),
                       lambda b, h, t: (b, t, OFF_AZ // ATT_HEAD_DIM + h))
    return pl.pallas_call(
        functools.partial(_attn_kernel, seq=seq),
        grid=(bs, ATT_HEADS, n_tiles),
        in_specs=([pl.BlockSpec(memory_space=pltpu.SMEM)]
                  + specs(1) + specs(4) + specs(16) + [tok]),
        out_specs=pl.BlockSpec((1, ATT_TILE, ATT_HEAD_DIM), lambda b, h, t: (b, t, h)),
        out_shape=jax.ShapeDtypeStruct((bs, seq, ATT_WIDTH), BF16),
        scratch_shapes=[pltpu.VMEM((N_GROUPS, ATT_TILE, ATT_HEAD_DIM), F32),
                        pltpu.VMEM((N_GROUPS, ATT_TILE, ATT_HEAD_DIM), F32),
                        pltpu.VMEM((N_GROUPS * len(ATT_WINDOW_OFFSETS), ATT_QB, ATT_KW),
                                   F32)],
        compiler_params=_params(("parallel", "parallel", "arbitrary")),
        name="attention",
    )(slopes, nat, nat, nat, cls4, cls4, cls4, cls16, cls16, cls16, proj3)


GLA_TILE_CHUNKS = 8
GLA_TILE = GLA_TILE_CHUNKS * GLA_CHUNK


def _gla_kernel(*refs, reverse):
    if reverse:
        (q_ref, k_ref, v_ref, z_ref, w2_ref, b2_ref, of_ref, gz_ref, g_ref,
         y_ref, st_ref) = refs
    else:
        q_ref, k_ref, v_ref, z_ref, w2_ref, b2_ref, y_ref, st_ref = refs

    @pl.when(pl.program_id(2) == 0)
    def _():
        st_ref[...] = jnp.zeros_like(st_ref)

    x = jnp.dot(z_ref[0], w2_ref[...], preferred_element_type=F32) + b2_ref[...]
    log_a = (jnp.minimum(x, 0.0) - jnp.log1p(jnp.exp(-jnp.abs(x)))) * (1.0 / GLA_TAU)

    ri = lax.broadcasted_iota(jnp.int32, (GLA_CHUNK, GLA_CHUNK), 0)
    ci = lax.broadcasted_iota(jnp.int32, (GLA_CHUNK, GLA_CHUNK), 1)
    if reverse:
        cum = (ci >= ri).astype(BF16)
        keep = ci > ri
        last = 0
    else:
        cum = (ci <= ri).astype(BF16)
        keep = ci <= ri
        last = GLA_CHUNK - 1
    nt = (((1,), (1,)), ((), ()))

    order = range(GLA_TILE_CHUNKS)
    for c in (reversed(order) if reverse else order):
        rows = slice(c * GLA_CHUNK, (c + 1) * GLA_CHUNK)
        la = log_a[rows]
        hi = la.astype(BF16)
        r1 = la - hi.astype(F32)
        mid = r1.astype(BF16)
        lo = (r1 - mid.astype(F32)).astype(BF16)
        b = (jnp.dot(cum, hi, preferred_element_type=F32)
             + jnp.dot(cum, mid, preferred_element_type=F32)
             + jnp.dot(cum, lo, preferred_element_type=F32))
        b_end = b[last:last + 1, :]
        q = q_ref[0, rows, :].astype(F32) * (GLA_DK ** -0.5)
        k = k_ref[0, rows, :].astype(F32)
        v = v_ref[0, rows, :]
        qe = (q * jnp.exp(b)).astype(BF16)
        ke = (k * jnp.exp(-b)).astype(BF16)
        kd = (k * jnp.exp(b_end - b)).astype(BF16)
        a = lax.dot_general(qe, ke, nt, preferred_element_type=F32)
        a = jnp.where(keep, a, 0.0).astype(BF16)
        st = st_ref[...]
        o = (jnp.dot(a, v, preferred_element_type=F32)
             + lax.dot_general(qe, st.astype(BF16), nt, preferred_element_type=F32))
        vt = v.astype(F32).T.astype(BF16)
        st_ref[...] = st * jnp.exp(b_end) + jnp.dot(vt, kd, preferred_element_type=F32)
        if reverse:
            o = o + of_ref[0, rows, :]
            ms = jnp.mean(o * o, axis=-1, keepdims=True)
            o = o * lax.rsqrt(ms + NORM_EPS) * g_ref[...]
            gz = gz_ref[0, rows, :].astype(F32)
            y_ref[0, rows, :] = (o * (gz / (1.0 + jnp.exp(-gz)))).astype(y_ref.dtype)
        else:
            y_ref[0, rows, :] = o


def _gla_direction(proj3, lr3, w2pad, bias, reverse, o_fwd=None, norm_g=None):
    bs, seq, _ = proj3.shape
    nt = seq // GLA_TILE

    def tile(t):
        return nt - 1 - t if reverse else t

    def spec(width, off):
        return pl.BlockSpec((1, GLA_TILE, width),
                            lambda b, h, t: (b, tile(t), off // width + h))

    in_specs = [spec(GLA_DK, OFF_GQ), spec(GLA_DK, OFF_GK), spec(GLA_DV, OFF_GV),
                pl.BlockSpec((1, GLA_TILE, LR_PAD), lambda b, h, t: (b, tile(t), 0)),
                pl.BlockSpec((LR_PAD, GLA_DK), lambda b, h, t: (0, h)),
                pl.BlockSpec((1, GLA_DK), lambda b, h, t: (0, h))]
    args = [proj3, proj3, proj3, lr3, w2pad, bias]
    out_spec = pl.BlockSpec((1, GLA_TILE, GLA_DV), lambda b, h, t: (b, tile(t), h))
    if reverse:
        in_specs += [out_spec, spec(GLA_DV, OFF_GZ),
                     pl.BlockSpec((1, GLA_DV), lambda b, h, t: (0, 0))]
        args += [o_fwd, proj3, norm_g]
    return pl.pallas_call(
        functools.partial(_gla_kernel, reverse=reverse),
        grid=(bs, GLA_HEADS, nt),
        in_specs=in_specs,
        out_specs=out_spec,
        out_shape=jax.ShapeDtypeStruct((bs, seq, GLA_VAL), BF16 if reverse else F32),
        scratch_shapes=[pltpu.VMEM((GLA_DV, GLA_DK), F32)],
        compiler_params=_params(("parallel", "parallel", "arbitrary")),
        name="gla_bwd" if reverse else "gla_fwd",
    )(*args)


def _merge_kernel(ya_ref, yb_ref, wa_ref, wb_ref, ga_ref, gb_ref, o_ref):
    ua = jnp.dot(ya_ref[...], wa_ref[...], preferred_element_type=F32)
    ub = jnp.dot(yb_ref[...], wb_ref[...], preferred_element_type=F32)
    ga = ga_ref[...].astype(F32)
    gb = gb_ref[...].astype(F32)
    merged = ua / (1.0 + jnp.exp(-ga)) + ub / (1.0 + jnp.exp(-gb))
    o_ref[...] = merged.astype(o_ref.dtype)


def _merge(ya, yb, wa, wb, gates, tm=1024, tn=512):
    m = ya.shape[0]
    return pl.pallas_call(
        _merge_kernel,
        grid=(m // tm, D_MODEL // tn),
        in_specs=[pl.BlockSpec((tm, ATT_WIDTH), lambda i, j: (i, 0)),
                  pl.BlockSpec((tm, GLA_VAL), lambda i, j: (i, 0)),
                  pl.BlockSpec((ATT_WIDTH, tn), lambda i, j: (0, j)),
                  pl.BlockSpec((GLA_VAL, tn), lambda i, j: (0, j)),
                  pl.BlockSpec((tm, tn), lambda i, j: (i, j)),
                  pl.BlockSpec((tm, tn), lambda i, j: (i, D_MODEL // tn + j))],
        out_specs=pl.BlockSpec((tm, tn), lambda i, j: (i, j)),
        out_shape=jax.ShapeDtypeStruct((m, D_MODEL), BF16),
        compiler_params=_params(("parallel", "arbitrary")),
        name="merge",
    )(ya, yb, wa, wb, gates, gates)


def _out_kernel(x_ref, a_ref, w_ref, g_ref, o_ref, acc_ref, *, nj):
    j = pl.program_id(1)
    acc_ref[j] = x_ref[...] + jnp.dot(a_ref[...], w_ref[...], preferred_element_type=F32)

    @pl.when(j == nj - 1)
    def _():
        tn = acc_ref.shape[2]
        ss = jnp.zeros((acc_ref.shape[1], 1), F32)
        for jj in range(nj):
            t = acc_ref[jj]
            ss = ss + jnp.sum(t * t, axis=-1, keepdims=True)
        rs = lax.rsqrt(ss * (1.0 / D_MODEL) + NORM_EPS)
        for jj in range(nj):
            cols = slice(jj * tn, (jj + 1) * tn)
            o_ref[:, cols] = (acc_ref[jj] * rs) * g_ref[:, cols]


def _out_proj(x, merged, w_o, g, tm=512, tn=512):
    m = x.shape[0]
    nj = D_MODEL // tn
    return pl.pallas_call(
        functools.partial(_out_kernel, nj=nj),
        grid=(m // tm, nj),
        in_specs=[pl.BlockSpec((tm, tn), lambda i, j: (i, j)),
                  pl.BlockSpec((tm, D_MODEL), lambda i, j: (i, 0)),
                  pl.BlockSpec((D_MODEL, tn), lambda i, j: (0, j)),
                  pl.BlockSpec((1, D_MODEL), lambda i, j: (0, 0))],
        out_specs=pl.BlockSpec((tm, D_MODEL), lambda i, j: (i, 0)),
        out_shape=jax.ShapeDtypeStruct((m, D_MODEL), F32),
        scratch_shapes=[pltpu.VMEM((nj, tm, tn), F32)],
        compiler_params=_params(("parallel", "arbitrary")),
        name="out_proj",
    )(x, merged, w_o, g.reshape(1, D_MODEL))


IN_PROJ_TN = 512


def _layer(x, w, final_g):
    bs, seq, d = x.shape
    m = bs * seq
    x2 = x.reshape(m, d)
    h, h4, h16, lr3 = _rmsnorm(x, w["norm_g"], w["wt_lr"])
    group_tiles = GROUP_COLS // IN_PROJ_TN
    proj = _matmul_nt(h.reshape(m, d), w["wt_a"], N_PROJ,
                      lambda j: jnp.where(j < group_tiles, j, j + 2 * group_tiles),
                      "in_proj")
    proj3 = proj.reshape(bs, seq, N_PROJ)
    cls = []
    for g, (dilation, hc) in enumerate(((4, h4), (16, h16)), start=1):
        p = _matmul_nt(hc.reshape(m, d), w["wt_a"], GROUP_COLS,
                       lambda j, g=g: j + g * group_tiles, f"in_proj_d{dilation}")
        cls.append(p.reshape(bs, dilation, seq // dilation, GROUP_COLS))
    gates = _matmul_nt(h.reshape(m, d), w["wt_b"], 2 * D_MODEL, lambda j: j, "in_proj_gates")
    ya = _attention(w["slopes"], proj3, *cls)

    o_f = _gla_direction(proj3, lr3, w["w2_f"], w["b_f"], False)
    yb = _gla_direction(proj3, lr3, w["w2_b"], w["b_b"], True, o_f, w["gla_norm_g"])

    merged = _merge(ya.reshape(m, ATT_WIDTH), yb.reshape(m, GLA_VAL),
                    w["w_up_a"], w["w_up_b"], gates)
    y = _out_proj(x2, merged, w["w_o"], final_g)
    return y.reshape(bs, seq, d)


def _pad_rows(w2, row0):
    return jnp.zeros((LR_PAD, GLA_KEY), BF16).at[row0:row0 + GLA_RANK].set(w2.astype(BF16))


def kernel(x_prompt, x_sample, norm_g, w_in, gla_w2_f, gla_b_f, gla_w2_b, gla_b_b,
           gla_norm_g, w_up_a, w_up_b, w_o, final_norm_g):
    assert norm_g.shape[0] == 1, "single-layer kernel"
    wt = w_in[0].T
    w = {
        "norm_g": norm_g[0],
        "wt_a": wt[:W_OFF_LR].astype(BF16),
        "wt_b": wt[W_OFF_MG:].astype(BF16),
        "wt_lr": jnp.pad(wt[W_OFF_LR:W_OFF_MG],
                         ((0, LR_PAD - 2 * GLA_RANK), (0, 0))).astype(BF16),
        "w2_f": _pad_rows(gla_w2_f[0], 0),
        "w2_b": _pad_rows(gla_w2_b[0], GLA_RANK),
        "b_f": gla_b_f[0].reshape(1, GLA_KEY),
        "b_b": gla_b_b[0].reshape(1, GLA_KEY),
        "gla_norm_g": gla_norm_g[0].reshape(1, GLA_DV),
        "w_up_a": w_up_a[0].astype(BF16),
        "w_up_b": w_up_b[0].astype(BF16),
        "w_o": w_o[0].astype(BF16),
        "slopes": jnp.exp2(-8.0 * (jnp.arange(ATT_HEADS, dtype=F32) + 1.0) / ATT_HEADS),
    }
    return (_layer(x_prompt, w, final_norm_g), _layer(x_sample, w, final_norm_g))
```

```python
import functools

import jax
import jax.numpy as jnp
from jax import lax
from jax.experimental import pallas as pl
from jax.experimental.pallas import tpu as pltpu

F32 = jnp.float32
BF16 = jnp.bfloat16

D_MODEL = 4096
ATT_DILATIONS = (1, 4, 16)
N_GROUPS = 3
ATT_HEAD_DIM = 128
ATT_HEADS = 8
ATT_WIDTH = ATT_HEADS * ATT_HEAD_DIM
ATT_RADIUS = 64
GLA_HEADS = 4
GLA_KEY = 1024
GLA_VAL = 2048
GLA_DK = GLA_KEY // GLA_HEADS
GLA_DV = GLA_VAL // GLA_HEADS
GLA_RANK = 16
GLA_TAU = 16.0
GLA_CHUNK = 64
NORM_EPS = 1e-6
NEG_INF = -1e30

GROUP_COLS = 3 * ATT_WIDTH
W_OFF_AZ = N_GROUPS * GROUP_COLS
W_OFF_LR = W_OFF_AZ + ATT_WIDTH + 2 * GLA_KEY + 2 * GLA_VAL
W_OFF_MG = W_OFF_LR + 2 * GLA_RANK
OFF_AZ = GROUP_COLS
OFF_GQ = OFF_AZ + ATT_WIDTH
OFF_GK = OFF_GQ + GLA_KEY
OFF_GV = OFF_GK + GLA_KEY
OFF_GZ = OFF_GV + GLA_VAL
N_PROJ = OFF_GZ + GLA_VAL
LR_PAD = 128

VMEM_LIMIT = 56 * 1024 * 1024


def _params(sem):
    return pltpu.CompilerParams(dimension_semantics=sem, vmem_limit_bytes=VMEM_LIMIT)


LANES = 128
NORM_ROWS = 256
NT_DIMS = (((1,), (1,)), ((), ()))


def _rmsnorm_kernel(x_ref, g_ref, wlr_ref, h_ref, h4_ref, h16_ref, lr_ref, hf_ref):
    x = x_ref[0]
    ms = jnp.mean(x * x, axis=-1, keepdims=True)
    hf = (x * lax.rsqrt(ms + NORM_EPS)) * g_ref[...]
    h = hf.astype(h_ref.dtype)
    h_ref[0] = h
    lr_ref[0] = lax.dot_general(h, wlr_ref[...], NT_DIMS,
                                preferred_element_type=F32).astype(lr_ref.dtype)
    for c in range(hf_ref.shape[0]):
        cols = slice(c * LANES, (c + 1) * LANES)
        hf_ref[c] = hf[:, cols]
        for out_ref, dilation in ((h4_ref, 4), (h16_ref, 16)):
            for r in range(dilation):
                rows = pl.ds(r, NORM_ROWS // dilation, stride=dilation)
                out_ref[0, r, :, cols] = hf_ref[c, rows, :].astype(out_ref.dtype)


def _rmsnorm(x, g, wt_lr):
    bs, seq, d = x.shape
    tm = NORM_ROWS

    def cls_spec(dilation):
        return pl.BlockSpec((1, dilation, tm // dilation, d), lambda b, i: (b, 0, i, 0))

    def cls_shape(dilation):
        return jax.ShapeDtypeStruct((bs, dilation, seq // dilation, d), BF16)

    return pl.pallas_call(
        _rmsnorm_kernel,
        grid=(bs, seq // tm),
        in_specs=[pl.BlockSpec((1, tm, d), lambda b, i: (b, i, 0)),
                  pl.BlockSpec((1, d), lambda b, i: (0, 0)),
                  pl.BlockSpec((LR_PAD, d), lambda b, i: (0, 0))],
        out_specs=[pl.BlockSpec((1, tm, d), lambda b, i: (b, i, 0)),
                   cls_spec(4), cls_spec(16),
                   pl.BlockSpec((1, tm, LR_PAD), lambda b, i: (b, i, 0))],
        out_shape=[jax.ShapeDtypeStruct((bs, seq, d), BF16), cls_shape(4), cls_shape(16),
                   jax.ShapeDtypeStruct((bs, seq, LR_PAD), BF16)],
        scratch_shapes=[pltpu.VMEM((d // LANES, tm, LANES), F32)],
        compiler_params=_params(("parallel", "parallel")),
        name="rmsnorm",
    )(x, g.reshape(1, d), wt_lr)


def _matmul_nt_kernel(a_ref, bt_ref, o_ref):
    o_ref[...] = lax.dot_general(a_ref[...], bt_ref[...], NT_DIMS,
                                 preferred_element_type=F32).astype(o_ref.dtype)


def _matmul_nt(a, bt, n_out, row_block, name, tm=1024, tn=512):
    m, k = a.shape
    return pl.pallas_call(
        _matmul_nt_kernel,
        grid=(m // tm, n_out // tn),
        in_specs=[pl.BlockSpec((tm, k), lambda i, j: (i, 0)),
                  pl.BlockSpec((tn, k), lambda i, j: (row_block(j), 0))],
        out_specs=pl.BlockSpec((tm, tn), lambda i, j: (i, j)),
        out_shape=jax.ShapeDtypeStruct((m, n_out), BF16),
        compiler_params=_params(("parallel", "arbitrary")),
        name=name,
    )(a, bt)


ATT_TILE = 2048
ATT_QB = 128
ATT_KW = ATT_QB + 2 * ATT_RADIUS
ATT_WINDOW_OFFSETS = (-ATT_RADIUS, 0, -2 * ATT_RADIUS)


def _attn_kernel(slopes_ref, q1_ref, k1_ref, v1_ref, q2_ref, k2_ref, v2_ref,
                 q3_ref, k3_ref, v3_ref, z_ref, y_ref, o_scr, l_scr, bias_scr, *, seq):
    h = pl.program_id(1)
    tile = pl.program_id(2)
    slope = slopes_ref[h]
    scale = ATT_HEAD_DIM ** -0.5

    col_minus_row = (lax.broadcasted_iota(jnp.int32, (ATT_QB, ATT_KW), 1)
                     - lax.broadcasted_iota(jnp.int32, (ATT_QB, ATT_KW), 0))
    for g, dilation in enumerate(ATT_DILATIONS):
        for v, off in enumerate(ATT_WINDOW_OFFSETS):
            dist = jnp.abs(col_minus_row + off)
            bias_scr[g * len(ATT_WINDOW_OFFSETS) + v] = jnp.where(
                dist <= ATT_RADIUS, -(slope * float(dilation)) * dist.astype(F32), NEG_INF)

    for g, (q_ref, k_ref, v_ref) in enumerate(((q1_ref, k1_ref, v1_ref),
                                               (q2_ref, k2_ref, v2_ref),
                                               (q3_ref, k3_ref, v3_ref))):
        dilation = ATT_DILATIONS[g]
        length = seq // dilation
        per_class = ATT_TILE // dilation // ATT_QB
        width = min(ATT_KW, length)
        for r in range(dilation):
            for j in range(per_class):
                row0 = j * ATT_QB
                p0 = tile * (ATT_TILE // dilation) + row0
                w0 = pl.multiple_of(jnp.clip(p0 - ATT_RADIUS, 0, length - width),
                                    ATT_RADIUS)
                off = w0 - p0
                variant = jnp.where(off == ATT_WINDOW_OFFSETS[1], 1,
                                    jnp.where(off == ATT_WINDOW_OFFSETS[2], 2, 0))
                bias = bias_scr[g * len(ATT_WINDOW_OFFSETS) + variant][:, :width]
                q = q_ref[0, r, row0:row0 + ATT_QB, :]
                s = lax.dot_general(q, k_ref[0, r, pl.ds(w0, width), :], NT_DIMS,
                                    preferred_element_type=F32) * scale + bias
                m = jnp.max(s, axis=-1, keepdims=True)
                e = jnp.exp(s - m)
                l = jnp.sum(e, axis=-1, keepdims=True)
                o = jnp.dot(e.astype(BF16), v_ref[0, r, pl.ds(w0, width), :],
                            preferred_element_type=F32) / l
                lse = jnp.broadcast_to(m + jnp.log(l), (ATT_QB, ATT_HEAD_DIM))
                if dilation == 1:
                    rows = pl.ds(row0, ATT_QB)
                else:
                    rows = pl.ds(row0 * dilation + r, ATT_QB, stride=dilation)
                o_scr[g, rows, :] = o
                l_scr[g, rows, :] = lse

    la, lb, lc = l_scr[0], l_scr[1], l_scr[2]
    m = jnp.maximum(jnp.maximum(la, lb), lc)
    ea, eb, ec = jnp.exp(la - m), jnp.exp(lb - m), jnp.exp(lc - m)
    mix = (ea * o_scr[0] + eb * o_scr[1] + ec * o_scr[2]) / (ea + eb + ec)
    z = z_ref[0].astype(F32)
    y_ref[0] = (mix * (z / (1.0 + jnp.exp(-z)))).astype(y_ref.dtype)


def _attention(slopes, proj3, cls4, cls16):
    bs, seq, _ = proj3.shape
    n_tiles = seq // ATT_TILE
    nat = proj3.reshape(bs, 1, seq, N_PROJ)

    def specs(dilation):
        tq = ATT_TILE // dilation
        q = pl.BlockSpec((1, dilation, tq, ATT_HEAD_DIM), lambda b, h, t: (b, 0, t, h))
        k = pl.BlockSpec((1, dilation, seq // dilation, ATT_HEAD_DIM),
                         lambda b, h, t: (b, 0, 0, ATT_HEADS + h))
        v = pl.BlockSpec((1, dilation, seq // dilation, ATT_HEAD_DIM),
                         lambda b, h, t: (b, 0, 0, 2 * ATT_HEADS + h))
        return [q, k, v]

    tok = pl.BlockSpec((1, ATT_TILE, ATT_HEAD_DIM),
                       lambda b, h, t: (b, t, OFF_AZ // ATT_HEAD_DIM + h))
    return pl.pallas_call(
        functools.partial(_attn_kernel, seq=seq),
        grid=(bs, ATT_HEADS, n_tiles),
        in_specs=([pl.BlockSpec(memory_space=pltpu.SMEM)]
                  + specs(1) + specs(4) + specs(16) + [tok]),
        out_specs=pl.BlockSpec((1, ATT_TILE, ATT_HEAD_DIM), lambda b, h, t: (b, t, h)),
        out_shape=jax.ShapeDtypeStruct((bs, seq, ATT_WIDTH), BF16),
        scratch_shapes=[pltpu.VMEM((N_GROUPS, ATT_TILE, ATT_HEAD_DIM), F32),
                        pltpu.VMEM((N_GROUPS, ATT_TILE, ATT_HEAD_DIM), F32),
                        pltpu.VMEM((N_GROUPS * len(ATT_WINDOW_OFFSETS), ATT_QB, ATT_KW),
                                   F32)],
        compiler_params=_params(("parallel", "parallel", "arbitrary")),
        name="attention",
    )(slopes, nat, nat, nat, cls4, cls4, cls4, cls16, cls16, cls16, proj3)


GLA_BLOCK_CHUNKS = 4
GLA_BLOCK = GLA_BLOCK_CHUNKS * GLA_CHUNK
GLA_TILE_BLOCKS = 2
GLA_TILE = GLA_TILE_BLOCKS * GLA_BLOCK
GLA_STEP_HEADS = 2


def _gla_kernel(*refs, reverse):
    if reverse:
        (q_ref, k_ref, v_ref, z_ref, w2_ref, b2_ref, of_ref, gz_ref, g_ref,
         y_ref, st_ref, qe_scr, kd_scr, ke_scr) = refs
    else:
        (q_ref, k_ref, v_ref, z_ref, w2_ref, b2_ref,
         y_ref, st_ref, qe_scr, kd_scr, ke_scr) = refs

    @pl.when(pl.program_id(2) == 0)
    def _():
        st_ref[...] = jnp.zeros_like(st_ref)

    nc, ch = GLA_BLOCK_CHUNKS, GLA_CHUNK
    heads = range(GLA_STEP_HEADS)
    ri = lax.broadcasted_iota(jnp.int32, (GLA_BLOCK, GLA_BLOCK), 0)
    ci = lax.broadcasted_iota(jnp.int32, (GLA_BLOCK, GLA_BLOCK), 1)
    same_chunk = (ri // ch) == (ci // ch)
    if reverse:
        cum = (same_chunk & (ci >= ri)).astype(BF16)
        keep = (ci // ch > ri // ch) | (same_chunk & (ci > ri))
        end_row = 0
        order = list(range(nc - 1, -1, -1))
    else:
        cum = (same_chunk & (ci <= ri)).astype(BF16)
        keep = (ci // ch < ri // ch) | (same_chunk & (ci <= ri))
        end_row = ch - 1
        order = list(range(nc))
    pos = {c: p for p, c in enumerate(order)}

    def chunk(c):
        return slice(c * ch, (c + 1) * ch)

    blocks = range(GLA_TILE_BLOCKS)
    for blk in (reversed(blocks) if reverse else blocks):
        r0 = blk * GLA_BLOCK
        rows = slice(r0, r0 + GLA_BLOCK)
        x = jnp.dot(z_ref[0, rows, :], w2_ref[...],
                    preferred_element_type=F32) + b2_ref[...]
        la = (jnp.minimum(x, 0.0) - jnp.log(1.0 + jnp.exp(-jnp.abs(x)))) * (1.0 / GLA_TAU)
        hi = la.astype(BF16)
        lo = (la - hi.astype(F32)).astype(BF16)
        b_all = (jnp.dot(cum, hi, preferred_element_type=F32)
                 + jnp.dot(cum, lo, preferred_element_type=F32))
        total = []
        for hd in heads:
            cols = slice(hd * GLA_DK, (hd + 1) * GLA_DK)
            b = b_all[:, cols]
            ends = [b[c * ch + end_row:c * ch + end_row + 1, :] for c in range(nc)]
            tot = [jnp.zeros_like(ends[0])]
            for p in range(nc):
                tot.append(tot[-1] + ends[order[p]])
            total.append(tot)
            e_end = jnp.concatenate(
                [jnp.broadcast_to(jnp.exp(e), (ch, GLA_DK)) for e in ends], axis=0)
            ke = k_ref[0, rows, cols].astype(F32) * jnp.exp(-b)
            qe_scr[hd, blk] = (q_ref[0, rows, cols].astype(F32) * (GLA_DK ** -0.5)
                               * jnp.exp(b))
            ke_scr[hd, blk] = ke.astype(BF16)
            kd_scr[hd, blk] = ke * e_end
        att = []
        for hd in heads:
            tot = total[hd]
            att_rows = [None] * nc
            for p in range(nc):
                a = order[p]
                parts = []
                for c in range(nc):
                    if c == a:
                        parts.append(ke_scr[hd, blk, chunk(c), :])
                    elif pos[c] < p:
                        kd_c = kd_scr[hd, blk, chunk(c), :]
                        if pos[c] < p - 1:
                            kd_c = kd_c * jnp.exp(tot[p] - tot[pos[c] + 1])
                        parts.append(kd_c.astype(BF16))
                    else:
                        parts.append(jnp.zeros((ch, GLA_DK), BF16))
                att_rows[a] = lax.dot_general(
                    qe_scr[hd, blk, chunk(a), :].astype(BF16),
                    jnp.concatenate(parts, axis=0), NT_DIMS, preferred_element_type=F32)
            att.append(jnp.where(keep, jnp.concatenate(att_rows, axis=0), 0.0).astype(BF16))
        for hd in heads:
            tot = total[hd]
            vcols = slice(hd * GLA_DV, (hd + 1) * GLA_DV)
            st = st_ref[hd]
            qe_in = jnp.concatenate(
                [(qe_scr[hd, blk, chunk(c), :] * jnp.exp(tot[pos[c]])).astype(BF16)
                 for c in range(nc)], axis=0)
            v = v_ref[0, rows, vcols]
            o = (jnp.dot(att[hd], v, preferred_element_type=F32)
                 + lax.dot_general(qe_in, st.astype(BF16), NT_DIMS,
                                   preferred_element_type=F32))
            if reverse:
                o = o + of_ref[0, rows, vcols]
                ms = jnp.mean(o * o, axis=-1, keepdims=True)
                o = o * lax.rsqrt(ms + NORM_EPS) * g_ref[...]
                gz = gz_ref[0, rows, vcols].astype(F32)
                y_ref[0, rows, vcols] = (o * (gz / (1.0 + jnp.exp(-gz)))).astype(y_ref.dtype)
            else:
                y_ref[0, rows, vcols] = o
            k_all = jnp.concatenate(
                [(kd_scr[hd, blk, chunk(c), :]
                  * jnp.exp(tot[nc] - tot[pos[c] + 1])).astype(BF16)
                 for c in range(nc)], axis=0)
            vt = v.astype(F32).T.astype(BF16)
            st_ref[hd] = (st * jnp.exp(tot[nc])
                          + jnp.dot(vt, k_all, preferred_element_type=F32))


def _gla_direction(proj3, lr3, w2pad, bias, reverse, o_fwd=None, norm_g=None):
    bs, seq, _ = proj3.shape
    nt = seq // GLA_TILE
    nh = GLA_STEP_HEADS

    def tile(t):
        return nt - 1 - t if reverse else t

    def spec(width, off):
        return pl.BlockSpec((1, GLA_TILE, nh * width),
                            lambda b, h, t: (b, tile(t), off // (nh * width) + h))

    in_specs = [spec(GLA_DK, OFF_GQ), spec(GLA_DK, OFF_GK), spec(GLA_DV, OFF_GV),
                pl.BlockSpec((1, GLA_TILE, LR_PAD), lambda b, h, t: (b, tile(t), 0)),
                pl.BlockSpec((LR_PAD, nh * GLA_DK), lambda b, h, t: (0, h)),
                pl.BlockSpec((1, nh * GLA_DK), lambda b, h, t: (0, h))]
    args = [proj3, proj3, proj3, lr3, w2pad, bias]
    out_spec = pl.BlockSpec((1, GLA_TILE, nh * GLA_DV), lambda b, h, t: (b, tile(t), h))
    if reverse:
        in_specs += [out_spec, spec(GLA_DV, OFF_GZ),
                     pl.BlockSpec((1, GLA_DV), lambda b, h, t: (0, 0))]
        args += [o_fwd, proj3, norm_g]
    block_scratch = (nh, GLA_TILE_BLOCKS, GLA_BLOCK, GLA_DK)
    return pl.pallas_call(
        functools.partial(_gla_kernel, reverse=reverse),
        grid=(bs, GLA_HEADS // nh, nt),
        in_specs=in_specs,
        out_specs=out_spec,
        out_shape=jax.ShapeDtypeStruct((bs, seq, GLA_VAL), BF16 if reverse else F32),
        scratch_shapes=[pltpu.VMEM((nh, GLA_DV, GLA_DK), F32),
                        pltpu.VMEM(block_scratch, F32),
                        pltpu.VMEM(block_scratch, F32),
                        pltpu.VMEM(block_scratch, BF16)],
        compiler_params=_params(("parallel", "parallel", "arbitrary")),
        name="gla_bwd" if reverse else "gla_fwd",
    )(*args)


def _merge_kernel(ya_ref, yb_ref, wa_ref, wb_ref, ga_ref, gb_ref, o_ref):
    ua = jnp.dot(ya_ref[...], wa_ref[...], preferred_element_type=F32)
    ub = jnp.dot(yb_ref[...], wb_ref[...], preferred_element_type=F32)
    ga = ga_ref[...].astype(F32)
    gb = gb_ref[...].astype(F32)
    merged = ua / (1.0 + jnp.exp(-ga)) + ub / (1.0 + jnp.exp(-gb))
    o_ref[...] = merged.astype(o_ref.dtype)


def _merge(ya, yb, wa, wb, gates, tm=1024, tn=512):
    m = ya.shape[0]
    return pl.pallas_call(
        _merge_kernel,
        grid=(m // tm, D_MODEL // tn),
        in_specs=[pl.BlockSpec((tm, ATT_WIDTH), lambda i, j: (i, 0)),
                  pl.BlockSpec((tm, GLA_VAL), lambda i, j: (i, 0)),
                  pl.BlockSpec((ATT_WIDTH, tn), lambda i, j: (0, j)),
                  pl.BlockSpec((GLA_VAL, tn), lambda i, j: (0, j)),
                  pl.BlockSpec((tm, tn), lambda i, j: (i, j)),
                  pl.BlockSpec((tm, tn), lambda i, j: (i, D_MODEL // tn + j))],
        out_specs=pl.BlockSpec((tm, tn), lambda i, j: (i, j)),
        out_shape=jax.ShapeDtypeStruct((m, D_MODEL), BF16),
        compiler_params=_params(("parallel", "arbitrary")),
        name="merge",
    )(ya, yb, wa, wb, gates, gates)


def _out_kernel(x_ref, a_ref, w_ref, g_ref, o_ref, acc_ref, *, nj):
    j = pl.program_id(1)
    acc_ref[j] = x_ref[...] + jnp.dot(a_ref[...], w_ref[...], preferred_element_type=F32)

    @pl.when(j == nj - 1)
    def _():
        tn = acc_ref.shape[2]
        ss = jnp.zeros((acc_ref.shape[1], 1), F32)
        for jj in range(nj):
            t = acc_ref[jj]
            ss = ss + jnp.sum(t * t, axis=-1, keepdims=True)
        rs = lax.rsqrt(ss * (1.0 / D_MODEL) + NORM_EPS)
        for jj in range(nj):
            cols = slice(jj * tn, (jj + 1) * tn)
            o_ref[:, cols] = (acc_ref[jj] * rs) * g_ref[:, cols]


def _out_proj(x, merged, w_o, g, tm=512, tn=512):
    m = x.shape[0]
    nj = D_MODEL // tn
    return pl.pallas_call(
        functools.partial(_out_kernel, nj=nj),
        grid=(m // tm, nj),
        in_specs=[pl.BlockSpec((tm, tn), lambda i, j: (i, j)),
                  pl.BlockSpec((tm, D_MODEL), lambda i, j: (i, 0)),
                  pl.BlockSpec((D_MODEL, tn), lambda i, j: (0, j)),
                  pl.BlockSpec((1, D_MODEL), lambda i, j: (0, 0))],
        out_specs=pl.BlockSpec((tm, D_MODEL), lambda i, j: (i, 0)),
        out_shape=jax.ShapeDtypeStruct((m, D_MODEL), F32),
        scratch_shapes=[pltpu.VMEM((nj, tm, tn), F32)],
        compiler_params=_params(("parallel", "arbitrary")),
        name="out_proj",
    )(x, merged, w_o, g.reshape(1, D_MODEL))


IN_PROJ_TN = 512


def _layer(x, w, final_g):
    bs, seq, d = x.shape
    m = bs * seq
    x2 = x.reshape(m, d)
    h, h4, h16, lr3 = _rmsnorm(x, w["norm_g"], w["wt_lr"])
    group_tiles = GROUP_COLS // IN_PROJ_TN
    proj = _matmul_nt(h.reshape(m, d), w["wt_a"], N_PROJ,
                      lambda j: jnp.where(j < group_tiles, j, j + 2 * group_tiles),
                      "in_proj")
    proj3 = proj.reshape(bs, seq, N_PROJ)
    cls = []
    for g, (dilation, hc) in enumerate(((4, h4), (16, h16)), start=1):
        p = _matmul_nt(hc.reshape(m, d), w["wt_a"], GROUP_COLS,
                       lambda j, g=g: j + g * group_tiles, f"in_proj_d{dilation}")
        cls.append(p.reshape(bs, dilation, seq // dilation, GROUP_COLS))
    gates = _matmul_nt(h.reshape(m, d), w["wt_b"], 2 * D_MODEL, lambda j: j, "in_proj_gates")
    ya = _attention(w["slopes"], proj3, *cls)

    o_f = _gla_direction(proj3, lr3, w["w2_f"], w["b_f"], False)
    yb = _gla_direction(proj3, lr3, w["w2_b"], w["b_b"], True, o_f, w["gla_norm_g"])

    merged = _merge(ya.reshape(m, ATT_WIDTH), yb.reshape(m, GLA_VAL),
                    w["w_up_a"], w["w_up_b"], gates)
    y = _out_proj(x2, merged, w["w_o"], final_g)
    return y.reshape(bs, seq, d)


def _pad_rows(w2, row0):
    return jnp.zeros((LR_PAD, GLA_KEY), BF16).at[row0:row0 + GLA_RANK].set(w2.astype(BF16))


def kernel(x_prompt, x_sample, norm_g, w_in, gla_w2_f, gla_b_f, gla_w2_b, gla_b_b,
           gla_norm_g, w_up_a, w_up_b, w_o, final_norm_g):
    assert norm_g.shape[0] == 1, "single-layer kernel"
    wt = w_in[0].T.astype(BF16)
    w = {
        "norm_g": norm_g[0],
        "wt_a": wt,
        "wt_b": wt[W_OFF_MG:],
        "wt_lr": jnp.pad(wt[W_OFF_LR:W_OFF_MG], ((0, LR_PAD - 2 * GLA_RANK), (0, 0))),
        "w2_f": _pad_rows(gla_w2_f[0], 0),
        "w2_b": _pad_rows(gla_w2_b[0], GLA_RANK),
        "b_f": gla_b_f[0].reshape(1, GLA_KEY),
        "b_b": gla_b_b[0].reshape(1, GLA_KEY),
        "gla_norm_g": gla_norm_g[0].reshape(1, GLA_DV),
        "w_up_a": w_up_a[0].astype(BF16),
        "w_up_b": w_up_b[0].astype(BF16),
        "w_o": w_o[0].astype(BF16),
        "slopes": jnp.exp2(-8.0 * (jnp.arange(ATT_HEADS, dtype=F32) + 1.0) / ATT_HEADS),
    }
    return (_layer(x_prompt, w, final_norm_g), _layer(x_sample, w, final_norm_g))
```

```python
import functools

import jax
import jax.numpy as jnp
import numpy as np
from jax import lax
from jax.experimental import pallas as pl
from jax.experimental.pallas import tpu as pltpu

F32 = jnp.float32
BF16 = jnp.bfloat16

D_MODEL = 4096
ATT_DILATIONS = (1, 4, 16)
N_GROUPS = 3
ATT_HEAD_DIM = 128
ATT_HEADS = 8
ATT_WIDTH = ATT_HEADS * ATT_HEAD_DIM
ATT_RADIUS = 64
GLA_HEADS = 4
GLA_KEY = 1024
GLA_VAL = 2048
GLA_DK = GLA_KEY // GLA_HEADS
GLA_DV = GLA_VAL // GLA_HEADS
GLA_RANK = 16
GLA_TAU = 16.0
GLA_CHUNK = 64
NORM_EPS = 1e-6
NEG_INF = -1e30

GROUP_COLS = 3 * ATT_WIDTH
W_OFF_AZ = N_GROUPS * GROUP_COLS
W_OFF_LR = W_OFF_AZ + ATT_WIDTH + 2 * GLA_KEY + 2 * GLA_VAL
W_OFF_MG = W_OFF_LR + 2 * GLA_RANK
OFF_GQ = 0
OFF_GK = OFF_GQ + GLA_KEY
OFF_GV = OFF_GK + GLA_KEY
OFF_GZ = OFF_GV + GLA_VAL
N_PROJ = OFF_GZ + GLA_VAL
LR_PAD = 128

VMEM_LIMIT = 56 * 1024 * 1024
BIG_VMEM_LIMIT = 62 * 1024 * 1024


def _params(sem):
    return pltpu.CompilerParams(dimension_semantics=sem, vmem_limit_bytes=VMEM_LIMIT)


LANES = 128
NORM_ROWS = 256
NT_DIMS = (((1,), (1,)), ((), ()))


def _class_major_permutation():
    perm = np.zeros((2 * NORM_ROWS, NORM_ROWS), np.float32)
    for block, dilation in enumerate((4, 16)):
        per_class = NORM_ROWS // dilation
        for r in range(dilation):
            for u in range(per_class):
                perm[block * NORM_ROWS + r * per_class + u, u * dilation + r] = 1.0
    return jnp.asarray(perm, BF16)


def _rmsnorm_kernel(x_ref, g_ref, wlr_ref, perm_ref, h_ref, h4_ref, h16_ref, lr_ref):
    x = x_ref[0]
    ms = jnp.mean(x * x, axis=-1, keepdims=True)
    h = ((x * lax.rsqrt(ms + NORM_EPS)) * g_ref[...]).astype(h_ref.dtype)
    h_ref[0] = h
    lr_ref[0] = lax.dot_general(h, wlr_ref[...], NT_DIMS,
                                preferred_element_type=F32).astype(lr_ref.dtype)
    hp = jnp.dot(perm_ref[...], h, preferred_element_type=F32).astype(h_ref.dtype)
    for block, (out_ref, dilation) in enumerate(((h4_ref, 4), (h16_ref, 16))):
        per_class = NORM_ROWS // dilation
        for r in range(dilation):
            row0 = block * NORM_ROWS + r * per_class
            out_ref[0, r] = hp[row0:row0 + per_class]


def _rmsnorm(x, g, wt_lr):
    bs, seq, d = x.shape
    tm = NORM_ROWS

    def cls_spec(dilation):
        return pl.BlockSpec((1, dilation, tm // dilation, d), lambda b, i: (b, 0, i, 0))

    def cls_shape(dilation):
        return jax.ShapeDtypeStruct((bs, dilation, seq // dilation, d), BF16)

    return pl.pallas_call(
        _rmsnorm_kernel,
        grid=(bs, seq // tm),
        in_specs=[pl.BlockSpec((1, tm, d), lambda b, i: (b, i, 0)),
                  pl.BlockSpec((1, d), lambda b, i: (0, 0)),
                  pl.BlockSpec((LR_PAD, d), lambda b, i: (0, 0)),
                  pl.BlockSpec((2 * tm, tm), lambda b, i: (0, 0))],
        out_specs=[pl.BlockSpec((1, tm, d), lambda b, i: (b, i, 0)),
                   cls_spec(4), cls_spec(16),
                   pl.BlockSpec((1, tm, LR_PAD), lambda b, i: (b, i, 0))],
        out_shape=[jax.ShapeDtypeStruct((bs, seq, d), BF16), cls_shape(4), cls_shape(16),
                   jax.ShapeDtypeStruct((bs, seq, LR_PAD), BF16)],
        compiler_params=_params(("parallel", "parallel")),
        name="rmsnorm",
    )(x, g.reshape(1, d), wt_lr, _class_major_permutation())


def _matmul_nt_kernel(a_ref, bt_ref, o_ref):
    o_ref[...] = lax.dot_general(a_ref[...], bt_ref[...].astype(BF16), NT_DIMS,
                                 preferred_element_type=F32).astype(o_ref.dtype)


def _matmul_nt_heads_kernel(a_ref, bt_ref, o_ref):
    res = lax.dot_general(a_ref[...], bt_ref[...].astype(BF16), NT_DIMS,
                          preferred_element_type=F32).astype(o_ref.dtype)
    for hh in range(o_ref.shape[0]):
        o_ref[hh] = res[:, hh * ATT_HEAD_DIM:(hh + 1) * ATT_HEAD_DIM]


def _matmul_nt(a, bt, n_out, row_block, name, tm=1024, tn=1024, row0=0, head_major=False):
    m, k = a.shape
    if head_major:
        hpt = tn // ATT_HEAD_DIM
        body = _matmul_nt_heads_kernel
        out_spec = pl.BlockSpec((hpt, tm, ATT_HEAD_DIM), lambda j, i: (j, i, 0))
        out_shape = jax.ShapeDtypeStruct((n_out // ATT_HEAD_DIM, m, ATT_HEAD_DIM), BF16)
    else:
        body = _matmul_nt_kernel
        out_spec = pl.BlockSpec((tm, tn), lambda j, i: (i, j))
        out_shape = jax.ShapeDtypeStruct((m, n_out), BF16)
    return pl.pallas_call(
        body,
        grid=(n_out // tn, m // tm),
        in_specs=[pl.BlockSpec((tm, k), lambda j, i: (i, 0)),
                  pl.BlockSpec((pl.Element(tn), pl.Element(k)),
                               lambda j, i: (pl.multiple_of(row_block(j) * tn + row0, 8), 0))],
        out_specs=out_spec,
        out_shape=out_shape,
        compiler_params=pltpu.CompilerParams(dimension_semantics=("parallel", "arbitrary"),
                                             vmem_limit_bytes=BIG_VMEM_LIMIT),
        name=name,
    )(a, bt)


ATT_TILE = 2048
ATT_QB = 128
ATT_KW = ATT_QB + 2 * ATT_RADIUS
ATT_WINDOW_OFFSETS = (-ATT_RADIUS, 0, -2 * ATT_RADIUS)


def _attn_kernel(slopes_ref, q1_ref, k1_ref, v1_ref, q2_ref, k2_ref, v2_ref,
                 q3_ref, k3_ref, v3_ref, z_ref, y_ref, o_scr, l_scr, bias_scr, *, seq):
    h = pl.program_id(1)
    tile = pl.program_id(2)
    slope = slopes_ref[h]
    scale = ATT_HEAD_DIM ** -0.5

    col_minus_row = (lax.broadcasted_iota(jnp.int32, (ATT_QB, ATT_KW), 1)
                     - lax.broadcasted_iota(jnp.int32, (ATT_QB, ATT_KW), 0))
    for g, dilation in enumerate(ATT_DILATIONS):
        for v, off in enumerate(ATT_WINDOW_OFFSETS):
            dist = jnp.abs(col_minus_row + off)
            bias_scr[g * len(ATT_WINDOW_OFFSETS) + v] = jnp.where(
                dist <= ATT_RADIUS, -(slope * float(dilation)) * dist.astype(F32), NEG_INF)

    for g, (q_ref, k_ref, v_ref) in enumerate(((q1_ref, k1_ref, v1_ref),
                                               (q2_ref, k2_ref, v2_ref),
                                               (q3_ref, k3_ref, v3_ref))):
        dilation = ATT_DILATIONS[g]
        length = seq // dilation
        per_class = ATT_TILE // dilation // ATT_QB
        width = min(ATT_KW, length)
        for r in range(dilation):
            for j in range(per_class):
                row0 = j * ATT_QB
                p0 = tile * (ATT_TILE // dilation) + row0
                w0 = pl.multiple_of(jnp.clip(p0 - ATT_RADIUS, 0, length - width),
                                    ATT_RADIUS)
                off = w0 - p0
                variant = jnp.where(off == ATT_WINDOW_OFFSETS[1], 1,
                                    jnp.where(off == ATT_WINDOW_OFFSETS[2], 2, 0))
                bias = bias_scr[g * len(ATT_WINDOW_OFFSETS) + variant][:, :width]
                q = q_ref[0, 0, r, row0:row0 + ATT_QB, :]
                s = lax.dot_general(q, k_ref[0, 0, r, pl.ds(w0, width), :], NT_DIMS,
                                    preferred_element_type=F32) * scale + bias
                m = jnp.max(s, axis=-1, keepdims=True)
                e = jnp.exp(s - m)
                l = jnp.sum(e, axis=-1, keepdims=True)
                o = jnp.dot(e.astype(BF16), v_ref[0, 0, r, pl.ds(w0, width), :],
                            preferred_element_type=F32) / l
                lse = jnp.broadcast_to(m + jnp.log(l), (ATT_QB, ATT_HEAD_DIM))
                if dilation == 1:
                    rows = pl.ds(row0, ATT_QB)
                else:
                    rows = pl.ds(row0 * dilation + r, ATT_QB, stride=dilation)
                o_scr[g, rows, :] = o
                l_scr[g, rows, :] = lse

    la, lb, lc = l_scr[0], l_scr[1], l_scr[2]
    m = jnp.maximum(jnp.maximum(la, lb), lc)
    ea, eb, ec = jnp.exp(la - m), jnp.exp(lb - m), jnp.exp(lc - m)
    mix = (ea * o_scr[0] + eb * o_scr[1] + ec * o_scr[2]) / (ea + eb + ec)
    z = z_ref[0, 0, 0].astype(F32)
    y_ref[0] = (mix * (z / (1.0 + jnp.exp(-z)))).astype(y_ref.dtype)


def _attention(slopes, qkv_groups):
    _, bs, _, seq, _ = qkv_groups[0].shape
    n_tiles = seq // ATT_TILE

    def specs(dilation):
        tq = ATT_TILE // dilation
        q = pl.BlockSpec((1, 1, dilation, tq, ATT_HEAD_DIM), lambda b, h, t: (h, b, 0, t, 0))
        k = pl.BlockSpec((1, 1, dilation, seq // dilation, ATT_HEAD_DIM),
                         lambda b, h, t: (ATT_HEADS + h, b, 0, 0, 0))
        v = pl.BlockSpec((1, 1, dilation, seq // dilation, ATT_HEAD_DIM),
                         lambda b, h, t: (2 * ATT_HEADS + h, b, 0, 0, 0))
        return [q, k, v]

    gate = pl.BlockSpec((1, 1, 1, ATT_TILE, ATT_HEAD_DIM),
                        lambda b, h, t: (3 * ATT_HEADS + h, b, 0, t, 0))
    operands = [slopes]
    in_specs = [pl.BlockSpec(memory_space=pltpu.SMEM)]
    for dilation, qkv in zip(ATT_DILATIONS, qkv_groups):
        operands += [qkv, qkv, qkv]
        in_specs += specs(dilation)
    return pl.pallas_call(
        functools.partial(_attn_kernel, seq=seq),
        grid=(bs, ATT_HEADS, n_tiles),
        in_specs=in_specs + [gate],
        out_specs=pl.BlockSpec((1, ATT_TILE, ATT_HEAD_DIM), lambda b, h, t: (b, t, h)),
        out_shape=jax.ShapeDtypeStruct((bs, seq, ATT_WIDTH), BF16),
        scratch_shapes=[pltpu.VMEM((N_GROUPS, ATT_TILE, ATT_HEAD_DIM), F32),
                        pltpu.VMEM((N_GROUPS, ATT_TILE, ATT_HEAD_DIM), F32),
                        pltpu.VMEM((N_GROUPS * len(ATT_WINDOW_OFFSETS), ATT_QB, ATT_KW),
                                   F32)],
        compiler_params=_params(("parallel", "parallel", "arbitrary")),
        name="attention",
    )(*operands, qkv_groups[0])


GLA_BLOCK_CHUNKS = 4
GLA_BLOCK = GLA_BLOCK_CHUNKS * GLA_CHUNK
GLA_TILE_BLOCKS = 2
GLA_TILE = GLA_TILE_BLOCKS * GLA_BLOCK
GLA_STEP_HEADS = 4


def _gla_kernel(*refs, reverse):
    if reverse:
        (q_ref, k_ref, v_ref, z_ref, w2_ref, b2_ref, of_ref, gz_ref, g_ref,
         y_ref, st_ref, qe_scr, kd_scr, ke_scr) = refs
    else:
        (q_ref, k_ref, v_ref, z_ref, w2_ref, b2_ref,
         y_ref, st_ref, qe_scr, kd_scr, ke_scr) = refs

    @pl.when(pl.program_id(2) == 0)
    def _():
        st_ref[...] = jnp.zeros_like(st_ref)

    nc, ch = GLA_BLOCK_CHUNKS, GLA_CHUNK
    heads = range(GLA_STEP_HEADS)
    ri = lax.broadcasted_iota(jnp.int32, (GLA_BLOCK, GLA_BLOCK), 0)
    ci = lax.broadcasted_iota(jnp.int32, (GLA_BLOCK, GLA_BLOCK), 1)
    same_chunk = (ri // ch) == (ci // ch)
    if reverse:
        cum = (same_chunk & (ci >= ri)).astype(BF16)
        keep = (ci // ch > ri // ch) | (same_chunk & (ci > ri))
        end_row = 0
        order = list(range(nc - 1, -1, -1))
    else:
        cum = (same_chunk & (ci <= ri)).astype(BF16)
        keep = (ci // ch < ri // ch) | (same_chunk & (ci <= ri))
        end_row = ch - 1
        order = list(range(nc))
    pos = {c: p for p, c in enumerate(order)}

    def chunk(c):
        return slice(c * ch, (c + 1) * ch)

    blocks = range(GLA_TILE_BLOCKS)
    for blk in (reversed(blocks) if reverse else blocks):
        r0 = blk * GLA_BLOCK
        rows = slice(r0, r0 + GLA_BLOCK)
        x = jnp.dot(z_ref[0, rows, :], w2_ref[...],
                    preferred_element_type=F32) + b2_ref[...]
        la = (jnp.minimum(x, 0.0) - jnp.log(1.0 + jnp.exp(-jnp.abs(x)))) * (1.0 / GLA_TAU)
        hi = la.astype(BF16)
        lo = (la - hi.astype(F32)).astype(BF16)
        b_all = (jnp.dot(cum, hi, preferred_element_type=F32)
                 + jnp.dot(cum, lo, preferred_element_type=F32))
        total = []
        for hd in heads:
            cols = slice(hd * GLA_DK, (hd + 1) * GLA_DK)
            b = b_all[:, cols]
            ends = [b[c * ch + end_row:c * ch + end_row + 1, :] for c in range(nc)]
            tot = [jnp.zeros_like(ends[0])]
            for p in range(nc):
                tot.append(tot[-1] + ends[order[p]])
            total.append(tot)
            e_end = jnp.concatenate(
                [jnp.broadcast_to(jnp.exp(e), (ch, GLA_DK)) for e in ends], axis=0)
            ke = k_ref[0, rows, cols].astype(F32) * jnp.exp(-b)
            qe_scr[hd, blk] = (q_ref[0, rows, cols].astype(F32) * (GLA_DK ** -0.5)
                               * jnp.exp(b))
            ke_scr[hd, blk] = ke.astype(BF16)
            kd_scr[hd, blk] = ke * e_end
        att = []
        for hd in heads:
            tot = total[hd]
            att_rows = [None] * nc
            for p in range(nc):
                a = order[p]
                parts = []
                for c in range(nc):
                    if c == a:
                        parts.append(ke_scr[hd, blk, chunk(c), :])
                    elif pos[c] < p:
                        kd_c = kd_scr[hd, blk, chunk(c), :]
                        if pos[c] < p - 1:
                            kd_c = kd_c * jnp.exp(tot[p] - tot[pos[c] + 1])
                        parts.append(kd_c.astype(BF16))
                    else:
                        parts.append(jnp.zeros((ch, GLA_DK), BF16))
                att_rows[a] = lax.dot_general(
                    qe_scr[hd, blk, chunk(a), :].astype(BF16),
                    jnp.concatenate(parts, axis=0), NT_DIMS, preferred_element_type=F32)
            att.append(jnp.where(keep, jnp.concatenate(att_rows, axis=0), 0.0).astype(BF16))
        for hd in heads:
            tot = total[hd]
            vcols = slice(hd * GLA_DV, (hd + 1) * GLA_DV)
            st = st_ref[hd]
            qe_in = jnp.concatenate(
                [(qe_scr[hd, blk, chunk(c), :] * jnp.exp(tot[pos[c]])).astype(BF16)
                 for c in range(nc)], axis=0)
            v = v_ref[0, rows, vcols]
            o = (jnp.dot(att[hd], v, preferred_element_type=F32)
                 + lax.dot_general(qe_in, st.astype(BF16), NT_DIMS,
                                   preferred_element_type=F32))
            if reverse:
                o = o + of_ref[0, rows, vcols]
                ms = jnp.mean(o * o, axis=-1, keepdims=True)
                o = o * lax.rsqrt(ms + NORM_EPS) * g_ref[...]
                gz = gz_ref[0, rows, vcols].astype(F32)
                y_ref[0, rows, vcols] = (o * (gz / (1.0 + jnp.exp(-gz)))).astype(y_ref.dtype)
            else:
                y_ref[0, rows, vcols] = o
            k_all = jnp.concatenate(
                [(kd_scr[hd, blk, chunk(c), :]
                  * jnp.exp(tot[nc] - tot[pos[c] + 1])).astype(BF16)
                 for c in range(nc)], axis=0)
            vt = v.astype(F32).T.astype(BF16)
            st_ref[hd] = (st * jnp.exp(tot[nc])
                          + jnp.dot(vt, k_all, preferred_element_type=F32))


def _gla_direction(proj3, lr3, w2pad, bias, reverse, o_fwd=None, norm_g=None):
    bs, seq, _ = proj3.shape
    nt = seq // GLA_TILE
    nh = GLA_STEP_HEADS

    def tile(t):
        return nt - 1 - t if reverse else t

    def spec(width, off):
        return pl.BlockSpec((1, GLA_TILE, nh * width),
                            lambda b, h, t: (b, tile(t), off // (nh * width) + h))

    in_specs = [spec(GLA_DK, OFF_GQ), spec(GLA_DK, OFF_GK), spec(GLA_DV, OFF_GV),
                pl.BlockSpec((1, GLA_TILE, LR_PAD), lambda b, h, t: (b, tile(t), 0)),
                pl.BlockSpec((LR_PAD, nh * GLA_DK), lambda b, h, t: (0, h)),
                pl.BlockSpec((1, nh * GLA_DK), lambda b, h, t: (0, h))]
    args = [proj3, proj3, proj3, lr3, w2pad, bias]
    out_spec = pl.BlockSpec((1, GLA_TILE, nh * GLA_DV), lambda b, h, t: (b, tile(t), h))
    if reverse:
        in_specs += [out_spec, spec(GLA_DV, OFF_GZ),
                     pl.BlockSpec((1, GLA_DV), lambda b, h, t: (0, 0))]
        args += [o_fwd, proj3, norm_g]
    block_scratch = (nh, GLA_TILE_BLOCKS, GLA_BLOCK, GLA_DK)
    return pl.pallas_call(
        functools.partial(_gla_kernel, reverse=reverse),
        grid=(bs, GLA_HEADS // nh, nt),
        in_specs=in_specs,
        out_specs=out_spec,
        out_shape=jax.ShapeDtypeStruct((bs, seq, GLA_VAL), BF16 if reverse else F32),
        scratch_shapes=[pltpu.VMEM((nh, GLA_DV, GLA_DK), F32),
                        pltpu.VMEM(block_scratch, F32),
                        pltpu.VMEM(block_scratch, F32),
                        pltpu.VMEM(block_scratch, BF16)],
        compiler_params=_params(("parallel", "parallel", "arbitrary")),
        name="gla_bwd" if reverse else "gla_fwd",
    )(*args)


def _merge_kernel(ya_ref, yb_ref, wa_ref, wb_ref, ga_ref, gb_ref, o_ref):
    ua = jnp.dot(ya_ref[...], wa_ref[...].astype(BF16), preferred_element_type=F32)
    ub = jnp.dot(yb_ref[...], wb_ref[...].astype(BF16), preferred_element_type=F32)
    ga = ga_ref[...].astype(F32)
    gb = gb_ref[...].astype(F32)
    merged = ua / (1.0 + jnp.exp(-ga)) + ub / (1.0 + jnp.exp(-gb))
    o_ref[...] = merged.astype(o_ref.dtype)


def _merge(ya, yb, wa, wb, gates, tm=1024, tn=512):
    m = ya.shape[0]
    return pl.pallas_call(
        _merge_kernel,
        grid=(m // tm, D_MODEL // tn),
        in_specs=[pl.BlockSpec((tm, ATT_WIDTH), lambda i, j: (i, 0)),
                  pl.BlockSpec((tm, GLA_VAL), lambda i, j: (i, 0)),
                  pl.BlockSpec((ATT_WIDTH, tn), lambda i, j: (0, j)),
                  pl.BlockSpec((GLA_VAL, tn), lambda i, j: (0, j)),
                  pl.BlockSpec((tm, tn), lambda i, j: (i, j)),
                  pl.BlockSpec((tm, tn), lambda i, j: (i, D_MODEL // tn + j))],
        out_specs=pl.BlockSpec((tm, tn), lambda i, j: (i, j)),
        out_shape=jax.ShapeDtypeStruct((m, D_MODEL), BF16),
        compiler_params=_params(("parallel", "arbitrary")),
        name="merge",
    )(ya, yb, wa, wb, gates, gates)


def _out_kernel(x_ref, a_ref, w_ref, g_ref, o_ref, u_ref, ss_ref, rs_ref, *, n_rows):
    i = pl.program_id(0)
    j = pl.program_id(1)
    slot = i % 2
    prev = 1 - slot
    lane_tiles = o_ref.shape[1] // LANES

    def matmul_tile(first_tile):
        u = x_ref[...] + jnp.dot(a_ref[...], w_ref[...], preferred_element_type=F32)
        u_ref[slot, j] = u
        sq = u * u
        part = sq[:, :LANES]
        for c in range(1, lane_tiles):
            part = part + sq[:, c * LANES:(c + 1) * LANES]
        ss_ref[slot] = part if first_tile else ss_ref[slot] + part

    def scale_previous():
        ss = jnp.sum(ss_ref[prev], axis=-1, keepdims=True)
        rs = lax.rsqrt(ss * (1.0 / D_MODEL) + NORM_EPS)
        rs_ref[...] = jnp.broadcast_to(rs, rs_ref.shape)

    def normalize_previous():
        for c in range(lane_tiles):
            cols = slice(c * LANES, (c + 1) * LANES)
            o_ref[:, cols] = (u_ref[prev, j, :, cols] * rs_ref[...]) * g_ref[:, cols]

    first, last = i == 0, i == n_rows
    middle = jnp.logical_not(first | last)

    @pl.when(first & (j == 0))
    def _():
        matmul_tile(True)

    @pl.when(first & (j > 0))
    def _():
        matmul_tile(False)

    @pl.when(middle & (j == 0))
    def _():
        scale_previous()
        matmul_tile(True)
        normalize_previous()

    @pl.when(middle & (j > 0))
    def _():
        matmul_tile(False)
        normalize_previous()

    @pl.when(last & (j == 0))
    def _():
        scale_previous()
        normalize_previous()

    @pl.when(last & (j > 0))
    def _():
        normalize_previous()


def _out_proj(x, merged, w_o, g, tm=1024, tn=256):
    m = x.shape[0]
    n_rows = m // tm
    nj = D_MODEL // tn

    def row(i):
        return jnp.minimum(i, n_rows - 1)

    return pl.pallas_call(
        functools.partial(_out_kernel, n_rows=n_rows),
        grid=(n_rows + 1, nj),
        in_specs=[pl.BlockSpec((tm, tn), lambda i, j: (row(i), j)),
                  pl.BlockSpec((tm, D_MODEL), lambda i, j: (row(i), 0)),
                  pl.BlockSpec((D_MODEL, tn), lambda i, j: (0, j)),
                  pl.BlockSpec((1, tn), lambda i, j: (0, j))],
        out_specs=pl.BlockSpec((tm, tn), lambda i, j: (jnp.maximum(i - 1, 0),
                                                       jnp.where(i == 0, 0, j))),
        out_shape=jax.ShapeDtypeStruct((m, D_MODEL), F32),
        scratch_shapes=[pltpu.VMEM((2, nj, tm, tn), F32),
                        pltpu.VMEM((2, tm, LANES), F32),
                        pltpu.VMEM((tm, LANES), F32)],
        compiler_params=pltpu.CompilerParams(dimension_semantics=("arbitrary", "arbitrary"),
                                             vmem_limit_bytes=BIG_VMEM_LIMIT),
        name="out_proj",
    )(x, merged, w_o, g.reshape(1, D_MODEL))


IN_PROJ_TN = 1024


def _layer(x, w, final_g):
    bs, seq, d = x.shape
    m = bs * seq
    x2 = x.reshape(m, d)
    h, h4, h16, lr3 = _rmsnorm(x, w["norm_g"], w["wt_lr"])
    group_tiles = GROUP_COLS // IN_PROJ_TN
    gate_tile = W_OFF_AZ // IN_PROJ_TN
    proj = _matmul_nt(h.reshape(m, d), w["wt_a"], N_PROJ,
                      lambda j: j + gate_tile + ATT_WIDTH // IN_PROJ_TN, "in_proj")
    proj3 = proj.reshape(bs, seq, N_PROJ)
    qkv_groups = []
    for g, (dilation, hc) in enumerate(zip(ATT_DILATIONS, (h, h4, h16))):
        if g == 0:
            p = _matmul_nt(hc.reshape(m, d), w["wt_a"], GROUP_COLS + ATT_WIDTH,
                           lambda j: jnp.where(j < group_tiles, j, gate_tile),
                           "in_proj_d1", head_major=True)
        else:
            p = _matmul_nt(hc.reshape(m, d), w["wt_a"], GROUP_COLS,
                           lambda j, g=g: j + g * group_tiles, f"in_proj_d{dilation}",
                           head_major=True)
        qkv_groups.append(p.reshape(-1, bs, dilation, seq // dilation, ATT_HEAD_DIM))
    gates = _matmul_nt(h.reshape(m, d), w["wt_a"], 2 * D_MODEL, lambda j: j, "in_proj_gates",
                       row0=W_OFF_MG)
    ya = _attention(w["slopes"], qkv_groups)

    o_f = _gla_direction(proj3, lr3, w["w2_f"], w["b_f"], False)
    yb = _gla_direction(proj3, lr3, w["w2_b"], w["b_b"], True, o_f, w["gla_norm_g"])

    merged = _merge(ya.reshape(m, ATT_WIDTH), yb.reshape(m, GLA_VAL),
                    w["w_up_a"], w["w_up_b"], gates)
    y = _out_proj(x2, merged, w["w_o"], final_g)
    return y.reshape(bs, seq, d)


def _pad_rows(w2, row0):
    return jnp.zeros((LR_PAD, GLA_KEY), BF16).at[row0:row0 + GLA_RANK].set(w2.astype(BF16))


def kernel(x_prompt, x_sample, norm_g, w_in, gla_w2_f, gla_b_f, gla_w2_b, gla_b_b,
           gla_norm_g, w_up_a, w_up_b, w_o, final_norm_g):
    assert norm_g.shape[0] == 1, "single-layer kernel"
    wt = w_in[0].T
    w = {
        "norm_g": norm_g[0],
        "wt_a": wt,
        "wt_lr": jnp.pad(wt[W_OFF_LR:W_OFF_MG],
                         ((0, LR_PAD - 2 * GLA_RANK), (0, 0))).astype(BF16),
        "w2_f": _pad_rows(gla_w2_f[0], 0),
        "w2_b": _pad_rows(gla_w2_b[0], GLA_RANK),
        "b_f": gla_b_f[0].reshape(1, GLA_KEY),
        "b_b": gla_b_b[0].reshape(1, GLA_KEY),
        "gla_norm_g": gla_norm_g[0].reshape(1, GLA_DV),
        "w_up_a": w_up_a[0],
        "w_up_b": w_up_b[0],
        "w_o": w_o[0].astype(BF16),
        "slopes": jnp.exp2(-8.0 * (jnp.arange(ATT_HEADS, dtype=F32) + 1.0) / ATT_HEADS),
    }
    return (_layer(x_prompt, w, final_norm_g), _layer(x_sample, w, final_norm_g))
```

```python
import functools

import jax
import jax.numpy as jnp
import numpy as np
from jax import lax
from jax.experimental import pallas as pl
from jax.experimental.pallas import tpu as pltpu

F32 = jnp.float32
BF16 = jnp.bfloat16

D_MODEL = 4096
ATT_DILATIONS = (1, 4, 16)
N_GROUPS = 3
ATT_HEAD_DIM = 128
ATT_HEADS = 8
ATT_WIDTH = ATT_HEADS * ATT_HEAD_DIM
ATT_RADIUS = 64
GLA_HEADS = 4
GLA_KEY = 1024
GLA_VAL = 2048
GLA_DK = GLA_KEY // GLA_HEADS
GLA_DV = GLA_VAL // GLA_HEADS
GLA_RANK = 16
GLA_TAU = 16.0
GLA_CHUNK = 64
NORM_EPS = 1e-6
NEG_INF = -1e30

GROUP_COLS = 3 * ATT_WIDTH
W_OFF_AZ = N_GROUPS * GROUP_COLS
W_OFF_LR = W_OFF_AZ + ATT_WIDTH + 2 * GLA_KEY + 2 * GLA_VAL
W_OFF_MG = W_OFF_LR + 2 * GLA_RANK
OFF_AZ = GROUP_COLS
OFF_GQ = OFF_AZ + ATT_WIDTH
OFF_GK = OFF_GQ + GLA_KEY
OFF_GV = OFF_GK + GLA_KEY
OFF_GZ = OFF_GV + GLA_VAL
N_PROJ = OFF_GZ + GLA_VAL
LR_PAD = 128

VMEM_LIMIT = 56 * 1024 * 1024
BIG_VMEM_LIMIT = 62 * 1024 * 1024


def _params(sem):
    return pltpu.CompilerParams(dimension_semantics=sem, vmem_limit_bytes=VMEM_LIMIT)


NORM_ROWS = 256
NT_DIMS = (((1,), (1,)), ((), ()))


def _class_major_permutation():
    perm = np.zeros((2 * NORM_ROWS, NORM_ROWS), np.float32)
    for block, dilation in enumerate((4, 16)):
        per_class = NORM_ROWS // dilation
        for r in range(dilation):
            for u in range(per_class):
                perm[block * NORM_ROWS + r * per_class + u, u * dilation + r] = 1.0
    return jnp.asarray(perm, BF16)


def _rmsnorm_kernel(x_ref, g_ref, wlr_ref, perm_ref, h_ref, h4_ref, h16_ref, lr_ref):
    x = x_ref[0]
    ms = jnp.mean(x * x, axis=-1, keepdims=True)
    h = ((x * lax.rsqrt(ms + NORM_EPS)) * g_ref[...]).astype(h_ref.dtype)
    h_ref[0] = h
    lr_ref[0] = lax.dot_general(h, wlr_ref[...], NT_DIMS,
                                preferred_element_type=F32).astype(lr_ref.dtype)
    hp = jnp.dot(perm_ref[...], h, preferred_element_type=F32).astype(h_ref.dtype)
    for block, (out_ref, dilation) in enumerate(((h4_ref, 4), (h16_ref, 16))):
        per_class = NORM_ROWS // dilation
        for r in range(dilation):
            row0 = block * NORM_ROWS + r * per_class
            out_ref[0, r] = hp[row0:row0 + per_class]


def _rmsnorm(x, g, wt_lr):
    bs, seq, d = x.shape
    tm = NORM_ROWS

    def cls_spec(dilation):
        return pl.BlockSpec((1, dilation, tm // dilation, d), lambda b, i: (b, 0, i, 0))

    def cls_shape(dilation):
        return jax.ShapeDtypeStruct((bs, dilation, seq // dilation, d), BF16)

    return pl.pallas_call(
        _rmsnorm_kernel,
        grid=(bs, seq // tm),
        in_specs=[pl.BlockSpec((1, tm, d), lambda b, i: (b, i, 0)),
                  pl.BlockSpec((1, d), lambda b, i: (0, 0)),
                  pl.BlockSpec((LR_PAD, d), lambda b, i: (0, 0)),
                  pl.BlockSpec((2 * tm, tm), lambda b, i: (0, 0))],
        out_specs=[pl.BlockSpec((1, tm, d), lambda b, i: (b, i, 0)),
                   cls_spec(4), cls_spec(16),
                   pl.BlockSpec((1, tm, LR_PAD), lambda b, i: (b, i, 0))],
        out_shape=[jax.ShapeDtypeStruct((bs, seq, d), BF16), cls_shape(4), cls_shape(16),
                   jax.ShapeDtypeStruct((bs, seq, LR_PAD), BF16)],
        compiler_params=_params(("parallel", "parallel")),
        name="rmsnorm",
    )(x, g.reshape(1, d), wt_lr, _class_major_permutation())


def _matmul_nt_kernel(a_ref, bt_ref, o_ref):
    o_ref[...] = lax.dot_general(a_ref[...], bt_ref[...].astype(BF16), NT_DIMS,
                                 preferred_element_type=F32).astype(o_ref.dtype)


def _matmul_nt(a, bt, n_out, row_block, name, tm=1024, tn=1024, row0=0):
    m, k = a.shape
    return pl.pallas_call(
        _matmul_nt_kernel,
        grid=(n_out // tn, m // tm),
        in_specs=[pl.BlockSpec((tm, k), lambda j, i: (i, 0)),
                  pl.BlockSpec((pl.Element(tn), pl.Element(k)),
                               lambda j, i: (pl.multiple_of(row_block(j) * tn + row0, 8), 0))],
        out_specs=pl.BlockSpec((tm, tn), lambda j, i: (i, j)),
        out_shape=jax.ShapeDtypeStruct((m, n_out), BF16),
        compiler_params=pltpu.CompilerParams(dimension_semantics=("parallel", "arbitrary"),
                                             vmem_limit_bytes=BIG_VMEM_LIMIT),
        name=name,
    )(a, bt)


ATT_TILE = 2048
ATT_QB = 128
ATT_KW = ATT_QB + 2 * ATT_RADIUS
ATT_WINDOW_OFFSETS = (-ATT_RADIUS, 0, -2 * ATT_RADIUS)


def _attn_kernel(slopes_ref, q1_ref, k1_ref, v1_ref, q2_ref, k2_ref, v2_ref,
                 q3_ref, k3_ref, v3_ref, z_ref, y_ref, o_scr, l_scr, bias_scr, *, seq):
    h = pl.program_id(1)
    tile = pl.program_id(2)
    slope = slopes_ref[h]
    scale = ATT_HEAD_DIM ** -0.5

    col_minus_row = (lax.broadcasted_iota(jnp.int32, (ATT_QB, ATT_KW), 1)
                     - lax.broadcasted_iota(jnp.int32, (ATT_QB, ATT_KW), 0))
    col = lax.broadcasted_iota(jnp.int32, (ATT_QB, ATT_KW), 1)
    for g, dilation in enumerate(ATT_DILATIONS):
        if seq // dilation >= ATT_KW:
            valid = [jnp.abs(col_minus_row + off) for off in ATT_WINDOW_OFFSETS]
        else:
            valid = [jnp.where(col // ATT_QB == c, jnp.abs(col_minus_row - c * ATT_QB),
                               ATT_KW) for c in range(2)]
        for v, dist in enumerate(valid):
            bias_scr[g * len(ATT_WINDOW_OFFSETS) + v] = jnp.where(
                dist <= ATT_RADIUS, -(slope * float(dilation)) * dist.astype(F32), NEG_INF)

    for g, (q_ref, k_ref, v_ref) in enumerate(((q1_ref, k1_ref, v1_ref),
                                               (q2_ref, k2_ref, v2_ref),
                                               (q3_ref, k3_ref, v3_ref))):
        dilation = ATT_DILATIONS[g]
        length = seq // dilation
        per_class = ATT_TILE // dilation // ATT_QB
        paired = length < ATT_KW
        for r in range(dilation):
            for j in range(per_class):
                row0 = j * ATT_QB
                if paired:
                    kv_class, variant, w0 = r // 2, r % 2, 0
                else:
                    p0 = tile * (ATT_TILE // dilation) + row0
                    w0 = pl.multiple_of(jnp.clip(p0 - ATT_RADIUS, 0, length - ATT_KW),
                                        ATT_RADIUS)
                    off = w0 - p0
                    variant = jnp.where(off == ATT_WINDOW_OFFSETS[1], 1,
                                        jnp.where(off == ATT_WINDOW_OFFSETS[2], 2, 0))
                    kv_class = r
                bias = bias_scr[g * len(ATT_WINDOW_OFFSETS) + variant]
                q = q_ref[0, r, row0:row0 + ATT_QB, :]
                s = lax.dot_general(q, k_ref[0, kv_class, pl.ds(w0, ATT_KW), :], NT_DIMS,
                                    preferred_element_type=F32) * scale + bias
                m = jnp.max(s, axis=-1, keepdims=True)
                e = jnp.exp(s - m)
                l = jnp.sum(e, axis=-1, keepdims=True)
                o = jnp.dot(e.astype(BF16), v_ref[0, kv_class, pl.ds(w0, ATT_KW), :],
                            preferred_element_type=F32) / l
                lse = jnp.broadcast_to(m + jnp.log(l), (ATT_QB, ATT_HEAD_DIM))
                if dilation == 1:
                    rows = pl.ds(row0, ATT_QB)
                else:
                    rows = pl.ds(row0 * dilation + r, ATT_QB, stride=dilation)
                o_scr[g, rows, :] = o
                l_scr[g, rows, :] = lse

    la, lb, lc = l_scr[0], l_scr[1], l_scr[2]
    m = jnp.maximum(jnp.maximum(la, lb), lc)
    ea, eb, ec = jnp.exp(la - m), jnp.exp(lb - m), jnp.exp(lc - m)
    mix = (ea * o_scr[0] + eb * o_scr[1] + ec * o_scr[2]) / (ea + eb + ec)
    z = z_ref[0].astype(F32)
    y_ref[0] = (mix * (z / (1.0 + jnp.exp(-z)))).astype(y_ref.dtype)


def _attention(slopes, proj3, cls4, cls16):
    bs, seq, _ = proj3.shape
    n_tiles = seq // ATT_TILE
    nat = proj3.reshape(bs, 1, seq, N_PROJ)

    def kv_shape(dilation):
        length = seq // dilation
        return (dilation, length) if length >= ATT_KW else (dilation // 2, 2 * length)

    def specs(dilation):
        tq = ATT_TILE // dilation
        q = pl.BlockSpec((1, dilation, tq, ATT_HEAD_DIM), lambda b, h, t: (b, 0, t, h))
        k = pl.BlockSpec((1,) + kv_shape(dilation) + (ATT_HEAD_DIM,),
                         lambda b, h, t: (b, 0, 0, ATT_HEADS + h))
        v = pl.BlockSpec((1,) + kv_shape(dilation) + (ATT_HEAD_DIM,),
                         lambda b, h, t: (b, 0, 0, 2 * ATT_HEADS + h))
        return [q, k, v]

    def kv_view(t, dilation):
        return t.reshape((bs,) + kv_shape(dilation) + (t.shape[-1],))

    tok = pl.BlockSpec((1, ATT_TILE, ATT_HEAD_DIM),
                       lambda b, h, t: (b, t, OFF_AZ // ATT_HEAD_DIM + h))
    return pl.pallas_call(
        functools.partial(_attn_kernel, seq=seq),
        grid=(bs, ATT_HEADS, n_tiles),
        in_specs=([pl.BlockSpec(memory_space=pltpu.SMEM)]
                  + specs(1) + specs(4) + specs(16) + [tok]),
        out_specs=pl.BlockSpec((1, ATT_TILE, ATT_HEAD_DIM), lambda b, h, t: (b, t, h)),
        out_shape=jax.ShapeDtypeStruct((bs, seq, ATT_WIDTH), BF16),
        scratch_shapes=[pltpu.VMEM((N_GROUPS, ATT_TILE, ATT_HEAD_DIM), F32),
                        pltpu.VMEM((N_GROUPS, ATT_TILE, ATT_HEAD_DIM), F32),
                        pltpu.VMEM((N_GROUPS * len(ATT_WINDOW_OFFSETS), ATT_QB, ATT_KW),
                                   F32)],
        compiler_params=_params(("parallel", "parallel", "arbitrary")),
        name="attention",
    )(slopes, nat, nat, nat, cls4, kv_view(cls4, 4), kv_view(cls4, 4),
      cls16, kv_view(cls16, 16), kv_view(cls16, 16), proj3)


GLA_BLOCK_CHUNKS = 4
GLA_BLOCK = GLA_BLOCK_CHUNKS * GLA_CHUNK
GLA_TILE_BLOCKS = 2
GLA_TILE = GLA_TILE_BLOCKS * GLA_BLOCK
GLA_STEP_HEADS = 4


def _gla_kernel(*refs, reverse):
    if reverse:
        (q_ref, k_ref, v_ref, z_ref, w2_ref, b2_ref, of_ref, gz_ref, g_ref,
         y_ref, st_ref, qe_scr, kd_scr, ke_scr) = refs
    else:
        (q_ref, k_ref, v_ref, z_ref, w2_ref, b2_ref,
         y_ref, st_ref, qe_scr, kd_scr, ke_scr) = refs

    @pl.when(pl.program_id(2) == 0)
    def _():
        st_ref[...] = jnp.zeros_like(st_ref)

    nc, ch = GLA_BLOCK_CHUNKS, GLA_CHUNK
    heads = range(GLA_STEP_HEADS)
    ri = lax.broadcasted_iota(jnp.int32, (GLA_BLOCK, GLA_BLOCK), 0)
    ci = lax.broadcasted_iota(jnp.int32, (GLA_BLOCK, GLA_BLOCK), 1)
    same_chunk = (ri // ch) == (ci // ch)
    if reverse:
        cum = (same_chunk & (ci >= ri)).astype(BF16)
        keep = (ci // ch > ri // ch) | (same_chunk & (ci > ri))
        end_row = 0
        order = list(range(nc - 1, -1, -1))
    else:
        cum = (same_chunk & (ci <= ri)).astype(BF16)
        keep = (ci // ch < ri // ch) | (same_chunk & (ci <= ri))
        end_row = ch - 1
        order = list(range(nc))
    pos = {c: p for p, c in enumerate(order)}

    def chunk(c):
        return slice(c * ch, (c + 1) * ch)

    blocks = range(GLA_TILE_BLOCKS)
    for blk in (reversed(blocks) if reverse else blocks):
        r0 = blk * GLA_BLOCK
        rows = slice(r0, r0 + GLA_BLOCK)
        x = jnp.dot(z_ref[0, rows, :], w2_ref[...],
                    preferred_element_type=F32) + b2_ref[...]
        la = (jnp.minimum(x, 0.0) - jnp.log(1.0 + jnp.exp(-jnp.abs(x)))) * (1.0 / GLA_TAU)
        hi = la.astype(BF16)
        lo = (la - hi.astype(F32)).astype(BF16)
        b_all = (jnp.dot(cum, hi, preferred_element_type=F32)
                 + jnp.dot(cum, lo, preferred_element_type=F32))
        total = []
        for hd in heads:
            cols = slice(hd * GLA_DK, (hd + 1) * GLA_DK)
            b = b_all[:, cols]
            ends = [b[c * ch + end_row:c * ch + end_row + 1, :] for c in range(nc)]
            tot = [jnp.zeros_like(ends[0])]
            for p in range(nc):
                tot.append(tot[-1] + ends[order[p]])
            total.append(tot)
            e_end = jnp.concatenate(
                [jnp.broadcast_to(jnp.exp(e), (ch, GLA_DK)) for e in ends], axis=0)
            ke = k_ref[0, rows, cols].astype(F32) * jnp.exp(-b)
            qe_scr[hd, blk] = (q_ref[0, rows, cols].astype(F32) * (GLA_DK ** -0.5)
                               * jnp.exp(b))
            ke_scr[hd, blk] = ke.astype(BF16)
            kd_scr[hd, blk] = ke * e_end
        att = []
        for hd in heads:
            tot = total[hd]
            att_rows = [None] * nc
            for p in range(nc):
                a = order[p]
                parts = []
                for c in range(nc):
                    if c == a:
                        parts.append(ke_scr[hd, blk, chunk(c), :])
                    elif pos[c] < p:
                        kd_c = kd_scr[hd, blk, chunk(c), :]
                        if pos[c] < p - 1:
                            kd_c = kd_c * jnp.exp(tot[p] - tot[pos[c] + 1])
                        parts.append(kd_c.astype(BF16))
                    else:
                        parts.append(jnp.zeros((ch, GLA_DK), BF16))
                att_rows[a] = lax.dot_general(
                    qe_scr[hd, blk, chunk(a), :].astype(BF16),
                    jnp.concatenate(parts, axis=0), NT_DIMS, preferred_element_type=F32)
            att.append(jnp.where(keep, jnp.concatenate(att_rows, axis=0), 0.0).astype(BF16))
        for hd in heads:
            tot = total[hd]
            vcols = slice(hd * GLA_DV, (hd + 1) * GLA_DV)
            st = st_ref[hd]
            qe_in = jnp.concatenate(
                [(qe_scr[hd, blk, chunk(c), :] * jnp.exp(tot[pos[c]])).astype(BF16)
                 for c in range(nc)], axis=0)
            v = v_ref[0, rows, vcols]
            o = (jnp.dot(att[hd], v, preferred_element_type=F32)
                 + lax.dot_general(qe_in, st.astype(BF16), NT_DIMS,
                                   preferred_element_type=F32))
            if reverse:
                o = o + of_ref[0, rows, vcols]
                ms = jnp.mean(o * o, axis=-1, keepdims=True)
                o = o * lax.rsqrt(ms + NORM_EPS) * g_ref[...]
                gz = gz_ref[0, rows, vcols].astype(F32)
                y_ref[0, rows, vcols] = (o * (gz / (1.0 + jnp.exp(-gz)))).astype(y_ref.dtype)
            else:
                y_ref[0, rows, vcols] = o
            k_all = jnp.concatenate(
                [(kd_scr[hd, blk, chunk(c), :]
                  * jnp.exp(tot[nc] - tot[pos[c] + 1])).astype(BF16)
                 for c in range(nc)], axis=0)
            vt = v.astype(F32).T.astype(BF16)
            st_ref[hd] = (st * jnp.exp(tot[nc])
                          + jnp.dot(vt, k_all, preferred_element_type=F32))


def _gla_direction(proj3, lr3, w2pad, bias, reverse, o_fwd=None, norm_g=None):
    bs, seq, _ = proj3.shape
    nt = seq // GLA_TILE
    nh = GLA_STEP_HEADS

    def tile(t):
        return nt - 1 - t if reverse else t

    def spec(width, off):
        return pl.BlockSpec((1, GLA_TILE, nh * width),
                            lambda b, h, t: (b, tile(t), off // (nh * width) + h))

    in_specs = [spec(GLA_DK, OFF_GQ), spec(GLA_DK, OFF_GK), spec(GLA_DV, OFF_GV),
                pl.BlockSpec((1, GLA_TILE, LR_PAD), lambda b, h, t: (b, tile(t), 0)),
                pl.BlockSpec((LR_PAD, nh * GLA_DK), lambda b, h, t: (0, h)),
                pl.BlockSpec((1, nh * GLA_DK), lambda b, h, t: (0, h))]
    args = [proj3, proj3, proj3, lr3, w2pad, bias]
    out_spec = pl.BlockSpec((1, GLA_TILE, nh * GLA_DV), lambda b, h, t: (b, tile(t), h))
    if reverse:
        in_specs += [out_spec, spec(GLA_DV, OFF_GZ),
                     pl.BlockSpec((1, GLA_DV), lambda b, h, t: (0, 0))]
        args += [o_fwd, proj3, norm_g]
    block_scratch = (nh, GLA_TILE_BLOCKS, GLA_BLOCK, GLA_DK)
    return pl.pallas_call(
        functools.partial(_gla_kernel, reverse=reverse),
        grid=(bs, GLA_HEADS // nh, nt),
        in_specs=in_specs,
        out_specs=out_spec,
        out_shape=jax.ShapeDtypeStruct((bs, seq, GLA_VAL), BF16 if reverse else F32),
        scratch_shapes=[pltpu.VMEM((nh, GLA_DV, GLA_DK), F32),
                        pltpu.VMEM(block_scratch, F32),
                        pltpu.VMEM(block_scratch, F32),
                        pltpu.VMEM(block_scratch, BF16)],
        compiler_params=_params(("parallel", "parallel", "arbitrary")),
        name="gla_bwd" if reverse else "gla_fwd",
    )(*args)


def _merge_kernel(ya_ref, yb_ref, wa_ref, wb_ref, ga_ref, gb_ref, o_ref):
    ua = jnp.dot(ya_ref[...], wa_ref[...].astype(BF16), preferred_element_type=F32)
    ub = jnp.dot(yb_ref[...], wb_ref[...].astype(BF16), preferred_element_type=F32)
    ga = ga_ref[...].astype(F32)
    gb = gb_ref[...].astype(F32)
    merged = ua / (1.0 + jnp.exp(-ga)) + ub / (1.0 + jnp.exp(-gb))
    o_ref[...] = merged.astype(o_ref.dtype)


def _merge(ya, yb, wa, wb, gates, tm=1024, tn=512):
    m = ya.shape[0]
    return pl.pallas_call(
        _merge_kernel,
        grid=(m // tm, D_MODEL // tn),
        in_specs=[pl.BlockSpec((tm, ATT_WIDTH), lambda i, j: (i, 0)),
                  pl.BlockSpec((tm, GLA_VAL), lambda i, j: (i, 0)),
                  pl.BlockSpec((ATT_WIDTH, tn), lambda i, j: (0, j)),
                  pl.BlockSpec((GLA_VAL, tn), lambda i, j: (0, j)),
                  pl.BlockSpec((tm, tn), lambda i, j: (i, j)),
                  pl.BlockSpec((tm, tn), lambda i, j: (i, D_MODEL // tn + j))],
        out_specs=pl.BlockSpec((tm, tn), lambda i, j: (i, j)),
        out_shape=jax.ShapeDtypeStruct((m, D_MODEL), BF16),
        compiler_params=_params(("parallel", "arbitrary")),
        name="merge",
    )(ya, yb, wa, wb, gates, gates)


def _out_kernel(x_ref, a_ref, w_ref, g_ref, o_ref, ss_ref, *, nj, tn):
    j = pl.program_id(1)
    u = x_ref[...] + jnp.dot(a_ref[...], w_ref[...], preferred_element_type=F32)
    o_ref[:, pl.ds(pl.multiple_of(j * tn, tn), tn)] = u
    part = jnp.sum(u * u, axis=-1, keepdims=True)

    @pl.when(j == 0)
    def _():
        ss_ref[...] = part

    @pl.when(j > 0)
    def _():
        ss_ref[...] += part

    @pl.when(j == nj - 1)
    def _():
        rs = lax.rsqrt(ss_ref[...] * (1.0 / D_MODEL) + NORM_EPS)
        for jj in range(nj):
            cols = slice(jj * tn, (jj + 1) * tn)
            o_ref[:, cols] = (o_ref[:, cols] * rs) * g_ref[:, cols]


def _out_proj(x, merged, w_o, g, tm=1024, tn=512):
    m = x.shape[0]
    nj = D_MODEL // tn
    return pl.pallas_call(
        functools.partial(_out_kernel, nj=nj, tn=tn),
        grid=(m // tm, nj),
        in_specs=[pl.BlockSpec((tm, tn), lambda i, j: (i, j)),
                  pl.BlockSpec((tm, D_MODEL), lambda i, j: (i, 0),
                               pipeline_mode=pl.Buffered(1)),
                  pl.BlockSpec((D_MODEL, tn), lambda i, j: (0, j)),
                  pl.BlockSpec((1, D_MODEL), lambda i, j: (0, 0))],
        out_specs=pl.BlockSpec((tm, D_MODEL), lambda i, j: (i, 0)),
        out_shape=jax.ShapeDtypeStruct((m, D_MODEL), F32),
        scratch_shapes=[pltpu.VMEM((tm, 1), F32)],
        compiler_params=_params(("parallel", "arbitrary")),
        name="out_proj",
    )(x, merged, w_o, g.reshape(1, D_MODEL))


IN_PROJ_TN = 1024


def _layer(x, w, final_g):
    bs, seq, d = x.shape
    m = bs * seq
    x2 = x.reshape(m, d)
    h, h4, h16, lr3 = _rmsnorm(x, w["norm_g"], w["wt_lr"])
    group_tiles = GROUP_COLS // IN_PROJ_TN
    proj = _matmul_nt(h.reshape(m, d), w["wt_a"], N_PROJ,
                      lambda j: jnp.where(j < group_tiles, j, j + 2 * group_tiles),
                      "in_proj")
    proj3 = proj.reshape(bs, seq, N_PROJ)
    cls = []
    for g, (dilation, hc) in enumerate(((4, h4), (16, h16)), start=1):
        p = _matmul_nt(hc.reshape(m, d), w["wt_a"], GROUP_COLS,
                       lambda j, g=g: j + g * group_tiles, f"in_proj_d{dilation}")
        cls.append(p.reshape(bs, dilation, seq // dilation, GROUP_COLS))
    gates = _matmul_nt(h.reshape(m, d), w["wt_a"], 2 * D_MODEL, lambda j: j, "in_proj_gates",
                       row0=W_OFF_MG)
    ya = _attention(w["slopes"], proj3, *cls)

    o_f = _gla_direction(proj3, lr3, w["w2_f"], w["b_f"], False)
    yb = _gla_direction(proj3, lr3, w["w2_b"], w["b_b"], True, o_f, w["gla_norm_g"])

    merged = _merge(ya.reshape(m, ATT_WIDTH), yb.reshape(m, GLA_VAL),
                    w["w_up_a"], w["w_up_b"], gates)
    y = _out_proj(x2, merged, w["w_o"], final_g)
    return y.reshape(bs, seq, d)


def _pad_rows(w2, row0):
    return jnp.zeros((LR_PAD, GLA_KEY), BF16).at[row0:row0 + GLA_RANK].set(w2.astype(BF16))


def kernel(x_prompt, x_sample, norm_g, w_in, gla_w2_f, gla_b_f, gla_w2_b, gla_b_b,
           gla_norm_g, w_up_a, w_up_b, w_o, final_norm_g):
    assert norm_g.shape[0] == 1, "single-layer kernel"
    wt = w_in[0].T
    w = {
        "norm_g": norm_g[0],
        "wt_a": wt,
        "wt_lr": jnp.pad(wt[W_OFF_LR:W_OFF_MG],
                         ((0, LR_PAD - 2 * GLA_RANK), (0, 0))).astype(BF16),
        "w2_f": _pad_rows(gla_w2_f[0], 0),
        "w2_b": _pad_rows(gla_w2_b[0], GLA_RANK),
        "b_f": gla_b_f[0].reshape(1, GLA_KEY),
        "b_b": gla_b_b[0].reshape(1, GLA_KEY),
        "gla_norm_g": gla_norm_g[0].reshape(1, GLA_DV),
        "w_up_a": w_up_a[0],
        "w_up_b": w_up_b[0],
        "w_o": w_o[0].astype(BF16),
        "slopes": jnp.exp2(-8.0 * (jnp.arange(ATT_HEADS, dtype=F32) + 1.0) / ATT_HEADS),
    }
    return (_layer(x_prompt, w, final_norm_g), _layer(x_sample, w, final_norm_g))
```

```python
import functools

import jax
import jax.numpy as jnp
import numpy as np
from jax import lax
from jax.experimental import pallas as pl
from jax.experimental.pallas import tpu as pltpu

F32 = jnp.float32
BF16 = jnp.bfloat16

D_MODEL = 4096
ATT_DILATIONS = (1, 4, 16)
N_GROUPS = 3
ATT_HEAD_DIM = 128
ATT_HEADS = 8
ATT_WIDTH = ATT_HEADS * ATT_HEAD_DIM
ATT_RADIUS = 64
GLA_HEADS = 4
GLA_KEY = 1024
GLA_VAL = 2048
GLA_DK = GLA_KEY // GLA_HEADS
GLA_DV = GLA_VAL // GLA_HEADS
GLA_RANK = 16
GLA_TAU = 16.0
GLA_CHUNK = 64
NORM_EPS = 1e-6
NEG_INF = -1e30

GROUP_COLS = 3 * ATT_WIDTH
W_OFF_AZ = N_GROUPS * GROUP_COLS
W_OFF_LR = W_OFF_AZ + ATT_WIDTH + 2 * GLA_KEY + 2 * GLA_VAL
W_OFF_MG = W_OFF_LR + 2 * GLA_RANK
OFF_AZ = GROUP_COLS
OFF_GQ = OFF_AZ + ATT_WIDTH
OFF_GK = OFF_GQ + GLA_KEY
OFF_GV = OFF_GK + GLA_KEY
OFF_GZ = OFF_GV + GLA_VAL
N_PROJ = OFF_GZ + GLA_VAL
LR_PAD = 128

VMEM_LIMIT = 56 * 1024 * 1024
BIG_VMEM_LIMIT = 62 * 1024 * 1024


def _params(sem):
    return pltpu.CompilerParams(dimension_semantics=sem, vmem_limit_bytes=VMEM_LIMIT)


NORM_ROWS = 256
NT_DIMS = (((1,), (1,)), ((), ()))


def _class_major_permutation():
    perm = np.zeros((2 * NORM_ROWS, NORM_ROWS), np.float32)
    for block, dilation in enumerate((4, 16)):
        per_class = NORM_ROWS // dilation
        for r in range(dilation):
            for u in range(per_class):
                perm[block * NORM_ROWS + r * per_class + u, u * dilation + r] = 1.0
    return jnp.asarray(perm, BF16)


def _rmsnorm_kernel(x_ref, g_ref, wlr_ref, perm_ref, h_ref, h4_ref, h16_ref, lr_ref):
    x = x_ref[0]
    ms = jnp.mean(x * x, axis=-1, keepdims=True)
    h = ((x * lax.rsqrt(ms + NORM_EPS)) * g_ref[...]).astype(h_ref.dtype)
    h_ref[0] = h
    lr_ref[0] = lax.dot_general(h, wlr_ref[...], NT_DIMS,
                                preferred_element_type=F32).astype(lr_ref.dtype)
    hp = jnp.dot(perm_ref[...], h, preferred_element_type=F32).astype(h_ref.dtype)
    for block, (out_ref, dilation) in enumerate(((h4_ref, 4), (h16_ref, 16))):
        per_class = NORM_ROWS // dilation
        for r in range(dilation):
            row0 = block * NORM_ROWS + r * per_class
            out_ref[0, r] = hp[row0:row0 + per_class]


def _rmsnorm(x, g, wt_lr):
    bs, seq, d = x.shape
    tm = NORM_ROWS

    def cls_spec(dilation):
        return pl.BlockSpec((1, dilation, tm // dilation, d), lambda b, i: (b, 0, i, 0))

    def cls_shape(dilation):
        return jax.ShapeDtypeStruct((bs, dilation, seq // dilation, d), BF16)

    return pl.pallas_call(
        _rmsnorm_kernel,
        grid=(bs, seq // tm),
        in_specs=[pl.BlockSpec((1, tm, d), lambda b, i: (b, i, 0)),
                  pl.BlockSpec((1, d), lambda b, i: (0, 0)),
                  pl.BlockSpec((LR_PAD, d), lambda b, i: (0, 0)),
                  pl.BlockSpec((2 * tm, tm), lambda b, i: (0, 0))],
        out_specs=[pl.BlockSpec((1, tm, d), lambda b, i: (b, i, 0)),
                   cls_spec(4), cls_spec(16),
                   pl.BlockSpec((1, tm, LR_PAD), lambda b, i: (b, i, 0))],
        out_shape=[jax.ShapeDtypeStruct((bs, seq, d), BF16), cls_shape(4), cls_shape(16),
                   jax.ShapeDtypeStruct((bs, seq, LR_PAD), BF16)],
        compiler_params=_params(("parallel", "parallel")),
        name="rmsnorm",
    )(x, g.reshape(1, d), wt_lr, _class_major_permutation())


def _matmul_nt_kernel(a_ref, bt_ref, o_ref):
    o_ref[...] = lax.dot_general(a_ref[...], bt_ref[...].astype(BF16), NT_DIMS,
                                 preferred_element_type=F32).astype(o_ref.dtype)


def _matmul_nt(a, bt, n_out, row_block, name, tm=1024, tn=1024, row0=0):
    m, k = a.shape
    return pl.pallas_call(
        _matmul_nt_kernel,
        grid=(n_out // tn, m // tm),
        in_specs=[pl.BlockSpec((tm, k), lambda j, i: (i, 0)),
                  pl.BlockSpec((pl.Element(tn), pl.Element(k)),
                               lambda j, i: (pl.multiple_of(row_block(j) * tn + row0, 8), 0))],
        out_specs=pl.BlockSpec((tm, tn), lambda j, i: (i, j)),
        out_shape=jax.ShapeDtypeStruct((m, n_out), BF16),
        compiler_params=pltpu.CompilerParams(dimension_semantics=("parallel", "arbitrary"),
                                             vmem_limit_bytes=BIG_VMEM_LIMIT),
        name=name,
    )(a, bt)


ATT_TILE = 2048
ATT_QB = 128
ATT_KW = ATT_QB + 2 * ATT_RADIUS
ATT_WINDOW_OFFSETS = (-ATT_RADIUS, 0, -2 * ATT_RADIUS)


def _attn_kernel(slopes_ref, q1_ref, k1_ref, v1_ref, q2_ref, k2_ref, v2_ref,
                 q3_ref, k3_ref, v3_ref, z_ref, y_ref, o_scr, l_scr, bias_scr, *, seq):
    h = pl.program_id(1)
    tile = pl.program_id(2)
    slope = slopes_ref[h]
    scale = ATT_HEAD_DIM ** -0.5

    @pl.when(tile == 0)
    def _():
        col_minus_row = (lax.broadcasted_iota(jnp.int32, (ATT_QB, ATT_KW), 1)
                         - lax.broadcasted_iota(jnp.int32, (ATT_QB, ATT_KW), 0))
        col = lax.broadcasted_iota(jnp.int32, (ATT_QB, ATT_KW), 1)
        for g, dilation in enumerate(ATT_DILATIONS):
            if seq // dilation >= ATT_KW:
                valid = [jnp.abs(col_minus_row + off) for off in ATT_WINDOW_OFFSETS]
            else:
                valid = [jnp.where(col // ATT_QB == c, jnp.abs(col_minus_row - c * ATT_QB),
                                   ATT_KW) for c in range(2)]
            for v, dist in enumerate(valid):
                bias_scr[g * len(ATT_WINDOW_OFFSETS) + v] = jnp.where(
                    dist <= ATT_RADIUS, -(slope * float(dilation)) * dist.astype(F32),
                    NEG_INF)

    for g, (q_ref, k_ref, v_ref) in enumerate(((q1_ref, k1_ref, v1_ref),
                                               (q2_ref, k2_ref, v2_ref),
                                               (q3_ref, k3_ref, v3_ref))):
        dilation = ATT_DILATIONS[g]
        length = seq // dilation
        per_class = ATT_TILE // dilation // ATT_QB
        paired = length < ATT_KW
        for r in range(dilation):
            for j in range(per_class):
                row0 = j * ATT_QB
                if paired:
                    kv_class, variant, w0 = r // 2, r % 2, 0
                else:
                    p0 = tile * (ATT_TILE // dilation) + row0
                    w0 = pl.multiple_of(jnp.clip(p0 - ATT_RADIUS, 0, length - ATT_KW),
                                        ATT_RADIUS)
                    off = w0 - p0
                    variant = jnp.where(off == ATT_WINDOW_OFFSETS[1], 1,
                                        jnp.where(off == ATT_WINDOW_OFFSETS[2], 2, 0))
                    kv_class = r
                bias = bias_scr[g * len(ATT_WINDOW_OFFSETS) + variant]
                q = q_ref[0, r, row0:row0 + ATT_QB, :]
                s = lax.dot_general(q, k_ref[0, kv_class, pl.ds(w0, ATT_KW), :], NT_DIMS,
                                    preferred_element_type=F32) * scale + bias
                m = jnp.max(s, axis=-1, keepdims=True)
                e = jnp.exp(s - m)
                l = jnp.sum(e, axis=-1, keepdims=True)
                o = jnp.dot(e.astype(BF16), v_ref[0, kv_class, pl.ds(w0, ATT_KW), :],
                            preferred_element_type=F32) / l
                lse = jnp.broadcast_to(m + jnp.log(l), (ATT_QB, ATT_HEAD_DIM))
                if dilation == 1:
                    rows = pl.ds(row0, ATT_QB)
                else:
                    rows = pl.ds(row0 * dilation + r, ATT_QB, stride=dilation)
                o_scr[g, rows, :] = o
                l_scr[g, rows, :] = lse

    la, lb, lc = l_scr[0], l_scr[1], l_scr[2]
    m = jnp.maximum(jnp.maximum(la, lb), lc)
    ea, eb, ec = jnp.exp(la - m), jnp.exp(lb - m), jnp.exp(lc - m)
    mix = (ea * o_scr[0] + eb * o_scr[1] + ec * o_scr[2]) / (ea + eb + ec)
    z = z_ref[0].astype(F32)
    y_ref[0] = (mix * (z / (1.0 + jnp.exp(-z)))).astype(y_ref.dtype)


def _attention(slopes, proj3, cls4, cls16):
    bs, seq, _ = proj3.shape
    n_tiles = seq // ATT_TILE
    nat = proj3.reshape(bs, 1, seq, N_PROJ)

    def kv_shape(dilation):
        length = seq // dilation
        return (dilation, length) if length >= ATT_KW else (dilation // 2, 2 * length)

    def specs(dilation):
        tq = ATT_TILE // dilation
        q = pl.BlockSpec((1, dilation, tq, ATT_HEAD_DIM), lambda b, h, t: (b, 0, t, h))
        k = pl.BlockSpec((1,) + kv_shape(dilation) + (ATT_HEAD_DIM,),
                         lambda b, h, t: (b, 0, 0, ATT_HEADS + h))
        v = pl.BlockSpec((1,) + kv_shape(dilation) + (ATT_HEAD_DIM,),
                         lambda b, h, t: (b, 0, 0, 2 * ATT_HEADS + h))
        return [q, k, v]

    def kv_view(t, dilation):
        return t.reshape((bs,) + kv_shape(dilation) + (t.shape[-1],))

    tok = pl.BlockSpec((1, ATT_TILE, ATT_HEAD_DIM),
                       lambda b, h, t: (b, t, OFF_AZ // ATT_HEAD_DIM + h))
    return pl.pallas_call(
        functools.partial(_attn_kernel, seq=seq),
        grid=(bs, ATT_HEADS, n_tiles),
        in_specs=([pl.BlockSpec(memory_space=pltpu.SMEM)]
                  + specs(1) + specs(4) + specs(16) + [tok]),
        out_specs=pl.BlockSpec((1, ATT_TILE, ATT_HEAD_DIM), lambda b, h, t: (b, t, h)),
        out_shape=jax.ShapeDtypeStruct((bs, seq, ATT_WIDTH), BF16),
        scratch_shapes=[pltpu.VMEM((N_GROUPS, ATT_TILE, ATT_HEAD_DIM), F32),
                        pltpu.VMEM((N_GROUPS, ATT_TILE, ATT_HEAD_DIM), F32),
                        pltpu.VMEM((N_GROUPS * len(ATT_WINDOW_OFFSETS), ATT_QB, ATT_KW),
                                   F32)],
        compiler_params=_params(("parallel", "parallel", "arbitrary")),
        name="attention",
    )(slopes, nat, nat, nat, cls4, kv_view(cls4, 4), kv_view(cls4, 4),
      cls16, kv_view(cls16, 16), kv_view(cls16, 16), proj3)


GLA_BLOCK_CHUNKS = 4
GLA_BLOCK = GLA_BLOCK_CHUNKS * GLA_CHUNK
GLA_TILE_BLOCKS = 2
GLA_TILE = GLA_TILE_BLOCKS * GLA_BLOCK
GLA_STEP_HEADS = 4


def _gla_kernel(*refs, reverse):
    if reverse:
        (q_ref, k_ref, v_ref, z_ref, w2_ref, b2_ref, of_ref, gz_ref, g_ref,
         y_ref, st_ref, qe_scr, kd_scr, ke_scr) = refs
    else:
        (q_ref, k_ref, v_ref, z_ref, w2_ref, b2_ref,
         y_ref, st_ref, qe_scr, kd_scr, ke_scr) = refs

    @pl.when(pl.program_id(2) == 0)
    def _():
        st_ref[...] = jnp.zeros_like(st_ref)

    nc, ch = GLA_BLOCK_CHUNKS, GLA_CHUNK
    heads = range(GLA_STEP_HEADS)
    ri = lax.broadcasted_iota(jnp.int32, (GLA_BLOCK, GLA_BLOCK), 0)
    ci = lax.broadcasted_iota(jnp.int32, (GLA_BLOCK, GLA_BLOCK), 1)
    same_chunk = (ri // ch) == (ci // ch)
    if reverse:
        cum = (same_chunk & (ci >= ri)).astype(BF16)
        keep = (ci // ch > ri // ch) | (same_chunk & (ci > ri))
        end_row = 0
        order = list(range(nc - 1, -1, -1))
    else:
        cum = (same_chunk & (ci <= ri)).astype(BF16)
        keep = (ci // ch < ri // ch) | (same_chunk & (ci <= ri))
        end_row = ch - 1
        order = list(range(nc))
    pos = {c: p for p, c in enumerate(order)}

    def chunk(c):
        return slice(c * ch, (c + 1) * ch)

    blocks = range(GLA_TILE_BLOCKS)
    for blk in (reversed(blocks) if reverse else blocks):
        r0 = blk * GLA_BLOCK
        rows = slice(r0, r0 + GLA_BLOCK)
        x = jnp.dot(z_ref[0, rows, :], w2_ref[...],
                    preferred_element_type=F32) + b2_ref[...]
        la = (jnp.minimum(x, 0.0) - jnp.log(1.0 + jnp.exp(-jnp.abs(x)))) * (1.0 / GLA_TAU)
        hi = la.astype(BF16)
        lo = (la - hi.astype(F32)).astype(BF16)
        b_all = (jnp.dot(cum, hi, preferred_element_type=F32)
                 + jnp.dot(cum, lo, preferred_element_type=F32))
        total = []
        for hd in heads:
            cols = slice(hd * GLA_DK, (hd + 1) * GLA_DK)
            b = b_all[:, cols]
            ends = [b[c * ch + end_row:c * ch + end_row + 1, :] for c in range(nc)]
            tot = [jnp.zeros_like(ends[0])]
            for p in range(nc):
                tot.append(tot[-1] + ends[order[p]])
            total.append(tot)
            e_end = jnp.concatenate(
                [jnp.broadcast_to(jnp.exp(e), (ch, GLA_DK)) for e in ends], axis=0)
            ke = k_ref[0, rows, cols].astype(F32) * jnp.exp(-b)
            qe_scr[hd, blk] = (q_ref[0, rows, cols].astype(F32) * (GLA_DK ** -0.5)
                               * jnp.exp(b))
            ke_scr[hd, blk] = ke.astype(BF16)
            kd_scr[hd, blk] = ke * e_end
        att = []
        for hd in heads:
            tot = total[hd]
            att_rows = [None] * nc
            for p in range(nc):
                a = order[p]
                parts = []
                for c in range(nc):
                    if c == a:
                        parts.append(ke_scr[hd, blk, chunk(c), :])
                    elif pos[c] < p:
                        kd_c = kd_scr[hd, blk, chunk(c), :]
                        if pos[c] < p - 1:
                            kd_c = kd_c * jnp.exp(tot[p] - tot[pos[c] + 1])
                        parts.append(kd_c.astype(BF16))
                    else:
                        parts.append(jnp.zeros((ch, GLA_DK), BF16))
                att_rows[a] = lax.dot_general(
                    qe_scr[hd, blk, chunk(a), :].astype(BF16),
                    jnp.concatenate(parts, axis=0), NT_DIMS, preferred_element_type=F32)
            att.append(jnp.where(keep, jnp.concatenate(att_rows, axis=0), 0.0).astype(BF16))
        for hd in heads:
            tot = total[hd]
            vcols = slice(hd * GLA_DV, (hd + 1) * GLA_DV)
            st = st_ref[hd]
            qe_in = jnp.concatenate(
                [(qe_scr[hd, blk, chunk(c), :] * jnp.exp(tot[pos[c]])).astype(BF16)
                 for c in range(nc)], axis=0)
            v = v_ref[0, rows, vcols]
            o = (jnp.dot(att[hd], v, preferred_element_type=F32)
                 + lax.dot_general(qe_in, st.astype(BF16), NT_DIMS,
                                   preferred_element_type=F32))
            if reverse:
                o = o + of_ref[0, rows, vcols]
                ms = jnp.mean(o * o, axis=-1, keepdims=True)
                o = o * lax.rsqrt(ms + NORM_EPS) * g_ref[...]
                gz = gz_ref[0, rows, vcols].astype(F32)
                y_ref[0, rows, vcols] = (o * (gz / (1.0 + jnp.exp(-gz)))).astype(y_ref.dtype)
            else:
                y_ref[0, rows, vcols] = o
            k_all = jnp.concatenate(
                [(kd_scr[hd, blk, chunk(c), :]
                  * jnp.exp(tot[nc] - tot[pos[c] + 1])).astype(BF16)
                 for c in range(nc)], axis=0)
            vt = v.astype(F32).T.astype(BF16)
            st_ref[hd] = (st * jnp.exp(tot[nc])
                          + jnp.dot(vt, k_all, preferred_element_type=F32))


def _gla_direction(proj3, lr3, w2pad, bias, reverse, o_fwd=None, norm_g=None):
    bs, seq, _ = proj3.shape
    nt = seq // GLA_TILE
    nh = GLA_STEP_HEADS

    def tile(t):
        return nt - 1 - t if reverse else t

    def spec(width, off):
        return pl.BlockSpec((1, GLA_TILE, nh * width),
                            lambda b, h, t: (b, tile(t), off // (nh * width) + h))

    in_specs = [spec(GLA_DK, OFF_GQ), spec(GLA_DK, OFF_GK), spec(GLA_DV, OFF_GV),
                pl.BlockSpec((1, GLA_TILE, LR_PAD), lambda b, h, t: (b, tile(t), 0)),
                pl.BlockSpec((LR_PAD, nh * GLA_DK), lambda b, h, t: (0, h)),
                pl.BlockSpec((1, nh * GLA_DK), lambda b, h, t: (0, h))]
    args = [proj3, proj3, proj3, lr3, w2pad, bias]
    out_spec = pl.BlockSpec((1, GLA_TILE, nh * GLA_DV), lambda b, h, t: (b, tile(t), h))
    if reverse:
        in_specs += [out_spec, spec(GLA_DV, OFF_GZ),
                     pl.BlockSpec((1, GLA_DV), lambda b, h, t: (0, 0))]
        args += [o_fwd, proj3, norm_g]
    block_scratch = (nh, GLA_TILE_BLOCKS, GLA_BLOCK, GLA_DK)
    return pl.pallas_call(
        functools.partial(_gla_kernel, reverse=reverse),
        grid=(bs, GLA_HEADS // nh, nt),
        in_specs=in_specs,
        out_specs=out_spec,
        out_shape=jax.ShapeDtypeStruct((bs, seq, GLA_VAL), BF16 if reverse else F32),
        scratch_shapes=[pltpu.VMEM((nh, GLA_DV, GLA_DK), F32),
                        pltpu.VMEM(block_scratch, F32),
                        pltpu.VMEM(block_scratch, F32),
                        pltpu.VMEM(block_scratch, BF16)],
        compiler_params=_params(("parallel", "parallel", "arbitrary")),
        name="gla_bwd" if reverse else "gla_fwd",
    )(*args)


def _merge_kernel(ya_ref, yb_ref, wa_ref, wb_ref, ga_ref, gb_ref, o_ref):
    ua = jnp.dot(ya_ref[...], wa_ref[...].astype(BF16), preferred_element_type=F32)
    ub = jnp.dot(yb_ref[...], wb_ref[...].astype(BF16), preferred_element_type=F32)
    ga = ga_ref[...].astype(F32)
    gb = gb_ref[...].astype(F32)
    merged = ua / (1.0 + jnp.exp(-ga)) + ub / (1.0 + jnp.exp(-gb))
    o_ref[...] = merged.astype(o_ref.dtype)


def _merge(ya, yb, wa, wb, gates, tm=1024, tn=1024):
    m = ya.shape[0]
    return pl.pallas_call(
        _merge_kernel,
        grid=(m // tm, D_MODEL // tn),
        in_specs=[pl.BlockSpec((tm, ATT_WIDTH), lambda i, j: (i, 0)),
                  pl.BlockSpec((tm, GLA_VAL), lambda i, j: (i, 0)),
                  pl.BlockSpec((ATT_WIDTH, tn), lambda i, j: (0, j)),
                  pl.BlockSpec((GLA_VAL, tn), lambda i, j: (0, j)),
                  pl.BlockSpec((tm, tn), lambda i, j: (i, j)),
                  pl.BlockSpec((tm, tn), lambda i, j: (i, D_MODEL // tn + j))],
        out_specs=pl.BlockSpec((tm, tn), lambda i, j: (i, j)),
        out_shape=jax.ShapeDtypeStruct((m, D_MODEL), BF16),
        compiler_params=pltpu.CompilerParams(dimension_semantics=("parallel", "arbitrary"),
                                             vmem_limit_bytes=BIG_VMEM_LIMIT),
        name="merge",
    )(ya, yb, wa, wb, gates, gates)


def _out_kernel(x_ref, a_ref, w_ref, g_ref, o_ref, ss_ref, *, nj, tn):
    j = pl.program_id(1)
    u = x_ref[...] + jnp.dot(a_ref[...], w_ref[...], preferred_element_type=F32)
    o_ref[:, pl.ds(pl.multiple_of(j * tn, tn), tn)] = u
    part = jnp.sum(u * u, axis=-1, keepdims=True)

    @pl.when(j == 0)
    def _():
        ss_ref[...] = part

    @pl.when(j > 0)
    def _():
        ss_ref[...] += part

    @pl.when(j == nj - 1)
    def _():
        rs = lax.rsqrt(ss_ref[...] * (1.0 / D_MODEL) + NORM_EPS)
        for jj in range(nj):
            cols = slice(jj * tn, (jj + 1) * tn)
            o_ref[:, cols] = (o_ref[:, cols] * rs) * g_ref[:, cols]


def _out_proj(x, merged, w_o, g, tm=1024, tn=512):
    m = x.shape[0]
    nj = D_MODEL // tn
    return pl.pallas_call(
        functools.partial(_out_kernel, nj=nj, tn=tn),
        grid=(m // tm, nj),
        in_specs=[pl.BlockSpec((tm, tn), lambda i, j: (i, j)),
                  pl.BlockSpec((tm, D_MODEL), lambda i, j: (i, 0),
                               pipeline_mode=pl.Buffered(1)),
                  pl.BlockSpec((D_MODEL, tn), lambda i, j: (0, j)),
                  pl.BlockSpec((1, D_MODEL), lambda i, j: (0, 0))],
        out_specs=pl.BlockSpec((tm, D_MODEL), lambda i, j: (i, 0)),
        out_shape=jax.ShapeDtypeStruct((m, D_MODEL), F32),
        scratch_shapes=[pltpu.VMEM((tm, 1), F32)],
        compiler_params=_params(("parallel", "arbitrary")),
        name="out_proj",
    )(x, merged, w_o, g.reshape(1, D_MODEL))


IN_PROJ_TN = 1024


def _layer(x, w, final_g):
    bs, seq, d = x.shape
    m = bs * seq
    x2 = x.reshape(m, d)
    h, h4, h16, lr3 = _rmsnorm(x, w["norm_g"], w["wt_lr"])
    group_tiles = GROUP_COLS // IN_PROJ_TN
    proj = _matmul_nt(h.reshape(m, d), w["wt_a"], N_PROJ,
                      lambda j: jnp.where(j < group_tiles, j, j + 2 * group_tiles),
                      "in_proj")
    proj3 = proj.reshape(bs, seq, N_PROJ)
    cls = []
    for g, (dilation, hc) in enumerate(((4, h4), (16, h16)), start=1):
        p = _matmul_nt(hc.reshape(m, d), w["wt_a"], GROUP_COLS,
                       lambda j, g=g: j + g * group_tiles, f"in_proj_d{dilation}")
        cls.append(p.reshape(bs, dilation, seq // dilation, GROUP_COLS))
    gates = _matmul_nt(h.reshape(m, d), w["wt_a"], 2 * D_MODEL, lambda j: j, "in_proj_gates",
                       row0=W_OFF_MG)
    ya = _attention(w["slopes"], proj3, *cls)

    o_f = _gla_direction(proj3, lr3, w["w2_f"], w["b_f"], False)
    yb = _gla_direction(proj3, lr3, w["w2_b"], w["b_b"], True, o_f, w["gla_norm_g"])

    merged = _merge(ya.reshape(m, ATT_WIDTH), yb.reshape(m, GLA_VAL),
                    w["w_up_a"], w["w_up_b"], gates)
    y = _out_proj(x2, merged, w["w_o"], final_g)
    return y.reshape(bs, seq, d)


def _pad_rows(w2, row0):
    return jnp.zeros((LR_PAD, GLA_KEY), BF16).at[row0:row0 + GLA_RANK].set(w2.astype(BF16))


def kernel(x_prompt, x_sample, norm_g, w_in, gla_w2_f, gla_b_f, gla_w2_b, gla_b_b,
           gla_norm_g, w_up_a, w_up_b, w_o, final_norm_g):
    assert norm_g.shape[0] == 1, "single-layer kernel"
    wt = w_in[0].T
    w = {
        "norm_g": norm_g[0],
        "wt_a": wt,
        "wt_lr": jnp.pad(wt[W_OFF_LR:W_OFF_MG],
                         ((0, LR_PAD - 2 * GLA_RANK), (0, 0))).astype(BF16),
        "w2_f": _pad_rows(gla_w2_f[0], 0),
        "w2_b": _pad_rows(gla_w2_b[0], GLA_RANK),
        "b_f": gla_b_f[0].reshape(1, GLA_KEY),
        "b_b": gla_b_b[0].reshape(1, GLA_KEY),
        "gla_norm_g": gla_norm_g[0].reshape(1, GLA_DV),
        "w_up_a": w_up_a[0],
        "w_up_b": w_up_b[0],
        "w_o": w_o[0].astype(BF16),
        "slopes": jnp.exp2(-8.0 * (jnp.arange(ATT_HEADS, dtype=F32) + 1.0) / ATT_HEADS),
    }
    return (_layer(x_prompt, w, final_norm_g), _layer(x_sample, w, final_norm_g))
```

```python
import functools

import jax
import jax.numpy as jnp
import numpy as np
from jax import lax
from jax.experimental import pallas as pl
from jax.experimental.pallas import tpu as pltpu

F32 = jnp.float32
BF16 = jnp.bfloat16

D_MODEL = 4096
ATT_DILATIONS = (1, 4, 16)
N_GROUPS = 3
ATT_HEAD_DIM = 128
ATT_HEADS = 8
ATT_WIDTH = ATT_HEADS * ATT_HEAD_DIM
ATT_RADIUS = 64
GLA_HEADS = 4
GLA_KEY = 1024
GLA_VAL = 2048
GLA_DK = GLA_KEY // GLA_HEADS
GLA_DV = GLA_VAL // GLA_HEADS
GLA_RANK = 16
GLA_TAU = 16.0
GLA_CHUNK = 64
NORM_EPS = 1e-6
NEG_INF = -1e30

GROUP_COLS = 3 * ATT_WIDTH
W_OFF_AZ = N_GROUPS * GROUP_COLS
W_OFF_LR = W_OFF_AZ + ATT_WIDTH + 2 * GLA_KEY + 2 * GLA_VAL
W_OFF_MG = W_OFF_LR + 2 * GLA_RANK
OFF_AZ = GROUP_COLS
OFF_GQ = OFF_AZ + ATT_WIDTH
OFF_GK = OFF_GQ + GLA_KEY
OFF_GV = OFF_GK + GLA_KEY
OFF_GZ = OFF_GV + GLA_VAL
N_PROJ = OFF_GZ + GLA_VAL
LR_PAD = 128

VMEM_LIMIT = 56 * 1024 * 1024
BIG_VMEM_LIMIT = 62 * 1024 * 1024


def _params(sem):
    return pltpu.CompilerParams(dimension_semantics=sem, vmem_limit_bytes=VMEM_LIMIT)


NORM_ROWS = 256
NT_DIMS = (((1,), (1,)), ((), ()))


def _class_major_permutation():
    perm = np.zeros((2 * NORM_ROWS, NORM_ROWS), np.float32)
    for block, dilation in enumerate((4, 16)):
        per_class = NORM_ROWS // dilation
        for r in range(dilation):
            for u in range(per_class):
                perm[block * NORM_ROWS + r * per_class + u, u * dilation + r] = 1.0
    return jnp.asarray(perm, BF16)


def _rmsnorm_kernel(x_ref, g_ref, wlr_ref, perm_ref, h_ref, h4_ref, h16_ref, lr_ref):
    x = x_ref[0]
    ms = jnp.mean(x * x, axis=-1, keepdims=True)
    h = ((x * lax.rsqrt(ms + NORM_EPS)) * g_ref[...]).astype(h_ref.dtype)
    h_ref[0] = h
    lr_ref[0] = lax.dot_general(h, wlr_ref[...], NT_DIMS,
                                preferred_element_type=F32).astype(lr_ref.dtype)
    hp = jnp.dot(perm_ref[...], h, preferred_element_type=F32).astype(h_ref.dtype)
    for block, (out_ref, dilation) in enumerate(((h4_ref, 4), (h16_ref, 16))):
        per_class = NORM_ROWS // dilation
        for r in range(dilation):
            row0 = block * NORM_ROWS + r * per_class
            out_ref[0, r] = hp[row0:row0 + per_class]


def _rmsnorm(x, g, wt_lr):
    bs, seq, d = x.shape
    tm = NORM_ROWS

    def cls_spec(dilation):
        return pl.BlockSpec((1, dilation, tm // dilation, d), lambda b, i: (b, 0, i, 0))

    def cls_shape(dilation):
        return jax.ShapeDtypeStruct((bs, dilation, seq // dilation, d), BF16)

    return pl.pallas_call(
        _rmsnorm_kernel,
        grid=(bs, seq // tm),
        in_specs=[pl.BlockSpec((1, tm, d), lambda b, i: (b, i, 0)),
                  pl.BlockSpec((1, d), lambda b, i: (0, 0)),
                  pl.BlockSpec((LR_PAD, d), lambda b, i: (0, 0)),
                  pl.BlockSpec((2 * tm, tm), lambda b, i: (0, 0))],
        out_specs=[pl.BlockSpec((1, tm, d), lambda b, i: (b, i, 0)),
                   cls_spec(4), cls_spec(16),
                   pl.BlockSpec((1, tm, LR_PAD), lambda b, i: (b, i, 0))],
        out_shape=[jax.ShapeDtypeStruct((bs, seq, d), BF16), cls_shape(4), cls_shape(16),
                   jax.ShapeDtypeStruct((bs, seq, LR_PAD), BF16)],
        compiler_params=_params(("parallel", "parallel")),
        name="rmsnorm",
    )(x, g.reshape(1, d), wt_lr, _class_major_permutation())


def _matmul_nt_kernel(a_ref, bt_ref, o_ref):
    o_ref[...] = lax.dot_general(a_ref[...], bt_ref[...].astype(BF16), NT_DIMS,
                                 preferred_element_type=F32).astype(o_ref.dtype)


def _matmul_nt(a, bt, n_out, row_block, name, tm=1024, tn=1024, row0=0):
    m, k = a.shape
    return pl.pallas_call(
        _matmul_nt_kernel,
        grid=(n_out // tn, m // tm),
        in_specs=[pl.BlockSpec((tm, k), lambda j, i: (i, 0)),
                  pl.BlockSpec((pl.Element(tn), pl.Element(k)),
                               lambda j, i: (pl.multiple_of(row_block(j) * tn + row0, 8), 0))],
        out_specs=pl.BlockSpec((tm, tn), lambda j, i: (i, j)),
        out_shape=jax.ShapeDtypeStruct((m, n_out), BF16),
        compiler_params=pltpu.CompilerParams(dimension_semantics=("parallel", "arbitrary"),
                                             vmem_limit_bytes=BIG_VMEM_LIMIT),
        name=name,
    )(a, bt)


ATT_TILE = 2048
ATT_QB = 128
ATT_KW = ATT_QB + 2 * ATT_RADIUS
ATT_WINDOW_OFFSETS = (-ATT_RADIUS, 0, -2 * ATT_RADIUS)


def _attn_kernel(slopes_ref, q1_ref, k1_ref, v1_ref, q2_ref, k2_ref, v2_ref,
                 q3_ref, k3_ref, v3_ref, z_ref, y_ref, o_scr, l_scr, bias_scr, *, seq):
    h = pl.program_id(1)
    tile = pl.program_id(2)
    slope = slopes_ref[h]
    scale = ATT_HEAD_DIM ** -0.5

    @pl.when(tile == 0)
    def _():
        col_minus_row = (lax.broadcasted_iota(jnp.int32, (ATT_QB, ATT_KW), 1)
                         - lax.broadcasted_iota(jnp.int32, (ATT_QB, ATT_KW), 0))
        col = lax.broadcasted_iota(jnp.int32, (ATT_QB, ATT_KW), 1)
        for g, dilation in enumerate(ATT_DILATIONS):
            if seq // dilation >= ATT_KW:
                valid = [jnp.abs(col_minus_row + off) for off in ATT_WINDOW_OFFSETS]
            else:
                valid = [jnp.where(col // ATT_QB == c, jnp.abs(col_minus_row - c * ATT_QB),
                                   ATT_KW) for c in range(2)]
            for v, dist in enumerate(valid):
                bias_scr[g * len(ATT_WINDOW_OFFSETS) + v] = jnp.where(
                    dist <= ATT_RADIUS, -(slope * float(dilation)) * dist.astype(F32),
                    NEG_INF)

    for g, (q_ref, k_ref, v_ref) in enumerate(((q1_ref, k1_ref, v1_ref),
                                               (q2_ref, k2_ref, v2_ref),
                                               (q3_ref, k3_ref, v3_ref))):
        dilation = ATT_DILATIONS[g]
        length = seq // dilation
        per_class = ATT_TILE // dilation // ATT_QB
        paired = length < ATT_KW
        for r in range(dilation):
            for j in range(per_class):
                row0 = j * ATT_QB
                if paired:
                    kv_class, variant, w0 = r // 2, r % 2, 0
                else:
                    p0 = tile * (ATT_TILE // dilation) + row0
                    w0 = pl.multiple_of(jnp.clip(p0 - ATT_RADIUS, 0, length - ATT_KW),
                                        ATT_RADIUS)
                    off = w0 - p0
                    variant = jnp.where(off == ATT_WINDOW_OFFSETS[1], 1,
                                        jnp.where(off == ATT_WINDOW_OFFSETS[2], 2, 0))
                    kv_class = r
                bias = bias_scr[g * len(ATT_WINDOW_OFFSETS) + variant]
                q = q_ref[0, r, row0:row0 + ATT_QB, :]
                s = lax.dot_general(q, k_ref[0, kv_class, pl.ds(w0, ATT_KW), :], NT_DIMS,
                                    preferred_element_type=F32) * scale + bias
                m = jnp.max(s, axis=-1, keepdims=True)
                e = jnp.exp(s - m)
                l = jnp.sum(e, axis=-1, keepdims=True)
                o = jnp.dot(e.astype(BF16), v_ref[0, kv_class, pl.ds(w0, ATT_KW), :],
                            preferred_element_type=F32) / l
                lse = jnp.broadcast_to(m + jnp.log(l), (ATT_QB, ATT_HEAD_DIM))
                if dilation == 1:
                    rows = pl.ds(row0, ATT_QB)
                else:
                    rows = pl.ds(row0 * dilation + r, ATT_QB, stride=dilation)
                o_scr[g, rows, :] = o
                l_scr[g, rows, :] = lse

    la, lb, lc = l_scr[0], l_scr[1], l_scr[2]
    m = jnp.maximum(jnp.maximum(la, lb), lc)
    ea, eb, ec = jnp.exp(la - m), jnp.exp(lb - m), jnp.exp(lc - m)
    mix = (ea * o_scr[0] + eb * o_scr[1] + ec * o_scr[2]) / (ea + eb + ec)
    z = z_ref[0].astype(F32)
    y_ref[0] = (mix * (z / (1.0 + jnp.exp(-z)))).astype(y_ref.dtype)


def _attention(slopes, proj3, cls4, cls16):
    bs, seq, _ = proj3.shape
    n_tiles = seq // ATT_TILE
    nat = proj3.reshape(bs, 1, seq, N_PROJ)

    def kv_shape(dilation):
        length = seq // dilation
        return (dilation, length) if length >= ATT_KW else (dilation // 2, 2 * length)

    def specs(dilation):
        tq = ATT_TILE // dilation
        q = pl.BlockSpec((1, dilation, tq, ATT_HEAD_DIM), lambda b, h, t: (b, 0, t, h))
        k = pl.BlockSpec((1,) + kv_shape(dilation) + (ATT_HEAD_DIM,),
                         lambda b, h, t: (b, 0, 0, ATT_HEADS + h))
        v = pl.BlockSpec((1,) + kv_shape(dilation) + (ATT_HEAD_DIM,),
                         lambda b, h, t: (b, 0, 0, 2 * ATT_HEADS + h))
        return [q, k, v]

    def kv_view(t, dilation):
        return t.reshape((bs,) + kv_shape(dilation) + (t.shape[-1],))

    tok = pl.BlockSpec((1, ATT_TILE, ATT_HEAD_DIM),
                       lambda b, h, t: (b, t, OFF_AZ // ATT_HEAD_DIM + h))
    return pl.pallas_call(
        functools.partial(_attn_kernel, seq=seq),
        grid=(bs, ATT_HEADS, n_tiles),
        in_specs=([pl.BlockSpec(memory_space=pltpu.SMEM)]
                  + specs(1) + specs(4) + specs(16) + [tok]),
        out_specs=pl.BlockSpec((1, ATT_TILE, ATT_HEAD_DIM), lambda b, h, t: (b, t, h)),
        out_shape=jax.ShapeDtypeStruct((bs, seq, ATT_WIDTH), BF16),
        scratch_shapes=[pltpu.VMEM((N_GROUPS, ATT_TILE, ATT_HEAD_DIM), F32),
                        pltpu.VMEM((N_GROUPS, ATT_TILE, ATT_HEAD_DIM), F32),
                        pltpu.VMEM((N_GROUPS * len(ATT_WINDOW_OFFSETS), ATT_QB, ATT_KW),
                                   F32)],
        compiler_params=_params(("parallel", "parallel", "arbitrary")),
        name="attention",
    )(slopes, nat, nat, nat, cls4, kv_view(cls4, 4), kv_view(cls4, 4),
      cls16, kv_view(cls16, 16), kv_view(cls16, 16), proj3)


GLA_BLOCK_CHUNKS = 4
GLA_BLOCK = GLA_BLOCK_CHUNKS * GLA_CHUNK
GLA_TILE_BLOCKS = 2
GLA_TILE = GLA_TILE_BLOCKS * GLA_BLOCK
GLA_STEP_HEADS = 4


def _gla_kernel(*refs, reverse):
    if reverse:
        (q_ref, k_ref, v_ref, z_ref, w2_ref, b2_ref, of_ref, gz_ref, g_ref,
         y_ref, st_ref, qe_scr, kd_scr, ke_scr) = refs
    else:
        (q_ref, k_ref, v_ref, z_ref, w2_ref, b2_ref,
         y_ref, st_ref, qe_scr, kd_scr, ke_scr) = refs

    @pl.when(pl.program_id(2) == 0)
    def _():
        st_ref[...] = jnp.zeros_like(st_ref)

    nc, ch = GLA_BLOCK_CHUNKS, GLA_CHUNK
    heads = range(GLA_STEP_HEADS)
    ri = lax.broadcasted_iota(jnp.int32, (GLA_BLOCK, GLA_BLOCK), 0)
    ci = lax.broadcasted_iota(jnp.int32, (GLA_BLOCK, GLA_BLOCK), 1)
    same_chunk = (ri // ch) == (ci // ch)
    if reverse:
        cum = (same_chunk & (ci >= ri)).astype(BF16)
        keep = (ci // ch > ri // ch) | (same_chunk & (ci > ri))
        end_row = 0
        order = list(range(nc - 1, -1, -1))
    else:
        cum = (same_chunk & (ci <= ri)).astype(BF16)
        keep = (ci // ch < ri // ch) | (same_chunk & (ci <= ri))
        end_row = ch - 1
        order = list(range(nc))
    pos = {c: p for p, c in enumerate(order)}

    def chunk(c):
        return slice(c * ch, (c + 1) * ch)

    blocks = range(GLA_TILE_BLOCKS)
    for blk in (reversed(blocks) if reverse else blocks):
        r0 = blk * GLA_BLOCK
        rows = slice(r0, r0 + GLA_BLOCK)
        x = jnp.dot(z_ref[0, rows, :], w2_ref[...],
                    preferred_element_type=F32) + b2_ref[...]
        la = (jnp.minimum(x, 0.0) - jnp.log(1.0 + jnp.exp(-jnp.abs(x)))) * (1.0 / GLA_TAU)
        hi = la.astype(BF16)
        lo = (la - hi.astype(F32)).astype(BF16)
        b_all = (jnp.dot(cum, hi, preferred_element_type=F32)
                 + jnp.dot(cum, lo, preferred_element_type=F32))
        total = []
        for hd in heads:
            cols = slice(hd * GLA_DK, (hd + 1) * GLA_DK)
            b = b_all[:, cols]
            ends = [b[c * ch + end_row:c * ch + end_row + 1, :] for c in range(nc)]
            tot = [jnp.zeros_like(ends[0])]
            for p in range(nc):
                tot.append(tot[-1] + ends[order[p]])
            total.append(tot)
            e_end = jnp.concatenate(
                [jnp.broadcast_to(jnp.exp(e), (ch, GLA_DK)) for e in ends], axis=0)
            ke = k_ref[0, rows, cols].astype(F32) * jnp.exp(-b)
            qe_scr[hd, blk] = (q_ref[0, rows, cols].astype(F32) * (GLA_DK ** -0.5)
                               * jnp.exp(b))
            ke_scr[hd, blk] = ke.astype(BF16)
            kd_scr[hd, blk] = ke * e_end
        att = []
        for hd in heads:
            tot = total[hd]
            att_rows = [None] * nc
            for p in range(nc):
                a = order[p]
                parts = []
                for c in range(nc):
                    if c == a:
                        parts.append(ke_scr[hd, blk, chunk(c), :])
                    elif pos[c] < p:
                        kd_c = kd_scr[hd, blk, chunk(c), :]
                        if pos[c] < p - 1:
                            kd_c = kd_c * jnp.exp(tot[p] - tot[pos[c] + 1])
                        parts.append(kd_c.astype(BF16))
                    else:
                        parts.append(jnp.zeros((ch, GLA_DK), BF16))
                att_rows[a] = lax.dot_general(
                    qe_scr[hd, blk, chunk(a), :].astype(BF16),
                    jnp.concatenate(parts, axis=0), NT_DIMS, preferred_element_type=F32)
            att.append(jnp.where(keep, jnp.concatenate(att_rows, axis=0), 0.0).astype(BF16))
        for hd in heads:
            tot = total[hd]
            vcols = slice(hd * GLA_DV, (hd + 1) * GLA_DV)
            st = st_ref[hd]
            qe_in = jnp.concatenate(
                [(qe_scr[hd, blk, chunk(c), :] * jnp.exp(tot[pos[c]])).astype(BF16)
                 for c in range(nc)], axis=0)
            v = v_ref[0, rows, vcols]
            o = (jnp.dot(att[hd], v, preferred_element_type=F32)
                 + lax.dot_general(qe_in, st.astype(BF16), NT_DIMS,
                                   preferred_element_type=F32))
            if reverse:
                o = o + of_ref[0, rows, vcols]
                ms = jnp.mean(o * o, axis=-1, keepdims=True)
                o = o * lax.rsqrt(ms + NORM_EPS) * g_ref[...]
                gz = gz_ref[0, rows, vcols].astype(F32)
                y_ref[0, rows, vcols] = (o * (gz / (1.0 + jnp.exp(-gz)))).astype(y_ref.dtype)
            else:
                y_ref[0, rows, vcols] = o
            k_all = jnp.concatenate(
                [(kd_scr[hd, blk, chunk(c), :]
                  * jnp.exp(tot[nc] - tot[pos[c] + 1])).astype(BF16)
                 for c in range(nc)], axis=0)
            vt = v.astype(F32).T.astype(BF16)
            st_ref[hd] = (st * jnp.exp(tot[nc])
                          + jnp.dot(vt, k_all, preferred_element_type=F32))


def _gla_direction(proj3, lr3, w2pad, bias, reverse, o_fwd=None, norm_g=None):
    bs, seq, _ = proj3.shape
    nt = seq // GLA_TILE
    nh = GLA_STEP_HEADS

    def tile(t):
        return nt - 1 - t if reverse else t

    def spec(width, off):
        return pl.BlockSpec((1, GLA_TILE, nh * width),
                            lambda b, h, t: (b, tile(t), off // (nh * width) + h))

    in_specs = [spec(GLA_DK, OFF_GQ), spec(GLA_DK, OFF_GK), spec(GLA_DV, OFF_GV),
                pl.BlockSpec((1, GLA_TILE, LR_PAD), lambda b, h, t: (b, tile(t), 0)),
                pl.BlockSpec((LR_PAD, nh * GLA_DK), lambda b, h, t: (0, h)),
                pl.BlockSpec((1, nh * GLA_DK), lambda b, h, t: (0, h))]
    args = [proj3, proj3, proj3, lr3, w2pad, bias]
    out_spec = pl.BlockSpec((1, GLA_TILE, nh * GLA_DV), lambda b, h, t: (b, tile(t), h))
    if reverse:
        in_specs += [out_spec, spec(GLA_DV, OFF_GZ),
                     pl.BlockSpec((1, GLA_DV), lambda b, h, t: (0, 0))]
        args += [o_fwd, proj3, norm_g]
    block_scratch = (nh, GLA_TILE_BLOCKS, GLA_BLOCK, GLA_DK)
    return pl.pallas_call(
        functools.partial(_gla_kernel, reverse=reverse),
        grid=(bs, GLA_HEADS // nh, nt),
        in_specs=in_specs,
        out_specs=out_spec,
        out_shape=jax.ShapeDtypeStruct((bs, seq, GLA_VAL), BF16 if reverse else F32),
        scratch_shapes=[pltpu.VMEM((nh, GLA_DV, GLA_DK), F32),
                        pltpu.VMEM(block_scratch, F32),
                        pltpu.VMEM(block_scratch, F32),
                        pltpu.VMEM(block_scratch, BF16)],
        compiler_params=_params(("parallel", "parallel", "arbitrary")),
        name="gla_bwd" if reverse else "gla_fwd",
    )(*args)


def _merge_kernel(ya_ref, yb_ref, wa_ref, wb_ref, ga_ref, gb_ref, o_ref):
    ua = jnp.dot(ya_ref[...], wa_ref[...].astype(BF16), preferred_element_type=F32)
    ub = jnp.dot(yb_ref[...], wb_ref[...].astype(BF16), preferred_element_type=F32)
    ga = ga_ref[...].astype(F32)
    gb = gb_ref[...].astype(F32)
    merged = ua / (1.0 + jnp.exp(-ga)) + ub / (1.0 + jnp.exp(-gb))
    o_ref[...] = merged.astype(o_ref.dtype)


def _merge(ya, yb, wa, wb, gates, tm=1024, tn=1024):
    m = ya.shape[0]
    return pl.pallas_call(
        _merge_kernel,
        grid=(m // tm, D_MODEL // tn),
        in_specs=[pl.BlockSpec((tm, ATT_WIDTH), lambda i, j: (i, 0)),
                  pl.BlockSpec((tm, GLA_VAL), lambda i, j: (i, 0)),
                  pl.BlockSpec((ATT_WIDTH, tn), lambda i, j: (0, j)),
                  pl.BlockSpec((GLA_VAL, tn), lambda i, j: (0, j)),
                  pl.BlockSpec((tm, tn), lambda i, j: (i, j)),
                  pl.BlockSpec((tm, tn), lambda i, j: (i, D_MODEL // tn + j))],
        out_specs=pl.BlockSpec((tm, tn), lambda i, j: (i, j)),
        out_shape=jax.ShapeDtypeStruct((m, D_MODEL), BF16),
        compiler_params=pltpu.CompilerParams(dimension_semantics=("parallel", "arbitrary"),
                                             vmem_limit_bytes=BIG_VMEM_LIMIT),
        name="merge",
    )(ya, yb, wa, wb, gates, gates)


def _out_kernel(x_ref, a_ref, w_ref, g_ref, o_ref, ss_ref, *, nj, tn):
    j = pl.program_id(1)
    u = x_ref[...] + lax.dot_general(a_ref[...], w_ref[...], NT_DIMS,
                                     preferred_element_type=F32)
    o_ref[:, pl.ds(pl.multiple_of(j * tn, tn), tn)] = u
    part = jnp.sum(u * u, axis=-1, keepdims=True)

    @pl.when(j == 0)
    def _():
        ss_ref[...] = part

    @pl.when(j > 0)
    def _():
        ss_ref[...] += part

    @pl.when(j == nj - 1)
    def _():
        rs = lax.rsqrt(ss_ref[...] * (1.0 / D_MODEL) + NORM_EPS)
        for jj in range(nj):
            cols = slice(jj * tn, (jj + 1) * tn)
            o_ref[:, cols] = (o_ref[:, cols] * rs) * g_ref[:, cols]


def _out_proj(x, merged, wt_o, g, tm=1024, tn=512):
    m = x.shape[0]
    nj = D_MODEL // tn
    return pl.pallas_call(
        functools.partial(_out_kernel, nj=nj, tn=tn),
        grid=(m // tm, nj),
        in_specs=[pl.BlockSpec((tm, tn), lambda i, j: (i, j)),
                  pl.BlockSpec((tm, D_MODEL), lambda i, j: (i, 0),
                               pipeline_mode=pl.Buffered(1)),
                  pl.BlockSpec((tn, D_MODEL), lambda i, j: (j, 0)),
                  pl.BlockSpec((1, D_MODEL), lambda i, j: (0, 0))],
        out_specs=pl.BlockSpec((tm, D_MODEL), lambda i, j: (i, 0)),
        out_shape=jax.ShapeDtypeStruct((m, D_MODEL), F32),
        scratch_shapes=[pltpu.VMEM((tm, 1), F32)],
        compiler_params=_params(("parallel", "arbitrary")),
        name="out_proj",
    )(x, merged, wt_o, g.reshape(1, D_MODEL))


IN_PROJ_TN = 1024


def _layer(x, w, final_g):
    bs, seq, d = x.shape
    m = bs * seq
    x2 = x.reshape(m, d)
    h, h4, h16, lr3 = _rmsnorm(x, w["norm_g"], w["wt_lr"])
    group_tiles = GROUP_COLS // IN_PROJ_TN
    proj = _matmul_nt(h.reshape(m, d), w["wt_a"], N_PROJ,
                      lambda j: jnp.where(j < group_tiles, j, j + 2 * group_tiles),
                      "in_proj")
    proj3 = proj.reshape(bs, seq, N_PROJ)
    cls = []
    for g, (dilation, hc) in enumerate(((4, h4), (16, h16)), start=1):
        p = _matmul_nt(hc.reshape(m, d), w["wt_a"], GROUP_COLS,
                       lambda j, g=g: j + g * group_tiles, f"in_proj_d{dilation}")
        cls.append(p.reshape(bs, dilation, seq // dilation, GROUP_COLS))
    gates = _matmul_nt(h.reshape(m, d), w["wt_a"], 2 * D_MODEL, lambda j: j, "in_proj_gates",
                       row0=W_OFF_MG)
    ya = _attention(w["slopes"], proj3, *cls)

    o_f = _gla_direction(proj3, lr3, w["w2_f"], w["b_f"], False)
    yb = _gla_direction(proj3, lr3, w["w2_b"], w["b_b"], True, o_f, w["gla_norm_g"])

    merged = _merge(ya.reshape(m, ATT_WIDTH), yb.reshape(m, GLA_VAL),
                    w["w_up_a"], w["w_up_b"], gates)
    y = _out_proj(x2, merged, w["w_o"], final_g)
    return y.reshape(bs, seq, d)


def _pad_rows(w2, row0):
    return jnp.zeros((LR_PAD, GLA_KEY), BF16).at[row0:row0 + GLA_RANK].set(w2.astype(BF16))


def kernel(x_prompt, x_sample, norm_g, w_in, gla_w2_f, gla_b_f, gla_w2_b, gla_b_b,
           gla_norm_g, w_up_a, w_up_b, w_o, final_norm_g):
    assert norm_g.shape[0] == 1, "single-layer kernel"
    wt = w_in[0].T
    w = {
        "norm_g": norm_g[0],
        "wt_a": wt,
        "wt_lr": jnp.pad(wt[W_OFF_LR:W_OFF_MG],
                         ((0, LR_PAD - 2 * GLA_RANK), (0, 0))).astype(BF16),
        "w2_f": _pad_rows(gla_w2_f[0], 0),
        "w2_b": _pad_rows(gla_w2_b[0], GLA_RANK),
        "b_f": gla_b_f[0].reshape(1, GLA_KEY),
        "b_b": gla_b_b[0].reshape(1, GLA_KEY),
        "gla_norm_g": gla_norm_g[0].reshape(1, GLA_DV),
        "w_up_a": w_up_a[0],
        "w_up_b": w_up_b[0],
        "w_o": w_o[0].T.astype(BF16),
        "slopes": jnp.exp2(-8.0 * (jnp.arange(ATT_HEADS, dtype=F32) + 1.0) / ATT_HEADS),
    }
    return (_layer(x_prompt, w, final_norm_g), _layer(x_sample, w, final_norm_g))
```

```python
import functools

import jax
import jax.numpy as jnp
import numpy as np
from jax import lax
from jax.experimental import pallas as pl
from jax.experimental.pallas import tpu as pltpu

F32 = jnp.float32
BF16 = jnp.bfloat16

D_MODEL = 4096
ATT_DILATIONS = (1, 4, 16)
N_GROUPS = 3
ATT_HEAD_DIM = 128
ATT_HEADS = 8
ATT_WIDTH = ATT_HEADS * ATT_HEAD_DIM
ATT_RADIUS = 64
GLA_HEADS = 4
GLA_KEY = 1024
GLA_VAL = 2048
GLA_DK = GLA_KEY // GLA_HEADS
GLA_DV = GLA_VAL // GLA_HEADS
GLA_RANK = 16
GLA_TAU = 16.0
GLA_CHUNK = 64
NORM_EPS = 1e-6
NEG_INF = -1e30

GROUP_COLS = 3 * ATT_WIDTH
W_OFF_AZ = N_GROUPS * GROUP_COLS
W_OFF_LR = W_OFF_AZ + ATT_WIDTH + 2 * GLA_KEY + 2 * GLA_VAL
W_OFF_MG = W_OFF_LR + 2 * GLA_RANK
OFF_AZ = GROUP_COLS
OFF_GQ = OFF_AZ + ATT_WIDTH
OFF_GK = OFF_GQ + GLA_KEY
OFF_GV = OFF_GK + GLA_KEY
OFF_GZ = OFF_GV + GLA_VAL
N_PROJ = OFF_GZ + GLA_VAL
LR_PAD = 128

VMEM_LIMIT = 56 * 1024 * 1024
BIG_VMEM_LIMIT = 62 * 1024 * 1024


def _params(sem):
    return pltpu.CompilerParams(dimension_semantics=sem, vmem_limit_bytes=VMEM_LIMIT)


NORM_ROWS = 256
NT_DIMS = (((1,), (1,)), ((), ()))


def _class_major_permutation():
    perm = np.zeros((2 * NORM_ROWS, NORM_ROWS), np.float32)
    for block, dilation in enumerate((4, 16)):
        per_class = NORM_ROWS // dilation
        for r in range(dilation):
            for u in range(per_class):
                perm[block * NORM_ROWS + r * per_class + u, u * dilation + r] = 1.0
    return jnp.asarray(perm, BF16)


def _rmsnorm_kernel(x_ref, g_ref, wlr_ref, perm_ref, h_ref, h4_ref, h16_ref, lr_ref):
    x = x_ref[0]
    ms = jnp.mean(x * x, axis=-1, keepdims=True)
    h = ((x * lax.rsqrt(ms + NORM_EPS)) * g_ref[...]).astype(h_ref.dtype)
    h_ref[0] = h
    lr_ref[0] = lax.dot_general(h, wlr_ref[...], NT_DIMS,
                                preferred_element_type=F32).astype(lr_ref.dtype)
    hp = jnp.dot(perm_ref[...], h, preferred_element_type=F32).astype(h_ref.dtype)
    for block, (out_ref, dilation) in enumerate(((h4_ref, 4), (h16_ref, 16))):
        per_class = NORM_ROWS // dilation
        for r in range(dilation):
            row0 = block * NORM_ROWS + r * per_class
            out_ref[0, r] = hp[row0:row0 + per_class]


def _rmsnorm(x, g, wt_lr):
    bs, seq, d = x.shape
    tm = NORM_ROWS

    def cls_spec(dilation):
        return pl.BlockSpec((1, dilation, tm // dilation, d), lambda b, i: (b, 0, i, 0))

    def cls_shape(dilation):
        return jax.ShapeDtypeStruct((bs, dilation, seq // dilation, d), BF16)

    return pl.pallas_call(
        _rmsnorm_kernel,
        grid=(bs, seq // tm),
        in_specs=[pl.BlockSpec((1, tm, d), lambda b, i: (b, i, 0)),
                  pl.BlockSpec((1, d), lambda b, i: (0, 0)),
                  pl.BlockSpec((LR_PAD, d), lambda b, i: (0, 0)),
                  pl.BlockSpec((2 * tm, tm), lambda b, i: (0, 0))],
        out_specs=[pl.BlockSpec((1, tm, d), lambda b, i: (b, i, 0)),
                   cls_spec(4), cls_spec(16),
                   pl.BlockSpec((1, tm, LR_PAD), lambda b, i: (b, i, 0))],
        out_shape=[jax.ShapeDtypeStruct((bs, seq, d), BF16), cls_shape(4), cls_shape(16),
                   jax.ShapeDtypeStruct((bs, seq, LR_PAD), BF16)],
        compiler_params=_params(("parallel", "parallel")),
        name="rmsnorm",
    )(x, g.reshape(1, d), wt_lr, _class_major_permutation())


def _matmul_nt_kernel(a_ref, bt_ref, o_ref):
    o_ref[...] = lax.dot_general(a_ref[...], bt_ref[...].astype(BF16), NT_DIMS,
                                 preferred_element_type=F32).astype(o_ref.dtype)


def _matmul_nt(a, bt, n_out, row_block, name, tm=1024, tn=1024, row0=0):
    m, k = a.shape
    return pl.pallas_call(
        _matmul_nt_kernel,
        grid=(m // tm, n_out // tn),
        in_specs=[pl.BlockSpec((tm, k), lambda i, j: (i, 0)),
                  pl.BlockSpec((pl.Element(tn), pl.Element(k)),
                               lambda i, j: (pl.multiple_of(row_block(j) * tn + row0, 8), 0))],
        out_specs=pl.BlockSpec((tm, tn), lambda i, j: (i, j)),
        out_shape=jax.ShapeDtypeStruct((m, n_out), BF16),
        compiler_params=pltpu.CompilerParams(dimension_semantics=("parallel", "arbitrary"),
                                             vmem_limit_bytes=BIG_VMEM_LIMIT),
        name=name,
    )(a, bt)


ATT_TILE = 2048
ATT_QB = 128
ATT_KW = ATT_QB + 2 * ATT_RADIUS
ATT_WINDOW_OFFSETS = (-ATT_RADIUS, 0, -2 * ATT_RADIUS)


def _attn_kernel(slopes_ref, q1_ref, k1_ref, v1_ref, q2_ref, k2_ref, v2_ref,
                 q3_ref, k3_ref, v3_ref, z_ref, y_ref, o_scr, l_scr, bias_scr, *, seq):
    h = pl.program_id(1)
    tile = pl.program_id(2)
    slope = slopes_ref[h]
    scale = ATT_HEAD_DIM ** -0.5

    @pl.when(tile == 0)
    def _():
        col_minus_row = (lax.broadcasted_iota(jnp.int32, (ATT_QB, ATT_KW), 1)
                         - lax.broadcasted_iota(jnp.int32, (ATT_QB, ATT_KW), 0))
        col = lax.broadcasted_iota(jnp.int32, (ATT_QB, ATT_KW), 1)
        for g, dilation in enumerate(ATT_DILATIONS):
            if seq // dilation >= ATT_KW:
                valid = [jnp.abs(col_minus_row + off) for off in ATT_WINDOW_OFFSETS]
            else:
                valid = [jnp.where(col // ATT_QB == c, jnp.abs(col_minus_row - c * ATT_QB),
                                   ATT_KW) for c in range(2)]
            for v, dist in enumerate(valid):
                bias_scr[g * len(ATT_WINDOW_OFFSETS) + v] = jnp.where(
                    dist <= ATT_RADIUS, -(slope * float(dilation)) * dist.astype(F32),
                    NEG_INF)

    for g, (q_ref, k_ref, v_ref) in enumerate(((q1_ref, k1_ref, v1_ref),
                                               (q2_ref, k2_ref, v2_ref),
                                               (q3_ref, k3_ref, v3_ref))):
        dilation = ATT_DILATIONS[g]
        length = seq // dilation
        per_class = ATT_TILE // dilation // ATT_QB
        paired = length < ATT_KW
        for r in range(dilation):
            for j in range(per_class):
                row0 = j * ATT_QB
                if paired:
                    kv_class, variant, w0 = r // 2, r % 2, 0
                else:
                    p0 = tile * (ATT_TILE // dilation) + row0
                    w0 = pl.multiple_of(jnp.clip(p0 - ATT_RADIUS, 0, length - ATT_KW),
                                        ATT_RADIUS)
                    off = w0 - p0
                    variant = jnp.where(off == ATT_WINDOW_OFFSETS[1], 1,
                                        jnp.where(off == ATT_WINDOW_OFFSETS[2], 2, 0))
                    kv_class = r
                bias = bias_scr[g * len(ATT_WINDOW_OFFSETS) + variant]
                q = q_ref[0, r, row0:row0 + ATT_QB, :]
                s = lax.dot_general(q, k_ref[0, kv_class, pl.ds(w0, ATT_KW), :], NT_DIMS,
                                    preferred_element_type=F32) * scale + bias
                m = jnp.max(s, axis=-1, keepdims=True)
                e = jnp.exp(s - m)
                l = jnp.sum(e, axis=-1, keepdims=True)
                o = jnp.dot(e.astype(BF16), v_ref[0, kv_class, pl.ds(w0, ATT_KW), :],
                            preferred_element_type=F32) / l
                lse = jnp.broadcast_to(m + jnp.log(l), (ATT_QB, ATT_HEAD_DIM))
                if dilation == 1:
                    rows = pl.ds(row0, ATT_QB)
                else:
                    rows = pl.ds(row0 * dilation + r, ATT_QB, stride=dilation)
                o_scr[g, rows, :] = o
                l_scr[g, rows, :] = lse

    la, lb, lc = l_scr[0], l_scr[1], l_scr[2]
    m = jnp.maximum(jnp.maximum(la, lb), lc)
    ea, eb, ec = jnp.exp(la - m), jnp.exp(lb - m), jnp.exp(lc - m)
    mix = (ea * o_scr[0] + eb * o_scr[1] + ec * o_scr[2]) / (ea + eb + ec)
    z = z_ref[0].astype(F32)
    y_ref[0] = (mix * (z / (1.0 + jnp.exp(-z)))).astype(y_ref.dtype)


def _attention(slopes, proj3, cls4, cls16):
    bs, seq, _ = proj3.shape
    n_tiles = seq // ATT_TILE
    nat = proj3.reshape(bs, 1, seq, N_PROJ)

    def kv_shape(dilation):
        length = seq // dilation
        return (dilation, length) if length >= ATT_KW else (dilation // 2, 2 * length)

    def specs(dilation):
        tq = ATT_TILE // dilation
        q = pl.BlockSpec((1, dilation, tq, ATT_HEAD_DIM), lambda b, h, t: (b, 0, t, h))
        k = pl.BlockSpec((1,) + kv_shape(dilation) + (ATT_HEAD_DIM,),
                         lambda b, h, t: (b, 0, 0, ATT_HEADS + h))
        v = pl.BlockSpec((1,) + kv_shape(dilation) + (ATT_HEAD_DIM,),
                         lambda b, h, t: (b, 0, 0, 2 * ATT_HEADS + h))
        return [q, k, v]

    def kv_view(t, dilation):
        return t.reshape((bs,) + kv_shape(dilation) + (t.shape[-1],))

    tok = pl.BlockSpec((1, ATT_TILE, ATT_HEAD_DIM),
                       lambda b, h, t: (b, t, OFF_AZ // ATT_HEAD_DIM + h))
    return pl.pallas_call(
        functools.partial(_attn_kernel, seq=seq),
        grid=(bs, ATT_HEADS, n_tiles),
        in_specs=([pl.BlockSpec(memory_space=pltpu.SMEM)]
                  + specs(1) + specs(4) + specs(16) + [tok]),
        out_specs=pl.BlockSpec((1, ATT_TILE, ATT_HEAD_DIM), lambda b, h, t: (b, t, h)),
        out_shape=jax.ShapeDtypeStruct((bs, seq, ATT_WIDTH), BF16),
        scratch_shapes=[pltpu.VMEM((N_GROUPS, ATT_TILE, ATT_HEAD_DIM), F32),
                        pltpu.VMEM((N_GROUPS, ATT_TILE, ATT_HEAD_DIM), F32),
                        pltpu.VMEM((N_GROUPS * len(ATT_WINDOW_OFFSETS), ATT_QB, ATT_KW),
                                   F32)],
        compiler_params=_params(("parallel", "parallel", "arbitrary")),
        name="attention",
    )(slopes, nat, nat, nat, cls4, kv_view(cls4, 4), kv_view(cls4, 4),
      cls16, kv_view(cls16, 16), kv_view(cls16, 16), proj3)


GLA_BLOCK_CHUNKS = 4
GLA_BLOCK = GLA_BLOCK_CHUNKS * GLA_CHUNK
GLA_TILE_BLOCKS = 2
GLA_TILE = GLA_TILE_BLOCKS * GLA_BLOCK
GLA_STEP_HEADS = 4


def _gla_kernel(*refs, reverse):
    if reverse:
        (q_ref, k_ref, v_ref, z_ref, w2_ref, b2_ref, of_ref, gz_ref, g_ref,
         y_ref, st_ref, qe_scr, kd_scr, ke_scr) = refs
    else:
        (q_ref, k_ref, v_ref, z_ref, w2_ref, b2_ref,
         y_ref, st_ref, qe_scr, kd_scr, ke_scr) = refs

    @pl.when(pl.program_id(2) == 0)
    def _():
        st_ref[...] = jnp.zeros_like(st_ref)

    nc, ch = GLA_BLOCK_CHUNKS, GLA_CHUNK
    heads = range(GLA_STEP_HEADS)
    ri = lax.broadcasted_iota(jnp.int32, (GLA_BLOCK, GLA_BLOCK), 0)
    ci = lax.broadcasted_iota(jnp.int32, (GLA_BLOCK, GLA_BLOCK), 1)
    same_chunk = (ri // ch) == (ci // ch)
    if reverse:
        cum = (same_chunk & (ci >= ri)).astype(BF16)
        keep = (ci // ch > ri // ch) | (same_chunk & (ci > ri))
        end_row = 0
        order = list(range(nc - 1, -1, -1))
    else:
        cum = (same_chunk & (ci <= ri)).astype(BF16)
        keep = (ci // ch < ri // ch) | (same_chunk & (ci <= ri))
        end_row = ch - 1
        order = list(range(nc))
    pos = {c: p for p, c in enumerate(order)}

    def chunk(c):
        return slice(c * ch, (c + 1) * ch)

    blocks = range(GLA_TILE_BLOCKS)
    for blk in (reversed(blocks) if reverse else blocks):
        r0 = blk * GLA_BLOCK
        rows = slice(r0, r0 + GLA_BLOCK)
        x = jnp.dot(z_ref[0, rows, :], w2_ref[...],
                    preferred_element_type=F32) + b2_ref[...]
        la = (jnp.minimum(x, 0.0) - jnp.log(1.0 + jnp.exp(-jnp.abs(x)))) * (1.0 / GLA_TAU)
        hi = la.astype(BF16)
        lo = (la - hi.astype(F32)).astype(BF16)
        b_all = (jnp.dot(cum, hi, preferred_element_type=F32)
                 + jnp.dot(cum, lo, preferred_element_type=F32))
        total = []
        for hd in heads:
            cols = slice(hd * GLA_DK, (hd + 1) * GLA_DK)
            b = b_all[:, cols]
            ends = [b[c * ch + end_row:c * ch + end_row + 1, :] for c in range(nc)]
            tot = [jnp.zeros_like(ends[0])]
            for p in range(nc):
                tot.append(tot[-1] + ends[order[p]])
            total.append(tot)
            e_end = jnp.concatenate(
                [jnp.broadcast_to(jnp.exp(e), (ch, GLA_DK)) for e in ends], axis=0)
            ke = k_ref[0, rows, cols].astype(F32) * jnp.exp(-b)
            qe_scr[hd, blk] = (q_ref[0, rows, cols].astype(F32) * (GLA_DK ** -0.5)
                               * jnp.exp(b))
            ke_scr[hd, blk] = ke.astype(BF16)
            kd_scr[hd, blk] = ke * e_end
        att = []
        for hd in heads:
            tot = total[hd]
            att_rows = [None] * nc
            for p in range(nc):
                a = order[p]
                parts = []
                for c in range(nc):
                    if c == a:
                        parts.append(ke_scr[hd, blk, chunk(c), :])
                    elif pos[c] < p:
                        kd_c = kd_scr[hd, blk, chunk(c), :]
                        if pos[c] < p - 1:
                            kd_c = kd_c * jnp.exp(tot[p] - tot[pos[c] + 1])
                        parts.append(kd_c.astype(BF16))
                    else:
                        parts.append(jnp.zeros((ch, GLA_DK), BF16))
                att_rows[a] = lax.dot_general(
                    qe_scr[hd, blk, chunk(a), :].astype(BF16),
                    jnp.concatenate(parts, axis=0), NT_DIMS, preferred_element_type=F32)
            att.append(jnp.where(keep, jnp.concatenate(att_rows, axis=0), 0.0).astype(BF16))
        for hd in heads:
            tot = total[hd]
            vcols = slice(hd * GLA_DV, (hd + 1) * GLA_DV)
            st = st_ref[hd]
            qe_in = jnp.concatenate(
                [(qe_scr[hd, blk, chunk(c), :] * jnp.exp(tot[pos[c]])).astype(BF16)
                 for c in range(nc)], axis=0)
            v = v_ref[0, rows, vcols]
            o = (jnp.dot(att[hd], v, preferred_element_type=F32)
                 + lax.dot_general(qe_in, st.astype(BF16), NT_DIMS,
                                   preferred_element_type=F32))
            if reverse:
                o = o + of_ref[0, rows, vcols]
                ms = jnp.mean(o * o, axis=-1, keepdims=True)
                o = o * lax.rsqrt(ms + NORM_EPS) * g_ref[...]
                gz = gz_ref[0, rows, vcols].astype(F32)
                y_ref[0, rows, vcols] = (o * (gz / (1.0 + jnp.exp(-gz)))).astype(y_ref.dtype)
            else:
                y_ref[0, rows, vcols] = o
            k_all = jnp.concatenate(
                [(kd_scr[hd, blk, chunk(c), :]
                  * jnp.exp(tot[nc] - tot[pos[c] + 1])).astype(BF16)
                 for c in range(nc)], axis=0)
            vt = v.astype(F32).T.astype(BF16)
            st_ref[hd] = (st * jnp.exp(tot[nc])
                          + jnp.dot(vt, k_all, preferred_element_type=F32))


def _gla_direction(proj3, lr3, w2pad, bias, reverse, o_fwd=None, norm_g=None):
    bs, seq, _ = proj3.shape
    nt = seq // GLA_TILE
    nh = GLA_STEP_HEADS

    def tile(t):
        return nt - 1 - t if reverse else t

    def spec(width, off):
        return pl.BlockSpec((1, GLA_TILE, nh * width),
                            lambda b, h, t: (b, tile(t), off // (nh * width) + h))

    in_specs = [spec(GLA_DK, OFF_GQ), spec(GLA_DK, OFF_GK), spec(GLA_DV, OFF_GV),
                pl.BlockSpec((1, GLA_TILE, LR_PAD), lambda b, h, t: (b, tile(t), 0)),
                pl.BlockSpec((LR_PAD, nh * GLA_DK), lambda b, h, t: (0, h)),
                pl.BlockSpec((1, nh * GLA_DK), lambda b, h, t: (0, h))]
    args = [proj3, proj3, proj3, lr3, w2pad, bias]
    out_spec = pl.BlockSpec((1, GLA_TILE, nh * GLA_DV), lambda b, h, t: (b, tile(t), h))
    if reverse:
        in_specs += [out_spec, spec(GLA_DV, OFF_GZ),
                     pl.BlockSpec((1, GLA_DV), lambda b, h, t: (0, 0))]
        args += [o_fwd, proj3, norm_g]
    block_scratch = (nh, GLA_TILE_BLOCKS, GLA_BLOCK, GLA_DK)
    return pl.pallas_call(
        functools.partial(_gla_kernel, reverse=reverse),
        grid=(bs, GLA_HEADS // nh, nt),
        in_specs=in_specs,
        out_specs=out_spec,
        out_shape=jax.ShapeDtypeStruct((bs, seq, GLA_VAL), BF16 if reverse else F32),
        scratch_shapes=[pltpu.VMEM((nh, GLA_DV, GLA_DK), F32),
                        pltpu.VMEM(block_scratch, F32),
                        pltpu.VMEM(block_scratch, F32),
                        pltpu.VMEM(block_scratch, BF16)],
        compiler_params=_params(("parallel", "parallel", "arbitrary")),
        name="gla_bwd" if reverse else "gla_fwd",
    )(*args)


def _merge_kernel(ya_ref, yb_ref, wa_ref, wb_ref, ga_ref, gb_ref, o_ref):
    ua = jnp.dot(ya_ref[...], wa_ref[...].astype(BF16), preferred_element_type=F32)
    ub = jnp.dot(yb_ref[...], wb_ref[...].astype(BF16), preferred_element_type=F32)
    ga = ga_ref[...].astype(F32)
    gb = gb_ref[...].astype(F32)
    merged = ua / (1.0 + jnp.exp(-ga)) + ub / (1.0 + jnp.exp(-gb))
    o_ref[...] = merged.astype(o_ref.dtype)


def _merge(ya, yb, wa, wb, gates, tm=1024, tn=1024):
    m = ya.shape[0]
    return pl.pallas_call(
        _merge_kernel,
        grid=(m // tm, D_MODEL // tn),
        in_specs=[pl.BlockSpec((tm, ATT_WIDTH), lambda i, j: (i, 0)),
                  pl.BlockSpec((tm, GLA_VAL), lambda i, j: (i, 0)),
                  pl.BlockSpec((ATT_WIDTH, tn), lambda i, j: (0, j)),
                  pl.BlockSpec((GLA_VAL, tn), lambda i, j: (0, j)),
                  pl.BlockSpec((tm, tn), lambda i, j: (i, j)),
                  pl.BlockSpec((tm, tn), lambda i, j: (i, D_MODEL // tn + j))],
        out_specs=pl.BlockSpec((tm, tn), lambda i, j: (i, j)),
        out_shape=jax.ShapeDtypeStruct((m, D_MODEL), BF16),
        compiler_params=pltpu.CompilerParams(dimension_semantics=("parallel", "arbitrary"),
                                             vmem_limit_bytes=BIG_VMEM_LIMIT),
        name="merge",
    )(ya, yb, wa, wb, gates, gates)


def _out_kernel(x_ref, a_ref, w_ref, g_ref, o_ref, ss_ref, *, nj, tn):
    j = pl.program_id(1)
    u = x_ref[...] + jnp.dot(a_ref[...], w_ref[...], preferred_element_type=F32)
    o_ref[:, pl.ds(pl.multiple_of(j * tn, tn), tn)] = u
    part = jnp.sum(u * u, axis=-1, keepdims=True)

    @pl.when(j == 0)
    def _():
        ss_ref[...] = part

    @pl.when(j > 0)
    def _():
        ss_ref[...] += part

    @pl.when(j == nj - 1)
    def _():
        rs = lax.rsqrt(ss_ref[...] * (1.0 / D_MODEL) + NORM_EPS)
        for jj in range(nj):
            cols = slice(jj * tn, (jj + 1) * tn)
            o_ref[:, cols] = (o_ref[:, cols] * rs) * g_ref[:, cols]


def _out_proj(x, merged, w_o, g, tm=1024, tn=512):
    m = x.shape[0]
    nj = D_MODEL // tn
    return pl.pallas_call(
        functools.partial(_out_kernel, nj=nj, tn=tn),
        grid=(m // tm, nj),
        in_specs=[pl.BlockSpec((tm, tn), lambda i, j: (i, j)),
                  pl.BlockSpec((tm, D_MODEL), lambda i, j: (i, 0),
                               pipeline_mode=pl.Buffered(1)),
                  pl.BlockSpec((D_MODEL, tn), lambda i, j: (0, j)),
                  pl.BlockSpec((1, D_MODEL), lambda i, j: (0, 0))],
        out_specs=pl.BlockSpec((tm, D_MODEL), lambda i, j: (i, 0)),
        out_shape=jax.ShapeDtypeStruct((m, D_MODEL), F32),
        scratch_shapes=[pltpu.VMEM((tm, 1), F32)],
        compiler_params=_params(("parallel", "arbitrary")),
        name="out_proj",
    )(x, merged, w_o, g.reshape(1, D_MODEL))


IN_PROJ_TN = 1024


def _layer(x, w, final_g):
    bs, seq, d = x.shape
    m = bs * seq
    x2 = x.reshape(m, d)
    h, h4, h16, lr3 = _rmsnorm(x, w["norm_g"], w["wt_lr"])
    group_tiles = GROUP_COLS // IN_PROJ_TN
    proj = _matmul_nt(h.reshape(m, d), w["wt_a"], N_PROJ,
                      lambda j: jnp.where(j < group_tiles, j, j + 2 * group_tiles),
                      "in_proj")
    proj3 = proj.reshape(bs, seq, N_PROJ)
    cls = []
    for g, (dilation, hc) in enumerate(((4, h4), (16, h16)), start=1):
        p = _matmul_nt(hc.reshape(m, d), w["wt_a"], GROUP_COLS,
                       lambda j, g=g: j + g * group_tiles, f"in_proj_d{dilation}")
        cls.append(p.reshape(bs, dilation, seq // dilation, GROUP_COLS))
    gates = _matmul_nt(h.reshape(m, d), w["wt_a"], 2 * D_MODEL, lambda j: j, "in_proj_gates",
                       row0=W_OFF_MG)
    ya = _attention(w["slopes"], proj3, *cls)

    o_f = _gla_direction(proj3, lr3, w["w2_f"], w["b_f"], False)
    yb = _gla_direction(proj3, lr3, w["w2_b"], w["b_b"], True, o_f, w["gla_norm_g"])

    merged = _merge(ya.reshape(m, ATT_WIDTH), yb.reshape(m, GLA_VAL),
                    w["w_up_a"], w["w_up_b"], gates)
    y = _out_proj(x2, merged, w["w_o"], final_g)
    return y.reshape(bs, seq, d)


def _pad_rows(w2, row0):
    return jnp.zeros((LR_PAD, GLA_KEY), BF16).at[row0:row0 + GLA_RANK].set(w2.astype(BF16))


def kernel(x_prompt, x_sample, norm_g, w_in, gla_w2_f, gla_b_f, gla_w2_b, gla_b_b,
           gla_norm_g, w_up_a, w_up_b, w_o, final_norm_g):
    assert norm_g.shape[0] == 1, "single-layer kernel"
    wt = w_in[0].T
    w = {
        "norm_g": norm_g[0],
        "wt_a": wt,
        "wt_lr": jnp.pad(wt[W_OFF_LR:W_OFF_MG],
                         ((0, LR_PAD - 2 * GLA_RANK), (0, 0))).astype(BF16),
        "w2_f": _pad_rows(gla_w2_f[0], 0),
        "w2_b": _pad_rows(gla_w2_b[0], GLA_RANK),
        "b_f": gla_b_f[0].reshape(1, GLA_KEY),
        "b_b": gla_b_b[0].reshape(1, GLA_KEY),
        "gla_norm_g": gla_norm_g[0].reshape(1, GLA_DV),
        "w_up_a": w_up_a[0],
        "w_up_b": w_up_b[0],
        "w_o": w_o[0].astype(BF16),
        "slopes": jnp.exp2(-8.0 * (jnp.arange(ATT_HEADS, dtype=F32) + 1.0) / ATT_HEADS),
    }
    return (_layer(x_prompt, w, final_norm_g), _layer(x_sample, w, final_norm_g))
```

```python
import functools

import jax
import jax.numpy as jnp
import numpy as np
from jax import lax
from jax.experimental import pallas as pl
from jax.experimental.pallas import tpu as pltpu

F32 = jnp.float32
BF16 = jnp.bfloat16

D_MODEL = 4096
ATT_DILATIONS = (1, 4, 16)
N_GROUPS = 3
ATT_HEAD_DIM = 128
ATT_HEADS = 8
ATT_WIDTH = ATT_HEADS * ATT_HEAD_DIM
ATT_RADIUS = 64
GLA_HEADS = 4
GLA_KEY = 1024
GLA_VAL = 2048
GLA_DK = GLA_KEY // GLA_HEADS
GLA_DV = GLA_VAL // GLA_HEADS
GLA_RANK = 16
GLA_TAU = 16.0
GLA_CHUNK = 64
NORM_EPS = 1e-6
NEG_INF = -1e30

GROUP_COLS = 3 * ATT_WIDTH
W_OFF_AZ = N_GROUPS * GROUP_COLS
W_OFF_LR = W_OFF_AZ + ATT_WIDTH + 2 * GLA_KEY + 2 * GLA_VAL
W_OFF_MG = W_OFF_LR + 2 * GLA_RANK
OFF_AZ = GROUP_COLS
OFF_GQ = OFF_AZ + ATT_WIDTH
OFF_GK = OFF_GQ + GLA_KEY
OFF_GV = OFF_GK + GLA_KEY
OFF_GZ = OFF_GV + GLA_VAL
N_PROJ = OFF_GZ + GLA_VAL
LR_PAD = 128

VMEM_LIMIT = 56 * 1024 * 1024
BIG_VMEM_LIMIT = 62 * 1024 * 1024


def _params(sem):
    return pltpu.CompilerParams(dimension_semantics=sem, vmem_limit_bytes=VMEM_LIMIT)


NORM_ROWS = 256
NT_DIMS = (((1,), (1,)), ((), ()))


def _class_major_permutation():
    perm = np.zeros((2 * NORM_ROWS, NORM_ROWS), np.float32)
    for block, dilation in enumerate((4, 16)):
        per_class = NORM_ROWS // dilation
        for r in range(dilation):
            for u in range(per_class):
                perm[block * NORM_ROWS + r * per_class + u, u * dilation + r] = 1.0
    return jnp.asarray(perm, BF16)


def _rmsnorm_kernel(x_ref, g_ref, wlr_ref, perm_ref, h_ref, h4_ref, h16_ref, lr_ref):
    x = x_ref[0]
    ms = jnp.mean(x * x, axis=-1, keepdims=True)
    h = ((x * lax.rsqrt(ms + NORM_EPS)) * g_ref[...]).astype(h_ref.dtype)
    h_ref[0] = h
    lr_ref[0] = lax.dot_general(h, wlr_ref[...], NT_DIMS,
                                preferred_element_type=F32).astype(lr_ref.dtype)
    hp = jnp.dot(perm_ref[...], h, preferred_element_type=F32).astype(h_ref.dtype)
    for block, (out_ref, dilation) in enumerate(((h4_ref, 4), (h16_ref, 16))):
        per_class = NORM_ROWS // dilation
        for r in range(dilation):
            row0 = block * NORM_ROWS + r * per_class
            out_ref[0, r] = hp[row0:row0 + per_class]


def _rmsnorm(x, g, wt_lr):
    bs, seq, d = x.shape
    tm = NORM_ROWS

    def cls_spec(dilation):
        return pl.BlockSpec((1, dilation, tm // dilation, d), lambda b, i: (b, 0, i, 0))

    def cls_shape(dilation):
        return jax.ShapeDtypeStruct((bs, dilation, seq // dilation, d), BF16)

    return pl.pallas_call(
        _rmsnorm_kernel,
        grid=(bs, seq // tm),
        in_specs=[pl.BlockSpec((1, tm, d), lambda b, i: (b, i, 0)),
                  pl.BlockSpec((1, d), lambda b, i: (0, 0)),
                  pl.BlockSpec((LR_PAD, d), lambda b, i: (0, 0)),
                  pl.BlockSpec((2 * tm, tm), lambda b, i: (0, 0))],
        out_specs=[pl.BlockSpec((1, tm, d), lambda b, i: (b, i, 0)),
                   cls_spec(4), cls_spec(16),
                   pl.BlockSpec((1, tm, LR_PAD), lambda b, i: (b, i, 0))],
        out_shape=[jax.ShapeDtypeStruct((bs, seq, d), BF16), cls_shape(4), cls_shape(16),
                   jax.ShapeDtypeStruct((bs, seq, LR_PAD), BF16)],
        compiler_params=_params(("parallel", "parallel")),
        name="rmsnorm",
    )(x, g.reshape(1, d), wt_lr, _class_major_permutation())


def _matmul_nt_kernel(a_ref, bt_ref, o_ref):
    o_ref[...] = lax.dot_general(a_ref[...], bt_ref[...].astype(BF16), NT_DIMS,
                                 preferred_element_type=F32).astype(o_ref.dtype)


def _matmul_nt(a, bt, n_out, row_block, name, tm=1024, tn=1024, row0=0):
    m, k = a.shape
    return pl.pallas_call(
        _matmul_nt_kernel,
        grid=(n_out // tn, m // tm),
        in_specs=[pl.BlockSpec((tm, k), lambda j, i: (i, 0)),
                  pl.BlockSpec((pl.Element(tn), pl.Element(k)),
                               lambda j, i: (pl.multiple_of(row_block(j) * tn + row0, 8), 0))],
        out_specs=pl.BlockSpec((tm, tn), lambda j, i: (i, j)),
        out_shape=jax.ShapeDtypeStruct((m, n_out), BF16),
        compiler_params=pltpu.CompilerParams(dimension_semantics=("parallel", "arbitrary"),
                                             vmem_limit_bytes=BIG_VMEM_LIMIT),
        name=name,
    )(a, bt)


ATT_TILE = 2048
ATT_QB = 128
ATT_KW = ATT_QB + 2 * ATT_RADIUS
ATT_WINDOW_OFFSETS = (-ATT_RADIUS, 0, -2 * ATT_RADIUS)


def _attn_kernel(slopes_ref, q1_ref, k1_ref, v1_ref, q2_ref, k2_ref, v2_ref,
                 q3_ref, k3_ref, v3_ref, z_ref, y_ref, o_scr, l_scr, bias_scr, *, seq):
    h = pl.program_id(1)
    tile = pl.program_id(2)
    slope = slopes_ref[h]
    scale = ATT_HEAD_DIM ** -0.5

    @pl.when(tile == 0)
    def _():
        col_minus_row = (lax.broadcasted_iota(jnp.int32, (ATT_QB, ATT_KW), 1)
                         - lax.broadcasted_iota(jnp.int32, (ATT_QB, ATT_KW), 0))
        col = lax.broadcasted_iota(jnp.int32, (ATT_QB, ATT_KW), 1)
        for g, dilation in enumerate(ATT_DILATIONS):
            if seq // dilation >= ATT_KW:
                valid = [jnp.abs(col_minus_row + off) for off in ATT_WINDOW_OFFSETS]
            else:
                valid = [jnp.where(col // ATT_QB == c, jnp.abs(col_minus_row - c * ATT_QB),
                                   ATT_KW) for c in range(2)]
            for v, dist in enumerate(valid):
                bias_scr[g * len(ATT_WINDOW_OFFSETS) + v] = jnp.where(
                    dist <= ATT_RADIUS, -(slope * float(dilation)) * dist.astype(F32),
                    NEG_INF)

    for g, (q_ref, k_ref, v_ref) in enumerate(((q1_ref, k1_ref, v1_ref),
                                               (q2_ref, k2_ref, v2_ref),
                                               (q3_ref, k3_ref, v3_ref))):
        dilation = ATT_DILATIONS[g]
        length = seq // dilation
        per_class = ATT_TILE // dilation // ATT_QB
        paired = length < ATT_KW
        for r in range(dilation):
            for j in range(per_class):
                row0 = j * ATT_QB
                if paired:
                    kv_class, variant, w0 = r // 2, r % 2, 0
                else:
                    p0 = tile * (ATT_TILE // dilation) + row0
                    w0 = pl.multiple_of(jnp.clip(p0 - ATT_RADIUS, 0, length - ATT_KW),
                                        ATT_RADIUS)
                    off = w0 - p0
                    variant = jnp.where(off == ATT_WINDOW_OFFSETS[1], 1,
                                        jnp.where(off == ATT_WINDOW_OFFSETS[2], 2, 0))
                    kv_class = r
                bias = bias_scr[g * len(ATT_WINDOW_OFFSETS) + variant]
                q = q_ref[0, r, row0:row0 + ATT_QB, :]
                s = lax.dot_general(q, k_ref[0, kv_class, pl.ds(w0, ATT_KW), :], NT_DIMS,
                                    preferred_element_type=F32) * scale + bias
                m = jnp.max(s, axis=-1, keepdims=True)
                e = jnp.exp(s - m)
                l = jnp.sum(e, axis=-1, keepdims=True)
                o = jnp.dot(e.astype(BF16), v_ref[0, kv_class, pl.ds(w0, ATT_KW), :],
                            preferred_element_type=F32) / l
                lse = jnp.broadcast_to(m + jnp.log(l), (ATT_QB, ATT_HEAD_DIM))
                if dilation == 1:
                    rows = pl.ds(row0, ATT_QB)
                else:
                    rows = pl.ds(row0 * dilation + r, ATT_QB, stride=dilation)
                o_scr[g, rows, :] = o
                l_scr[g, rows, :] = lse

    la, lb, lc = l_scr[0], l_scr[1], l_scr[2]
    m = jnp.maximum(jnp.maximum(la, lb), lc)
    ea, eb, ec = jnp.exp(la - m), jnp.exp(lb - m), jnp.exp(lc - m)
    mix = (ea * o_scr[0] + eb * o_scr[1] + ec * o_scr[2]) / (ea + eb + ec)
    z = z_ref[0].astype(F32)
    y_ref[0] = (mix * (z / (1.0 + jnp.exp(-z)))).astype(y_ref.dtype)


def _attention(slopes, proj3, cls4, cls16):
    bs, seq, _ = proj3.shape
    n_tiles = seq // ATT_TILE
    nat = proj3.reshape(bs, 1, seq, N_PROJ)

    def kv_shape(dilation):
        length = seq // dilation
        return (dilation, length) if length >= ATT_KW else (dilation // 2, 2 * length)

    def specs(dilation):
        tq = ATT_TILE // dilation
        q = pl.BlockSpec((1, dilation, tq, ATT_HEAD_DIM), lambda b, h, t: (b, 0, t, h))
        k = pl.BlockSpec((1,) + kv_shape(dilation) + (ATT_HEAD_DIM,),
                         lambda b, h, t: (b, 0, 0, ATT_HEADS + h))
        v = pl.BlockSpec((1,) + kv_shape(dilation) + (ATT_HEAD_DIM,),
                         lambda b, h, t: (b, 0, 0, 2 * ATT_HEADS + h))
        return [q, k, v]

    def kv_view(t, dilation):
        return t.reshape((bs,) + kv_shape(dilation) + (t.shape[-1],))

    tok = pl.BlockSpec((1, ATT_TILE, ATT_HEAD_DIM),
                       lambda b, h, t: (b, t, OFF_AZ // ATT_HEAD_DIM + h))
    return pl.pallas_call(
        functools.partial(_attn_kernel, seq=seq),
        grid=(bs, ATT_HEADS, n_tiles),
        in_specs=([pl.BlockSpec(memory_space=pltpu.SMEM)]
                  + specs(1) + specs(4) + specs(16) + [tok]),
        out_specs=pl.BlockSpec((1, ATT_TILE, ATT_HEAD_DIM), lambda b, h, t: (b, t, h)),
        out_shape=jax.ShapeDtypeStruct((bs, seq, ATT_WIDTH), BF16),
        scratch_shapes=[pltpu.VMEM((N_GROUPS, ATT_TILE, ATT_HEAD_DIM), F32),
                        pltpu.VMEM((N_GROUPS, ATT_TILE, ATT_HEAD_DIM), F32),
                        pltpu.VMEM((N_GROUPS * len(ATT_WINDOW_OFFSETS), ATT_QB, ATT_KW),
                                   F32)],
        compiler_params=_params(("parallel", "parallel", "arbitrary")),
        name="attention",
    )(slopes, nat, nat, nat, cls4, kv_view(cls4, 4), kv_view(cls4, 4),
      cls16, kv_view(cls16, 16), kv_view(cls16, 16), proj3)


GLA_BLOCK_CHUNKS = 4
GLA_BLOCK = GLA_BLOCK_CHUNKS * GLA_CHUNK
GLA_TILE_BLOCKS = 2
GLA_TILE = GLA_TILE_BLOCKS * GLA_BLOCK
GLA_STEP_HEADS = 4


def _gla_kernel(*refs, reverse):
    if reverse:
        (q_ref, k_ref, v_ref, z_ref, w2_ref, b2_ref, of_ref, gz_ref, g_ref,
         y_ref, st_ref, qe_scr, kd_scr, ke_scr) = refs
    else:
        (q_ref, k_ref, v_ref, z_ref, w2_ref, b2_ref,
         y_ref, st_ref, qe_scr, kd_scr, ke_scr) = refs

    @pl.when(pl.program_id(2) == 0)
    def _():
        st_ref[...] = jnp.zeros_like(st_ref)

    nc, ch = GLA_BLOCK_CHUNKS, GLA_CHUNK
    heads = range(GLA_STEP_HEADS)
    ri = lax.broadcasted_iota(jnp.int32, (GLA_BLOCK, GLA_BLOCK), 0)
    ci = lax.broadcasted_iota(jnp.int32, (GLA_BLOCK, GLA_BLOCK), 1)
    same_chunk = (ri // ch) == (ci // ch)
    if reverse:
        cum = (same_chunk & (ci >= ri)).astype(BF16)
        keep = (ci // ch > ri // ch) | (same_chunk & (ci > ri))
        end_row = 0
        order = list(range(nc - 1, -1, -1))
    else:
        cum = (same_chunk & (ci <= ri)).astype(BF16)
        keep = (ci // ch < ri // ch) | (same_chunk & (ci <= ri))
        end_row = ch - 1
        order = list(range(nc))
    pos = {c: p for p, c in enumerate(order)}

    def chunk(c):
        return slice(c * ch, (c + 1) * ch)

    blocks = range(GLA_TILE_BLOCKS)
    for blk in (reversed(blocks) if reverse else blocks):
        r0 = blk * GLA_BLOCK
        rows = slice(r0, r0 + GLA_BLOCK)
        x = jnp.dot(z_ref[0, rows, :], w2_ref[...],
                    preferred_element_type=F32) + b2_ref[...]
        la = (jnp.minimum(x, 0.0) - jnp.log(1.0 + jnp.exp(-jnp.abs(x)))) * (1.0 / GLA_TAU)
        hi = la.astype(BF16)
        lo = (la - hi.astype(F32)).astype(BF16)
        b_all = (jnp.dot(cum, hi, preferred_element_type=F32)
                 + jnp.dot(cum, lo, preferred_element_type=F32))
        total = []
        for hd in heads:
            cols = slice(hd * GLA_DK, (hd + 1) * GLA_DK)
            b = b_all[:, cols]
            ends = [b[c * ch + end_row:c * ch + end_row + 1, :] for c in range(nc)]
            tot = [jnp.zeros_like(ends[0])]
            for p in range(nc):
                tot.append(tot[-1] + ends[order[p]])
            total.append(tot)
            e_end = jnp.concatenate(
                [jnp.broadcast_to(jnp.exp(e), (ch, GLA_DK)) for e in ends], axis=0)
            ke = k_ref[0, rows, cols].astype(F32) * jnp.exp(-b)
            qe_scr[hd, blk] = (q_ref[0, rows, cols].astype(F32) * (GLA_DK ** -0.5)
                               * jnp.exp(b))
            ke_scr[hd, blk] = ke.astype(BF16)
            kd_scr[hd, blk] = ke * e_end
        att = []
        for hd in heads:
            tot = total[hd]
            att_rows = [None] * nc
            for p in range(nc):
                a = order[p]
                parts = []
                for c in range(nc):
                    if c == a:
                        parts.append(ke_scr[hd, blk, chunk(c), :])
                    elif pos[c] < p:
                        kd_c = kd_scr[hd, blk, chunk(c), :]
                        if pos[c] < p - 1:
                            kd_c = kd_c * jnp.exp(tot[p] - tot[pos[c] + 1])
                        parts.append(kd_c.astype(BF16))
                    else:
                        parts.append(jnp.zeros((ch, GLA_DK), BF16))
                att_rows[a] = lax.dot_general(
                    qe_scr[hd, blk, chunk(a), :].astype(BF16),
                    jnp.concatenate(parts, axis=0), NT_DIMS, preferred_element_type=F32)
            att.append(jnp.where(keep, jnp.concatenate(att_rows, axis=0), 0.0).astype(BF16))
        for hd in heads:
            tot = total[hd]
            vcols = slice(hd * GLA_DV, (hd + 1) * GLA_DV)
            st = st_ref[hd]
            qe_in = jnp.concatenate(
                [(qe_scr[hd, blk, chunk(c), :] * jnp.exp(tot[pos[c]])).astype(BF16)
                 for c in range(nc)], axis=0)
            v = v_ref[0, rows, vcols]
            o = (jnp.dot(att[hd], v, preferred_element_type=F32)
                 + lax.dot_general(qe_in, st.astype(BF16), NT_DIMS,
                                   preferred_element_type=F32))
            if reverse:
                o = o + of_ref[0, rows, vcols]
                ms = jnp.mean(o * o, axis=-1, keepdims=True)
                o = o * lax.rsqrt(ms + NORM_EPS) * g_ref[...]
                gz = gz_ref[0, rows, vcols].astype(F32)
                y_ref[0, rows, vcols] = (o * (gz / (1.0 + jnp.exp(-gz)))).astype(y_ref.dtype)
            else:
                y_ref[0, rows, vcols] = o
            k_all = jnp.concatenate(
                [(kd_scr[hd, blk, chunk(c), :]
                  * jnp.exp(tot[nc] - tot[pos[c] + 1])).astype(BF16)
                 for c in range(nc)], axis=0)
            vt = v.astype(F32).T.astype(BF16)
            st_ref[hd] = (st * jnp.exp(tot[nc])
                          + jnp.dot(vt, k_all, preferred_element_type=F32))


def _gla_direction(proj3, lr3, w2pad, bias, reverse, o_fwd=None, norm_g=None):
    bs, seq, _ = proj3.shape
    nt = seq // GLA_TILE
    nh = GLA_STEP_HEADS

    def tile(t):
        return nt - 1 - t if reverse else t

    def spec(width, off):
        return pl.BlockSpec((1, GLA_TILE, nh * width),
                            lambda b, h, t: (b, tile(t), off // (nh * width) + h))

    in_specs = [spec(GLA_DK, OFF_GQ), spec(GLA_DK, OFF_GK), spec(GLA_DV, OFF_GV),
                pl.BlockSpec((1, GLA_TILE, LR_PAD), lambda b, h, t: (b, tile(t), 0)),
                pl.BlockSpec((LR_PAD, nh * GLA_DK), lambda b, h, t: (0, h)),
                pl.BlockSpec((1, nh * GLA_DK), lambda b, h, t: (0, h))]
    args = [proj3, proj3, proj3, lr3, w2pad, bias]
    out_spec = pl.BlockSpec((1, GLA_TILE, nh * GLA_DV), lambda b, h, t: (b, tile(t), h))
    if reverse:
        in_specs += [out_spec, spec(GLA_DV, OFF_GZ),
                     pl.BlockSpec((1, GLA_DV), lambda b, h, t: (0, 0))]
        args += [o_fwd, proj3, norm_g]
    block_scratch = (nh, GLA_TILE_BLOCKS, GLA_BLOCK, GLA_DK)
    return pl.pallas_call(
        functools.partial(_gla_kernel, reverse=reverse),
        grid=(bs, GLA_HEADS // nh, nt),
        in_specs=in_specs,
        out_specs=out_spec,
        out_shape=jax.ShapeDtypeStruct((bs, seq, GLA_VAL), BF16 if reverse else F32),
        scratch_shapes=[pltpu.VMEM((nh, GLA_DV, GLA_DK), F32),
                        pltpu.VMEM(block_scratch, F32),
                        pltpu.VMEM(block_scratch, F32),
                        pltpu.VMEM(block_scratch, BF16)],
        compiler_params=_params(("parallel", "parallel", "arbitrary")),
        name="gla_bwd" if reverse else "gla_fwd",
    )(*args)


def _merge_kernel(ya_ref, yb_ref, wa_ref, wb_ref, ga_ref, gb_ref, o_ref):
    ua = jnp.dot(ya_ref[...], wa_ref[...].astype(BF16), preferred_element_type=F32)
    ub = jnp.dot(yb_ref[...], wb_ref[...].astype(BF16), preferred_element_type=F32)
    ga = ga_ref[...].astype(F32)
    gb = gb_ref[...].astype(F32)
    merged = ua / (1.0 + jnp.exp(-ga)) + ub / (1.0 + jnp.exp(-gb))
    o_ref[...] = merged.astype(o_ref.dtype)


def _merge(ya, yb, wa, wb, gates, tm=1024, tn=1024):
    m = ya.shape[0]
    return pl.pallas_call(
        _merge_kernel,
        grid=(m // tm, D_MODEL // tn),
        in_specs=[pl.BlockSpec((tm, ATT_WIDTH), lambda i, j: (i, 0)),
                  pl.BlockSpec((tm, GLA_VAL), lambda i, j: (i, 0)),
                  pl.BlockSpec((ATT_WIDTH, tn), lambda i, j: (0, j)),
                  pl.BlockSpec((GLA_VAL, tn), lambda i, j: (0, j)),
                  pl.BlockSpec((tm, tn), lambda i, j: (i, j)),
                  pl.BlockSpec((tm, tn), lambda i, j: (i, D_MODEL // tn + j))],
        out_specs=pl.BlockSpec((tm, tn), lambda i, j: (i, j)),
        out_shape=jax.ShapeDtypeStruct((m, D_MODEL), BF16),
        compiler_params=pltpu.CompilerParams(dimension_semantics=("parallel", "arbitrary"),
                                             vmem_limit_bytes=BIG_VMEM_LIMIT),
        name="merge",
    )(ya, yb, wa, wb, gates, gates)


def _out_kernel(x_ref, a_ref, w_ref, g_ref, o_ref):
    u = x_ref[...] + jnp.dot(a_ref[...], w_ref[...], preferred_element_type=F32)
    ms = jnp.mean(u * u, axis=-1, keepdims=True)
    o_ref[...] = (u * lax.rsqrt(ms + NORM_EPS)) * g_ref[...]


def _out_proj(x, merged, w_o, g, tm=256):
    m = x.shape[0]
    return pl.pallas_call(
        _out_kernel,
        grid=(m // tm,),
        in_specs=[pl.BlockSpec((tm, D_MODEL), lambda i: (i, 0)),
                  pl.BlockSpec((tm, D_MODEL), lambda i: (i, 0)),
                  pl.BlockSpec((D_MODEL, D_MODEL), lambda i: (0, 0),
                               pipeline_mode=pl.Buffered(1)),
                  pl.BlockSpec((1, D_MODEL), lambda i: (0, 0))],
        out_specs=pl.BlockSpec((tm, D_MODEL), lambda i: (i, 0)),
        out_shape=jax.ShapeDtypeStruct((m, D_MODEL), F32),
        compiler_params=pltpu.CompilerParams(dimension_semantics=("parallel",),
                                             vmem_limit_bytes=BIG_VMEM_LIMIT),
        name="out_proj",
    )(x, merged, w_o, g.reshape(1, D_MODEL))


IN_PROJ_TN = 1024


def _layer(x, w, final_g):
    bs, seq, d = x.shape
    m = bs * seq
    x2 = x.reshape(m, d)
    h, h4, h16, lr3 = _rmsnorm(x, w["norm_g"], w["wt_lr"])
    group_tiles = GROUP_COLS // IN_PROJ_TN
    proj = _matmul_nt(h.reshape(m, d), w["wt_a"], N_PROJ,
                      lambda j: jnp.where(j < group_tiles, j, j + 2 * group_tiles),
                      "in_proj")
    proj3 = proj.reshape(bs, seq, N_PROJ)
    cls = []
    for g, (dilation, hc) in enumerate(((4, h4), (16, h16)), start=1):
        p = _matmul_nt(hc.reshape(m, d), w["wt_a"], GROUP_COLS,
                       lambda j, g=g: j + g * group_tiles, f"in_proj_d{dilation}")
        cls.append(p.reshape(bs, dilation, seq // dilation, GROUP_COLS))
    gates = _matmul_nt(h.reshape(m, d), w["wt_a"], 2 * D_MODEL, lambda j: j, "in_proj_gates",
                       row0=W_OFF_MG)
    ya = _attention(w["slopes"], proj3, *cls)

    o_f = _gla_direction(proj3, lr3, w["w2_f"], w["b_f"], False)
    yb = _gla_direction(proj3, lr3, w["w2_b"], w["b_b"], True, o_f, w["gla_norm_g"])

    merged = _merge(ya.reshape(m, ATT_WIDTH), yb.reshape(m, GLA_VAL),
                    w["w_up_a"], w["w_up_b"], gates)
    y = _out_proj(x2, merged, w["w_o"], final_g)
    return y.reshape(bs, seq, d)


def _pad_rows(w2, row0):
    return jnp.zeros((LR_PAD, GLA_KEY), BF16).at[row0:row0 + GLA_RANK].set(w2.astype(BF16))


def kernel(x_prompt, x_sample, norm_g, w_in, gla_w2_f, gla_b_f, gla_w2_b, gla_b_b,
           gla_norm_g, w_up_a, w_up_b, w_o, final_norm_g):
    assert norm_g.shape[0] == 1, "single-layer kernel"
    wt = w_in[0].T
    w = {
        "norm_g": norm_g[0],
        "wt_a": wt,
        "wt_lr": jnp.pad(wt[W_OFF_LR:W_OFF_MG],
                         ((0, LR_PAD - 2 * GLA_RANK), (0, 0))).astype(BF16),
        "w2_f": _pad_rows(gla_w2_f[0], 0),
        "w2_b": _pad_rows(gla_w2_b[0], GLA_RANK),
        "b_f": gla_b_f[0].reshape(1, GLA_KEY),
        "b_b": gla_b_b[0].reshape(1, GLA_KEY),
        "gla_norm_g": gla_norm_g[0].reshape(1, GLA_DV),
        "w_up_a": w_up_a[0],
        "w_up_b": w_up_b[0],
        "w_o": w_o[0].astype(BF16),
        "slopes": jnp.exp2(-8.0 * (jnp.arange(ATT_HEADS, dtype=F32) + 1.0) / ATT_HEADS),
    }
    return (_layer(x_prompt, w, final_norm_g), _layer(x_sample, w, final_norm_g))
```

```python
import functools

import jax
import jax.numpy as jnp
import numpy as np
from jax import lax
from jax.experimental import pallas as pl
from jax.experimental.pallas import tpu as pltpu

F32 = jnp.float32
BF16 = jnp.bfloat16

D_MODEL = 4096
ATT_DILATIONS = (1, 4, 16)
N_GROUPS = 3
ATT_HEAD_DIM = 128
ATT_HEADS = 8
ATT_WIDTH = ATT_HEADS * ATT_HEAD_DIM
ATT_RADIUS = 64
GLA_HEADS = 4
GLA_KEY = 1024
GLA_VAL = 2048
GLA_DK = GLA_KEY // GLA_HEADS
GLA_DV = GLA_VAL // GLA_HEADS
GLA_RANK = 16
GLA_TAU = 16.0
GLA_CHUNK = 64
NORM_EPS = 1e-6
NEG_INF = -1e30

GROUP_COLS = 3 * ATT_WIDTH
W_OFF_AZ = N_GROUPS * GROUP_COLS
W_OFF_LR = W_OFF_AZ + ATT_WIDTH + 2 * GLA_KEY + 2 * GLA_VAL
W_OFF_MG = W_OFF_LR + 2 * GLA_RANK
OFF_AZ = GROUP_COLS
OFF_GQ = OFF_AZ + ATT_WIDTH
OFF_GK = OFF_GQ + GLA_KEY
OFF_GV = OFF_GK + GLA_KEY
OFF_GZ = OFF_GV + GLA_VAL
N_PROJ = OFF_GZ + GLA_VAL
LR_PAD = 128

VMEM_LIMIT = 56 * 1024 * 1024
BIG_VMEM_LIMIT = 62 * 1024 * 1024


def _params(sem):
    return pltpu.CompilerParams(dimension_semantics=sem, vmem_limit_bytes=VMEM_LIMIT)


NORM_ROWS = 256
NT_DIMS = (((1,), (1,)), ((), ()))


def _class_major_permutation():
    perm = np.zeros((2 * NORM_ROWS, NORM_ROWS), np.float32)
    for block, dilation in enumerate((4, 16)):
        per_class = NORM_ROWS // dilation
        for r in range(dilation):
            for u in range(per_class):
                perm[block * NORM_ROWS + r * per_class + u, u * dilation + r] = 1.0
    return jnp.asarray(perm, BF16)


def _rmsnorm_kernel(x_ref, g_ref, wlr_ref, perm_ref, h_ref, h4_ref, h16_ref, lr_ref):
    x = x_ref[0]
    ms = jnp.mean(x * x, axis=-1, keepdims=True)
    h = ((x * lax.rsqrt(ms + NORM_EPS)) * g_ref[...]).astype(h_ref.dtype)
    h_ref[0] = h
    lr_ref[0] = lax.dot_general(h, wlr_ref[...], NT_DIMS,
                                preferred_element_type=F32).astype(lr_ref.dtype)
    hp = jnp.dot(perm_ref[...], h, preferred_element_type=F32).astype(h_ref.dtype)
    for block, (out_ref, dilation) in enumerate(((h4_ref, 4), (h16_ref, 16))):
        per_class = NORM_ROWS // dilation
        for r in range(dilation):
            row0 = block * NORM_ROWS + r * per_class
            out_ref[0, r] = hp[row0:row0 + per_class]


def _rmsnorm(x, g, wt_lr):
    bs, seq, d = x.shape
    tm = NORM_ROWS

    def cls_spec(dilation):
        return pl.BlockSpec((1, dilation, tm // dilation, d), lambda b, i: (b, 0, i, 0))

    def cls_shape(dilation):
        return jax.ShapeDtypeStruct((bs, dilation, seq // dilation, d), BF16)

    return pl.pallas_call(
        _rmsnorm_kernel,
        grid=(bs, seq // tm),
        in_specs=[pl.BlockSpec((1, tm, d), lambda b, i: (b, i, 0)),
                  pl.BlockSpec((1, d), lambda b, i: (0, 0)),
                  pl.BlockSpec((LR_PAD, d), lambda b, i: (0, 0)),
                  pl.BlockSpec((2 * tm, tm), lambda b, i: (0, 0))],
        out_specs=[pl.BlockSpec((1, tm, d), lambda b, i: (b, i, 0)),
                   cls_spec(4), cls_spec(16),
                   pl.BlockSpec((1, tm, LR_PAD), lambda b, i: (b, i, 0))],
        out_shape=[jax.ShapeDtypeStruct((bs, seq, d), BF16), cls_shape(4), cls_shape(16),
                   jax.ShapeDtypeStruct((bs, seq, LR_PAD), BF16)],
        compiler_params=_params(("parallel", "parallel")),
        name="rmsnorm",
    )(x, g.reshape(1, d), wt_lr, _class_major_permutation())


def _matmul_nt_kernel(a_ref, bt_ref, o_ref):
    o_ref[...] = lax.dot_general(a_ref[...], bt_ref[...].astype(BF16), NT_DIMS,
                                 preferred_element_type=F32).astype(o_ref.dtype)


def _matmul_nt(a, bt, n_out, row_block, name, tm=1024, tn=1024, row0=0):
    m, k = a.shape
    return pl.pallas_call(
        _matmul_nt_kernel,
        grid=(n_out // tn, m // tm),
        in_specs=[pl.BlockSpec((tm, k), lambda j, i: (i, 0)),
                  pl.BlockSpec((pl.Element(tn), pl.Element(k)),
                               lambda j, i: (pl.multiple_of(row_block(j) * tn + row0, 8), 0))],
        out_specs=pl.BlockSpec((tm, tn), lambda j, i: (i, j)),
        out_shape=jax.ShapeDtypeStruct((m, n_out), BF16),
        compiler_params=pltpu.CompilerParams(dimension_semantics=("parallel", "arbitrary"),
                                             vmem_limit_bytes=BIG_VMEM_LIMIT),
        name=name,
    )(a, bt)


ATT_TILE = 2048
ATT_QB = 128
ATT_KW = ATT_QB + 2 * ATT_RADIUS
ATT_WINDOW_OFFSETS = (-ATT_RADIUS, 0, -2 * ATT_RADIUS)


def _attn_kernel(slopes_ref, q1_ref, k1_ref, v1_ref, q2_ref, k2_ref, v2_ref,
                 q3_ref, k3_ref, v3_ref, z_ref, y_ref, o_scr, l_scr, bias_scr, *, seq):
    h = pl.program_id(1)
    tile = pl.program_id(2)
    slope = slopes_ref[h]
    scale = ATT_HEAD_DIM ** -0.5

    @pl.when(tile == 0)
    def _():
        col_minus_row = (lax.broadcasted_iota(jnp.int32, (ATT_QB, ATT_KW), 1)
                         - lax.broadcasted_iota(jnp.int32, (ATT_QB, ATT_KW), 0))
        col = lax.broadcasted_iota(jnp.int32, (ATT_QB, ATT_KW), 1)
        for g, dilation in enumerate(ATT_DILATIONS):
            if seq // dilation >= ATT_KW:
                valid = [jnp.abs(col_minus_row + off) for off in ATT_WINDOW_OFFSETS]
            else:
                valid = [jnp.where(col // ATT_QB == c, jnp.abs(col_minus_row - c * ATT_QB),
                                   ATT_KW) for c in range(2)]
            for v, dist in enumerate(valid):
                bias_scr[g * len(ATT_WINDOW_OFFSETS) + v] = jnp.where(
                    dist <= ATT_RADIUS, -(slope * float(dilation)) * dist.astype(F32),
                    NEG_INF)

    for g, (q_ref, k_ref, v_ref) in enumerate(((q1_ref, k1_ref, v1_ref),
                                               (q2_ref, k2_ref, v2_ref),
                                               (q3_ref, k3_ref, v3_ref))):
        dilation = ATT_DILATIONS[g]
        length = seq // dilation
        per_class = ATT_TILE // dilation // ATT_QB
        paired = length < ATT_KW
        for r in range(dilation):
            for j in range(per_class):
                row0 = j * ATT_QB
                if paired:
                    kv_class, variant, w0 = r // 2, r % 2, 0
                else:
                    p0 = tile * (ATT_TILE // dilation) + row0
                    w0 = pl.multiple_of(jnp.clip(p0 - ATT_RADIUS, 0, length - ATT_KW),
                                        ATT_RADIUS)
                    off = w0 - p0
                    variant = jnp.where(off == ATT_WINDOW_OFFSETS[1], 1,
                                        jnp.where(off == ATT_WINDOW_OFFSETS[2], 2, 0))
                    kv_class = r
                bias = bias_scr[g * len(ATT_WINDOW_OFFSETS) + variant]
                q = q_ref[0, r, row0:row0 + ATT_QB, :]
                s = lax.dot_general(q, k_ref[0, kv_class, pl.ds(w0, ATT_KW), :], NT_DIMS,
                                    preferred_element_type=F32) * scale + bias
                m = jnp.max(s, axis=-1, keepdims=True)
                e = jnp.exp(s - m)
                l = jnp.sum(e, axis=-1, keepdims=True)
                o = jnp.dot(e.astype(BF16), v_ref[0, kv_class, pl.ds(w0, ATT_KW), :],
                            preferred_element_type=F32) / l
                lse = jnp.broadcast_to(m + jnp.log(l), (ATT_QB, ATT_HEAD_DIM))
                if dilation == 1:
                    rows = pl.ds(row0, ATT_QB)
                else:
                    rows = pl.ds(row0 * dilation + r, ATT_QB, stride=dilation)
                o_scr[g, rows, :] = o
                l_scr[g, rows, :] = lse

    la, lb, lc = l_scr[0], l_scr[1], l_scr[2]
    m = jnp.maximum(jnp.maximum(la, lb), lc)
    ea, eb, ec = jnp.exp(la - m), jnp.exp(lb - m), jnp.exp(lc - m)
    mix = (ea * o_scr[0] + eb * o_scr[1] + ec * o_scr[2]) / (ea + eb + ec)
    z = z_ref[0].astype(F32)
    y_ref[0] = (mix * (z / (1.0 + jnp.exp(-z)))).astype(y_ref.dtype)


def _attention(slopes, proj3, cls4, cls16):
    bs, seq, _ = proj3.shape
    n_tiles = seq // ATT_TILE
    nat = proj3.reshape(bs, 1, seq, N_PROJ)

    def kv_shape(dilation):
        length = seq // dilation
        return (dilation, length) if length >= ATT_KW else (dilation // 2, 2 * length)

    def specs(dilation):
        tq = ATT_TILE // dilation
        q = pl.BlockSpec((1, dilation, tq, ATT_HEAD_DIM), lambda b, h, t: (b, 0, t, h))
        k = pl.BlockSpec((1,) + kv_shape(dilation) + (ATT_HEAD_DIM,),
                         lambda b, h, t: (b, 0, 0, ATT_HEADS + h))
        v = pl.BlockSpec((1,) + kv_shape(dilation) + (ATT_HEAD_DIM,),
                         lambda b, h, t: (b, 0, 0, 2 * ATT_HEADS + h))
        return [q, k, v]

    def kv_view(t, dilation):
        return t.reshape((bs,) + kv_shape(dilation) + (t.shape[-1],))

    tok = pl.BlockSpec((1, ATT_TILE, ATT_HEAD_DIM),
                       lambda b, h, t: (b, t, OFF_AZ // ATT_HEAD_DIM + h))
    return pl.pallas_call(
        functools.partial(_attn_kernel, seq=seq),
        grid=(bs, ATT_HEADS, n_tiles),
        in_specs=([pl.BlockSpec(memory_space=pltpu.SMEM)]
                  + specs(1) + specs(4) + specs(16) + [tok]),
        out_specs=pl.BlockSpec((1, ATT_TILE, ATT_HEAD_DIM), lambda b, h, t: (b, t, h)),
        out_shape=jax.ShapeDtypeStruct((bs, seq, ATT_WIDTH), BF16),
        scratch_shapes=[pltpu.VMEM((N_GROUPS, ATT_TILE, ATT_HEAD_DIM), F32),
                        pltpu.VMEM((N_GROUPS, ATT_TILE, ATT_HEAD_DIM), F32),
                        pltpu.VMEM((N_GROUPS * len(ATT_WINDOW_OFFSETS), ATT_QB, ATT_KW),
                                   F32)],
        compiler_params=_params(("parallel", "parallel", "arbitrary")),
        name="attention",
    )(slopes, nat, nat, nat, cls4, kv_view(cls4, 4), kv_view(cls4, 4),
      cls16, kv_view(cls16, 16), kv_view(cls16, 16), proj3)


GLA_BLOCK_CHUNKS = 4
GLA_BLOCK = GLA_BLOCK_CHUNKS * GLA_CHUNK
GLA_TILE_BLOCKS = 2
GLA_TILE = GLA_TILE_BLOCKS * GLA_BLOCK
GLA_STEP_HEADS = 4


def _gla_kernel(*refs, reverse):
    if reverse:
        (q_ref, k_ref, v_ref, z_ref, w2_ref, b2_ref, of_ref, gz_ref, g_ref,
         y_ref, st_ref, qe_scr, kd_scr, ke_scr) = refs
    else:
        (q_ref, k_ref, v_ref, z_ref, w2_ref, b2_ref,
         y_ref, st_ref, qe_scr, kd_scr, ke_scr) = refs

    @pl.when(pl.program_id(2) == 0)
    def _():
        st_ref[...] = jnp.zeros_like(st_ref)

    nc, ch = GLA_BLOCK_CHUNKS, GLA_CHUNK
    heads = range(GLA_STEP_HEADS)
    ri = lax.broadcasted_iota(jnp.int32, (GLA_BLOCK, GLA_BLOCK), 0)
    ci = lax.broadcasted_iota(jnp.int32, (GLA_BLOCK, GLA_BLOCK), 1)
    same_chunk = (ri // ch) == (ci // ch)
    if reverse:
        cum = (same_chunk & (ci >= ri)).astype(BF16)
        keep = (ci // ch > ri // ch) | (same_chunk & (ci > ri))
        end_row = 0
        order = list(range(nc - 1, -1, -1))
    else:
        cum = (same_chunk & (ci <= ri)).astype(BF16)
        keep = (ci // ch < ri // ch) | (same_chunk & (ci <= ri))
        end_row = ch - 1
        order = list(range(nc))
    pos = {c: p for p, c in enumerate(order)}

    def chunk(c):
        return slice(c * ch, (c + 1) * ch)

    blocks = range(GLA_TILE_BLOCKS)
    for blk in (reversed(blocks) if reverse else blocks):
        r0 = blk * GLA_BLOCK
        rows = slice(r0, r0 + GLA_BLOCK)
        x = jnp.dot(z_ref[0, rows, :], w2_ref[...],
                    preferred_element_type=F32) + b2_ref[...]
        la = (jnp.minimum(x, 0.0) - jnp.log(1.0 + jnp.exp(-jnp.abs(x)))) * (1.0 / GLA_TAU)
        hi = la.astype(BF16)
        lo = (la - hi.astype(F32)).astype(BF16)
        b_all = (jnp.dot(cum, hi, preferred_element_type=F32)
                 + jnp.dot(cum, lo, preferred_element_type=F32))
        total = []
        for hd in heads:
            cols = slice(hd * GLA_DK, (hd + 1) * GLA_DK)
            b = b_all[:, cols]
            ends = [b[c * ch + end_row:c * ch + end_row + 1, :] for c in range(nc)]
            tot = [jnp.zeros_like(ends[0])]
            for p in range(nc):
                tot.append(tot[-1] + ends[order[p]])
            total.append(tot)
            e_end = jnp.concatenate(
                [jnp.broadcast_to(jnp.exp(e), (ch, GLA_DK)) for e in ends], axis=0)
            ke = k_ref[0, rows, cols].astype(F32) * jnp.exp(-b)
            qe_scr[hd, blk] = (q_ref[0, rows, cols].astype(F32) * (GLA_DK ** -0.5)
                               * jnp.exp(b))
            ke_scr[hd, blk] = ke.astype(BF16)
            kd_scr[hd, blk] = ke * e_end
        att = []
        for hd in heads:
            tot = total[hd]
            att_rows = [None] * nc
            for p in range(nc):
                a = order[p]
                parts = []
                for c in range(nc):
                    if c == a:
                        parts.append(ke_scr[hd, blk, chunk(c), :])
                    elif pos[c] < p:
                        kd_c = kd_scr[hd, blk, chunk(c), :]
                        if pos[c] < p - 1:
                            kd_c = kd_c * jnp.exp(tot[p] - tot[pos[c] + 1])
                        parts.append(kd_c.astype(BF16))
                    else:
                        parts.append(jnp.zeros((ch, GLA_DK), BF16))
                att_rows[a] = lax.dot_general(
                    qe_scr[hd, blk, chunk(a), :].astype(BF16),
                    jnp.concatenate(parts, axis=0), NT_DIMS, preferred_element_type=F32)
            att.append(jnp.where(keep, jnp.concatenate(att_rows, axis=0), 0.0).astype(BF16))
        for hd in heads:
            tot = total[hd]
            vcols = slice(hd * GLA_DV, (hd + 1) * GLA_DV)
            st = st_ref[hd]
            qe_in = jnp.concatenate(
                [(qe_scr[hd, blk, chunk(c), :] * jnp.exp(tot[pos[c]])).astype(BF16)
                 for c in range(nc)], axis=0)
            v = v_ref[0, rows, vcols]
            o = (jnp.dot(att[hd], v, preferred_element_type=F32)
                 + lax.dot_general(qe_in, st.astype(BF16), NT_DIMS,
                                   preferred_element_type=F32))
            if reverse:
                o = o + of_ref[0, rows, vcols].astype(F32)
                ms = jnp.mean(o * o, axis=-1, keepdims=True)
                o = o * lax.rsqrt(ms + NORM_EPS) * g_ref[...]
                gz = gz_ref[0, rows, vcols].astype(F32)
                y_ref[0, rows, vcols] = (o * (gz / (1.0 + jnp.exp(-gz)))).astype(y_ref.dtype)
            else:
                y_ref[0, rows, vcols] = o.astype(y_ref.dtype)
            k_all = jnp.concatenate(
                [(kd_scr[hd, blk, chunk(c), :]
                  * jnp.exp(tot[nc] - tot[pos[c] + 1])).astype(BF16)
                 for c in range(nc)], axis=0)
            vt = v.astype(F32).T.astype(BF16)
            st_ref[hd] = (st * jnp.exp(tot[nc])
                          + jnp.dot(vt, k_all, preferred_element_type=F32))


def _gla_direction(proj3, lr3, w2pad, bias, reverse, o_fwd=None, norm_g=None):
    bs, seq, _ = proj3.shape
    nt = seq // GLA_TILE
    nh = GLA_STEP_HEADS

    def tile(t):
        return nt - 1 - t if reverse else t

    def spec(width, off):
        return pl.BlockSpec((1, GLA_TILE, nh * width),
                            lambda b, h, t: (b, tile(t), off // (nh * width) + h))

    in_specs = [spec(GLA_DK, OFF_GQ), spec(GLA_DK, OFF_GK), spec(GLA_DV, OFF_GV),
                pl.BlockSpec((1, GLA_TILE, LR_PAD), lambda b, h, t: (b, tile(t), 0)),
                pl.BlockSpec((LR_PAD, nh * GLA_DK), lambda b, h, t: (0, h)),
                pl.BlockSpec((1, nh * GLA_DK), lambda b, h, t: (0, h))]
    args = [proj3, proj3, proj3, lr3, w2pad, bias]
    out_spec = pl.BlockSpec((1, GLA_TILE, nh * GLA_DV), lambda b, h, t: (b, tile(t), h))
    if reverse:
        in_specs += [out_spec, spec(GLA_DV, OFF_GZ),
                     pl.BlockSpec((1, GLA_DV), lambda b, h, t: (0, 0))]
        args += [o_fwd, proj3, norm_g]
    block_scratch = (nh, GLA_TILE_BLOCKS, GLA_BLOCK, GLA_DK)
    return pl.pallas_call(
        functools.partial(_gla_kernel, reverse=reverse),
        grid=(bs, GLA_HEADS // nh, nt),
        in_specs=in_specs,
        out_specs=out_spec,
        out_shape=jax.ShapeDtypeStruct((bs, seq, GLA_VAL), BF16),
        scratch_shapes=[pltpu.VMEM((nh, GLA_DV, GLA_DK), F32),
                        pltpu.VMEM(block_scratch, F32),
                        pltpu.VMEM(block_scratch, F32),
                        pltpu.VMEM(block_scratch, BF16)],
        compiler_params=_params(("parallel", "parallel", "arbitrary")),
        name="gla_bwd" if reverse else "gla_fwd",
    )(*args)


def _merge_kernel(ya_ref, yb_ref, wa_ref, wb_ref, ga_ref, gb_ref, o_ref):
    ua = jnp.dot(ya_ref[...], wa_ref[...].astype(BF16), preferred_element_type=F32)
    ub = jnp.dot(yb_ref[...], wb_ref[...].astype(BF16), preferred_element_type=F32)
    ga = ga_ref[...].astype(F32)
    gb = gb_ref[...].astype(F32)
    merged = ua / (1.0 + jnp.exp(-ga)) + ub / (1.0 + jnp.exp(-gb))
    o_ref[...] = merged.astype(o_ref.dtype)


def _merge(ya, yb, wa, wb, gates, tm=1024, tn=1024):
    m = ya.shape[0]
    return pl.pallas_call(
        _merge_kernel,
        grid=(m // tm, D_MODEL // tn),
        in_specs=[pl.BlockSpec((tm, ATT_WIDTH), lambda i, j: (i, 0)),
                  pl.BlockSpec((tm, GLA_VAL), lambda i, j: (i, 0)),
                  pl.BlockSpec((ATT_WIDTH, tn), lambda i, j: (0, j)),
                  pl.BlockSpec((GLA_VAL, tn), lambda i, j: (0, j)),
                  pl.BlockSpec((tm, tn), lambda i, j: (i, j)),
                  pl.BlockSpec((tm, tn), lambda i, j: (i, D_MODEL // tn + j))],
        out_specs=pl.BlockSpec((tm, tn), lambda i, j: (i, j)),
        out_shape=jax.ShapeDtypeStruct((m, D_MODEL), BF16),
        compiler_params=pltpu.CompilerParams(dimension_semantics=("parallel", "arbitrary"),
                                             vmem_limit_bytes=BIG_VMEM_LIMIT),
        name="merge",
    )(ya, yb, wa, wb, gates, gates)


def _out_kernel(x_ref, a_ref, w_ref, g_ref, o_ref):
    u = x_ref[...] + jnp.dot(a_ref[...], w_ref[...], preferred_element_type=F32)
    ms = jnp.mean(u * u, axis=-1, keepdims=True)
    o_ref[...] = (u * lax.rsqrt(ms + NORM_EPS)) * g_ref[...]


def _out_proj(x, merged, w_o, g, tm=256):
    m = x.shape[0]
    return pl.pallas_call(
        _out_kernel,
        grid=(m // tm,),
        in_specs=[pl.BlockSpec((tm, D_MODEL), lambda i: (i, 0)),
                  pl.BlockSpec((tm, D_MODEL), lambda i: (i, 0)),
                  pl.BlockSpec((D_MODEL, D_MODEL), lambda i: (0, 0),
                               pipeline_mode=pl.Buffered(1)),
                  pl.BlockSpec((1, D_MODEL), lambda i: (0, 0))],
        out_specs=pl.BlockSpec((tm, D_MODEL), lambda i: (i, 0)),
        out_shape=jax.ShapeDtypeStruct((m, D_MODEL), F32),
        compiler_params=pltpu.CompilerParams(dimension_semantics=("parallel",),
                                             vmem_limit_bytes=BIG_VMEM_LIMIT),
        name="out_proj",
    )(x, merged, w_o, g.reshape(1, D_MODEL))


IN_PROJ_TN = 1024


def _layer(x, w, final_g):
    bs, seq, d = x.shape
    m = bs * seq
    x2 = x.reshape(m, d)
    h, h4, h16, lr3 = _rmsnorm(x, w["norm_g"], w["wt_lr"])
    group_tiles = GROUP_COLS // IN_PROJ_TN
    proj = _matmul_nt(h.reshape(m, d), w["wt_a"], N_PROJ,
                      lambda j: jnp.where(j < group_tiles, j, j + 2 * group_tiles),
                      "in_proj")
    proj3 = proj.reshape(bs, seq, N_PROJ)
    cls = []
    for g, (dilation, hc) in enumerate(((4, h4), (16, h16)), start=1):
        p = _matmul_nt(hc.reshape(m, d), w["wt_a"], GROUP_COLS,
                       lambda j, g=g: j + g * group_tiles, f"in_proj_d{dilation}")
        cls.append(p.reshape(bs, dilation, seq // dilation, GROUP_COLS))
    gates = _matmul_nt(h.reshape(m, d), w["wt_a"], 2 * D_MODEL, lambda j: j, "in_proj_gates",
                       row0=W_OFF_MG)
    ya = _attention(w["slopes"], proj3, *cls)

    o_f = _gla_direction(proj3, lr3, w["w2_f"], w["b_f"], False)
    yb = _gla_direction(proj3, lr3, w["w2_b"], w["b_b"], True, o_f, w["gla_norm_g"])

    merged = _merge(ya.reshape(m, ATT_WIDTH), yb.reshape(m, GLA_VAL),
                    w["w_up_a"], w["w_up_b"], gates)
    y = _out_proj(x2, merged, w["w_o"], final_g)
    return y.reshape(bs, seq, d)


def _pad_rows(w2, row0):
    return jnp.zeros((LR_PAD, GLA_KEY), BF16).at[row0:row0 + GLA_RANK].set(w2.astype(BF16))


def kernel(x_prompt, x_sample, norm_g, w_in, gla_w2_f, gla_b_f, gla_w2_b, gla_b_b,
           gla_norm_g, w_up_a, w_up_b, w_o, final_norm_g):
    assert norm_g.shape[0] == 1, "single-layer kernel"
    wt = w_in[0].T
    w = {
        "norm_g": norm_g[0],
        "wt_a": wt,
        "wt_lr": jnp.pad(wt[W_OFF_LR:W_OFF_MG],
                         ((0, LR_PAD - 2 * GLA_RANK), (0, 0))).astype(BF16),
        "w2_f": _pad_rows(gla_w2_f[0], 0),
        "w2_b": _pad_rows(gla_w2_b[0], GLA_RANK),
        "b_f": gla_b_f[0].reshape(1, GLA_KEY),
        "b_b": gla_b_b[0].reshape(1, GLA_KEY),
        "gla_norm_g": gla_norm_g[0].reshape(1, GLA_DV),
        "w_up_a": w_up_a[0],
        "w_up_b": w_up_b[0],
        "w_o": w_o[0].astype(BF16),
        "slopes": jnp.exp2(-8.0 * (jnp.arange(ATT_HEADS, dtype=F32) + 1.0) / ATT_HEADS),
    }
    return (_layer(x_prompt, w, final_norm_g), _layer(x_sample, w, final_norm_g))
```

```python
import functools

import jax
import jax.numpy as jnp
import numpy as np
from jax import lax
from jax.experimental import pallas as pl
from jax.experimental.pallas import tpu as pltpu

F32 = jnp.float32
BF16 = jnp.bfloat16

D_MODEL = 4096
ATT_DILATIONS = (1, 4, 16)
N_GROUPS = 3
ATT_HEAD_DIM = 128
ATT_HEADS = 8
ATT_WIDTH = ATT_HEADS * ATT_HEAD_DIM
ATT_RADIUS = 64
GLA_HEADS = 4
GLA_KEY = 1024
GLA_VAL = 2048
GLA_DK = GLA_KEY // GLA_HEADS
GLA_DV = GLA_VAL // GLA_HEADS
GLA_RANK = 16
GLA_TAU = 16.0
GLA_CHUNK = 64
NORM_EPS = 1e-6
NEG_INF = -1e30

GROUP_COLS = 3 * ATT_WIDTH
W_OFF_AZ = N_GROUPS * GROUP_COLS
W_OFF_LR = W_OFF_AZ + ATT_WIDTH + 2 * GLA_KEY + 2 * GLA_VAL
W_OFF_MG = W_OFF_LR + 2 * GLA_RANK
OFF_AZ = GROUP_COLS
OFF_GQ = OFF_AZ + ATT_WIDTH
OFF_GK = OFF_GQ + GLA_KEY
OFF_GV = OFF_GK + GLA_KEY
OFF_GZ = OFF_GV + GLA_VAL
OFF_MG = OFF_GZ + GLA_VAL
N_PROJ = OFF_MG + 2 * D_MODEL
LR_PAD = 128

VMEM_LIMIT = 56 * 1024 * 1024
BIG_VMEM_LIMIT = 62 * 1024 * 1024


def _params(sem):
    return pltpu.CompilerParams(dimension_semantics=sem, vmem_limit_bytes=VMEM_LIMIT)


NORM_ROWS = 256
NT_DIMS = (((1,), (1,)), ((), ()))


def _class_major_permutation():
    perm = np.zeros((2 * NORM_ROWS, NORM_ROWS), np.float32)
    for block, dilation in enumerate((4, 16)):
        per_class = NORM_ROWS // dilation
        for r in range(dilation):
            for u in range(per_class):
                perm[block * NORM_ROWS + r * per_class + u, u * dilation + r] = 1.0
    return jnp.asarray(perm, BF16)


def _rmsnorm_kernel(x_ref, g_ref, wlr_ref, perm_ref, h_ref, h4_ref, h16_ref, lr_ref):
    x = x_ref[0]
    ms = jnp.mean(x * x, axis=-1, keepdims=True)
    h = ((x * lax.rsqrt(ms + NORM_EPS)) * g_ref[...]).astype(h_ref.dtype)
    h_ref[0] = h
    lr_ref[0] = lax.dot_general(h, wlr_ref[...], NT_DIMS,
                                preferred_element_type=F32).astype(lr_ref.dtype)
    hp = jnp.dot(perm_ref[...], h, preferred_element_type=F32).astype(h_ref.dtype)
    for block, (out_ref, dilation) in enumerate(((h4_ref, 4), (h16_ref, 16))):
        per_class = NORM_ROWS // dilation
        for r in range(dilation):
            row0 = block * NORM_ROWS + r * per_class
            out_ref[0, r] = hp[row0:row0 + per_class]


def _rmsnorm(x, g, wt_lr):
    bs, seq, d = x.shape
    tm = NORM_ROWS

    def cls_spec(dilation):
        return pl.BlockSpec((1, dilation, tm // dilation, d), lambda b, i: (b, 0, i, 0))

    def cls_shape(dilation):
        return jax.ShapeDtypeStruct((bs, dilation, seq // dilation, d), BF16)

    return pl.pallas_call(
        _rmsnorm_kernel,
        grid=(bs, seq // tm),
        in_specs=[pl.BlockSpec((1, tm, d), lambda b, i: (b, i, 0)),
                  pl.BlockSpec((1, d), lambda b, i: (0, 0)),
                  pl.BlockSpec((LR_PAD, d), lambda b, i: (0, 0)),
                  pl.BlockSpec((2 * tm, tm), lambda b, i: (0, 0))],
        out_specs=[pl.BlockSpec((1, tm, d), lambda b, i: (b, i, 0)),
                   cls_spec(4), cls_spec(16),
                   pl.BlockSpec((1, tm, LR_PAD), lambda b, i: (b, i, 0))],
        out_shape=[jax.ShapeDtypeStruct((bs, seq, d), BF16), cls_shape(4), cls_shape(16),
                   jax.ShapeDtypeStruct((bs, seq, LR_PAD), BF16)],
        compiler_params=_params(("parallel", "parallel")),
        name="rmsnorm",
    )(x, g.reshape(1, d), wt_lr, _class_major_permutation())


IN_PROJ_TN = 1024


def _matmul_nt_kernel(a_ref, bt_ref, o_ref):
    o_ref[...] = lax.dot_general(a_ref[...], bt_ref[...].astype(BF16), NT_DIMS,
                                 preferred_element_type=F32).astype(o_ref.dtype)


def _matmul_nt(a, bt, n_out, row_start, name, tm=1024, tn=IN_PROJ_TN):
    m, k = a.shape
    return pl.pallas_call(
        _matmul_nt_kernel,
        grid=(n_out // tn, m // tm),
        in_specs=[pl.BlockSpec((tm, k), lambda j, i: (i, 0)),
                  pl.BlockSpec((pl.Element(tn), pl.Element(k)),
                               lambda j, i: (pl.multiple_of(row_start(j), 8), 0))],
        out_specs=pl.BlockSpec((tm, tn), lambda j, i: (i, j)),
        out_shape=jax.ShapeDtypeStruct((m, n_out), BF16),
        compiler_params=pltpu.CompilerParams(dimension_semantics=("parallel", "arbitrary"),
                                             vmem_limit_bytes=BIG_VMEM_LIMIT),
        name=name,
    )(a, bt)


ATT_TILE = 2048
ATT_QB = 128
ATT_KW = ATT_QB + 2 * ATT_RADIUS
ATT_WINDOW_OFFSETS = (-ATT_RADIUS, 0, -2 * ATT_RADIUS)


def _attn_kernel(slopes_ref, q1_ref, k1_ref, v1_ref, q2_ref, k2_ref, v2_ref,
                 q3_ref, k3_ref, v3_ref, z_ref, y_ref, o_scr, l_scr, bias_scr, *, seq):
    h = pl.program_id(1)
    tile = pl.program_id(2)
    slope = slopes_ref[h]
    scale = ATT_HEAD_DIM ** -0.5

    @pl.when(tile == 0)
    def _():
        col_minus_row = (lax.broadcasted_iota(jnp.int32, (ATT_QB, ATT_KW), 1)
                         - lax.broadcasted_iota(jnp.int32, (ATT_QB, ATT_KW), 0))
        col = lax.broadcasted_iota(jnp.int32, (ATT_QB, ATT_KW), 1)
        for g, dilation in enumerate(ATT_DILATIONS):
            if seq // dilation >= ATT_KW:
                valid = [jnp.abs(col_minus_row + off) for off in ATT_WINDOW_OFFSETS]
            else:
                valid = [jnp.where(col // ATT_QB == c, jnp.abs(col_minus_row - c * ATT_QB),
                                   ATT_KW) for c in range(2)]
            for v, dist in enumerate(valid):
                bias_scr[g * len(ATT_WINDOW_OFFSETS) + v] = jnp.where(
                    dist <= ATT_RADIUS, -(slope * float(dilation)) * dist.astype(F32),
                    NEG_INF)

    for g, (q_ref, k_ref, v_ref) in enumerate(((q1_ref, k1_ref, v1_ref),
                                               (q2_ref, k2_ref, v2_ref),
                                               (q3_ref, k3_ref, v3_ref))):
        dilation = ATT_DILATIONS[g]
        length = seq // dilation
        per_class = ATT_TILE // dilation // ATT_QB
        paired = length < ATT_KW
        for r in range(dilation):
            for j in range(per_class):
                row0 = j * ATT_QB
                if paired:
                    kv_class, variant, w0 = r // 2, r % 2, 0
                else:
                    p0 = tile * (ATT_TILE // dilation) + row0
                    w0 = pl.multiple_of(jnp.clip(p0 - ATT_RADIUS, 0, length - ATT_KW),
                                        ATT_RADIUS)
                    off = w0 - p0
                    variant = jnp.where(off == ATT_WINDOW_OFFSETS[1], 1,
                                        jnp.where(off == ATT_WINDOW_OFFSETS[2], 2, 0))
                    kv_class = r
                bias = bias_scr[g * len(ATT_WINDOW_OFFSETS) + variant]
                q = q_ref[0, r, row0:row0 + ATT_QB, :]
                s = lax.dot_general(q, k_ref[0, kv_class, pl.ds(w0, ATT_KW), :], NT_DIMS,
                                    preferred_element_type=F32) * scale + bias
                m = jnp.max(s, axis=-1, keepdims=True)
                e = jnp.exp(s - m)
                l = jnp.sum(e, axis=-1, keepdims=True)
                o = jnp.dot(e.astype(BF16), v_ref[0, kv_class, pl.ds(w0, ATT_KW), :],
                            preferred_element_type=F32) / l
                lse = jnp.broadcast_to(m + jnp.log(l), (ATT_QB, ATT_HEAD_DIM))
                if dilation == 1:
                    rows = pl.ds(row0, ATT_QB)
                else:
                    rows = pl.ds(row0 * dilation + r, ATT_QB, stride=dilation)
                o_scr[g, rows, :] = o
                l_scr[g, rows, :] = lse

    la, lb, lc = l_scr[0], l_scr[1], l_scr[2]
    m = jnp.maximum(jnp.maximum(la, lb), lc)
    ea, eb, ec = jnp.exp(la - m), jnp.exp(lb - m), jnp.exp(lc - m)
    mix = (ea * o_scr[0] + eb * o_scr[1] + ec * o_scr[2]) / (ea + eb + ec)
    z = z_ref[0].astype(F32)
    y_ref[0] = (mix * (z / (1.0 + jnp.exp(-z)))).astype(y_ref.dtype)


def _attention(slopes, proj3, cls4, cls16):
    bs, seq, _ = proj3.shape
    n_tiles = seq // ATT_TILE
    nat = proj3.reshape(bs, 1, seq, N_PROJ)

    def kv_shape(dilation):
        length = seq // dilation
        return (dilation, length) if length >= ATT_KW else (dilation // 2, 2 * length)

    def specs(dilation):
        tq = ATT_TILE // dilation
        q = pl.BlockSpec((1, dilation, tq, ATT_HEAD_DIM), lambda b, h, t: (b, 0, t, h))
        k = pl.BlockSpec((1,) + kv_shape(dilation) + (ATT_HEAD_DIM,),
                         lambda b, h, t: (b, 0, 0, ATT_HEADS + h))
        v = pl.BlockSpec((1,) + kv_shape(dilation) + (ATT_HEAD_DIM,),
                         lambda b, h, t: (b, 0, 0, 2 * ATT_HEADS + h))
        return [q, k, v]

    def kv_view(t, dilation):
        return t.reshape((bs,) + kv_shape(dilation) + (t.shape[-1],))

    tok = pl.BlockSpec((1, ATT_TILE, ATT_HEAD_DIM),
                       lambda b, h, t: (b, t, OFF_AZ // ATT_HEAD_DIM + h))
    return pl.pallas_call(
        functools.partial(_attn_kernel, seq=seq),
        grid=(bs, ATT_HEADS, n_tiles),
        in_specs=([pl.BlockSpec(memory_space=pltpu.SMEM)]
                  + specs(1) + specs(4) + specs(16) + [tok]),
        out_specs=pl.BlockSpec((1, ATT_TILE, ATT_HEAD_DIM), lambda b, h, t: (b, t, h)),
        out_shape=jax.ShapeDtypeStruct((bs, seq, ATT_WIDTH), BF16),
        scratch_shapes=[pltpu.VMEM((N_GROUPS, ATT_TILE, ATT_HEAD_DIM), F32),
                        pltpu.VMEM((N_GROUPS, ATT_TILE, ATT_HEAD_DIM), F32),
                        pltpu.VMEM((N_GROUPS * len(ATT_WINDOW_OFFSETS), ATT_QB, ATT_KW),
                                   F32)],
        compiler_params=_params(("parallel", "parallel", "arbitrary")),
        name="attention",
    )(slopes, nat, nat, nat, cls4, kv_view(cls4, 4), kv_view(cls4, 4),
      cls16, kv_view(cls16, 16), kv_view(cls16, 16), proj3)


GLA_BLOCK_CHUNKS = 4
GLA_BLOCK = GLA_BLOCK_CHUNKS * GLA_CHUNK
GLA_TILE_BLOCKS = 2
GLA_TILE = GLA_TILE_BLOCKS * GLA_BLOCK
GLA_STEP_HEADS = 4


def _gla_kernel(*refs, reverse):
    if reverse:
        (q_ref, k_ref, v_ref, z_ref, w2_ref, b2_ref, of_ref, gz_ref, g_ref,
         y_ref, st_ref, qe_scr, kd_scr, ke_scr) = refs
    else:
        (q_ref, k_ref, v_ref, z_ref, w2_ref, b2_ref,
         y_ref, st_ref, qe_scr, kd_scr, ke_scr) = refs

    @pl.when(pl.program_id(2) == 0)
    def _():
        st_ref[...] = jnp.zeros_like(st_ref)

    nc, ch = GLA_BLOCK_CHUNKS, GLA_CHUNK
    heads = range(GLA_STEP_HEADS)
    ri = lax.broadcasted_iota(jnp.int32, (GLA_BLOCK, GLA_BLOCK), 0)
    ci = lax.broadcasted_iota(jnp.int32, (GLA_BLOCK, GLA_BLOCK), 1)
    same_chunk = (ri // ch) == (ci // ch)
    if reverse:
        cum = (same_chunk & (ci >= ri)).astype(BF16)
        keep = (ci // ch > ri // ch) | (same_chunk & (ci > ri))
        end_row = 0
        order = list(range(nc - 1, -1, -1))
    else:
        cum = (same_chunk & (ci <= ri)).astype(BF16)
        keep = (ci // ch < ri // ch) | (same_chunk & (ci <= ri))
        end_row = ch - 1
        order = list(range(nc))
    pos = {c: p for p, c in enumerate(order)}

    def chunk(c):
        return slice(c * ch, (c + 1) * ch)

    blocks = range(GLA_TILE_BLOCKS)
    for blk in (reversed(blocks) if reverse else blocks):
        r0 = blk * GLA_BLOCK
        rows = slice(r0, r0 + GLA_BLOCK)
        x = jnp.dot(z_ref[0, rows, :], w2_ref[...],
                    preferred_element_type=F32) + b2_ref[...]
        la = (jnp.minimum(x, 0.0) - jnp.log(1.0 + jnp.exp(-jnp.abs(x)))) * (1.0 / GLA_TAU)
        hi = la.astype(BF16)
        lo = (la - hi.astype(F32)).astype(BF16)
        b_all = (jnp.dot(cum, hi, preferred_element_type=F32)
                 + jnp.dot(cum, lo, preferred_element_type=F32))
        total = []
        for hd in heads:
            cols = slice(hd * GLA_DK, (hd + 1) * GLA_DK)
            b = b_all[:, cols]
            ends = [b[c * ch + end_row:c * ch + end_row + 1, :] for c in range(nc)]
            tot = [jnp.zeros_like(ends[0])]
            for p in range(nc):
                tot.append(tot[-1] + ends[order[p]])
            total.append(tot)
            e_end = jnp.concatenate(
                [jnp.broadcast_to(jnp.exp(e), (ch, GLA_DK)) for e in ends], axis=0)
            ke = k_ref[0, rows, cols].astype(F32) * jnp.exp(-b)
            qe_scr[hd, blk] = (q_ref[0, rows, cols].astype(F32) * (GLA_DK ** -0.5)
                               * jnp.exp(b))
            ke_scr[hd, blk] = ke.astype(BF16)
            kd_scr[hd, blk] = ke * e_end
        att = []
        for hd in heads:
            tot = total[hd]
            att_rows = [None] * nc
            for p in range(nc):
                a = order[p]
                parts = []
                for c in range(nc):
                    if c == a:
                        parts.append(ke_scr[hd, blk, chunk(c), :])
                    elif pos[c] < p:
                        kd_c = kd_scr[hd, blk, chunk(c), :]
                        if pos[c] < p - 1:
                            kd_c = kd_c * jnp.exp(tot[p] - tot[pos[c] + 1])
                        parts.append(kd_c.astype(BF16))
                    else:
                        parts.append(jnp.zeros((ch, GLA_DK), BF16))
                att_rows[a] = lax.dot_general(
                    qe_scr[hd, blk, chunk(a), :].astype(BF16),
                    jnp.concatenate(parts, axis=0), NT_DIMS, preferred_element_type=F32)
            att.append(jnp.where(keep, jnp.concatenate(att_rows, axis=0), 0.0).astype(BF16))
        for hd in heads:
            tot = total[hd]
            vcols = slice(hd * GLA_DV, (hd + 1) * GLA_DV)
            st = st_ref[hd]
            qe_in = jnp.concatenate(
                [(qe_scr[hd, blk, chunk(c), :] * jnp.exp(tot[pos[c]])).astype(BF16)
                 for c in range(nc)], axis=0)
            v = v_ref[0, rows, vcols]
            o = (jnp.dot(att[hd], v, preferred_element_type=F32)
                 + lax.dot_general(qe_in, st.astype(BF16), NT_DIMS,
                                   preferred_element_type=F32))
            if reverse:
                o = o + of_ref[0, rows, vcols]
                ms = jnp.mean(o * o, axis=-1, keepdims=True)
                o = o * lax.rsqrt(ms + NORM_EPS) * g_ref[...]
                gz = gz_ref[0, rows, vcols].astype(F32)
                y_ref[0, rows, vcols] = (o * (gz / (1.0 + jnp.exp(-gz)))).astype(y_ref.dtype)
            else:
                y_ref[0, rows, vcols] = o
            k_all = jnp.concatenate(
                [(kd_scr[hd, blk, chunk(c), :]
                  * jnp.exp(tot[nc] - tot[pos[c] + 1])).astype(BF16)
                 for c in range(nc)], axis=0)
            vt = v.astype(F32).T.astype(BF16)
            st_ref[hd] = (st * jnp.exp(tot[nc])
                          + jnp.dot(vt, k_all, preferred_element_type=F32))


def _gla_direction(proj3, lr3, w2pad, bias, reverse, o_fwd=None, norm_g=None):
    bs, seq, _ = proj3.shape
    nt = seq // GLA_TILE
    nh = GLA_STEP_HEADS

    def tile(t):
        return nt - 1 - t if reverse else t

    def spec(width, off):
        return pl.BlockSpec((1, GLA_TILE, nh * width),
                            lambda b, h, t: (b, tile(t), off // (nh * width) + h))

    in_specs = [spec(GLA_DK, OFF_GQ), spec(GLA_DK, OFF_GK), spec(GLA_DV, OFF_GV),
                pl.BlockSpec((1, GLA_TILE, LR_PAD), lambda b, h, t: (b, tile(t), 0)),
                pl.BlockSpec((LR_PAD, nh * GLA_DK), lambda b, h, t: (0, h)),
                pl.BlockSpec((1, nh * GLA_DK), lambda b, h, t: (0, h))]
    args = [proj3, proj3, proj3, lr3, w2pad, bias]
    out_spec = pl.BlockSpec((1, GLA_TILE, nh * GLA_DV), lambda b, h, t: (b, tile(t), h))
    if reverse:
        in_specs += [out_spec, spec(GLA_DV, OFF_GZ),
                     pl.BlockSpec((1, GLA_DV), lambda b, h, t: (0, 0))]
        args += [o_fwd, proj3, norm_g]
    block_scratch = (nh, GLA_TILE_BLOCKS, GLA_BLOCK, GLA_DK)
    return pl.pallas_call(
        functools.partial(_gla_kernel, reverse=reverse),
        grid=(bs, GLA_HEADS // nh, nt),
        in_specs=in_specs,
        out_specs=out_spec,
        out_shape=jax.ShapeDtypeStruct((bs, seq, GLA_VAL), BF16 if reverse else F32),
        scratch_shapes=[pltpu.VMEM((nh, GLA_DV, GLA_DK), F32),
                        pltpu.VMEM(block_scratch, F32),
                        pltpu.VMEM(block_scratch, F32),
                        pltpu.VMEM(block_scratch, BF16)],
        compiler_params=_params(("parallel", "parallel", "arbitrary")),
        name="gla_bwd" if reverse else "gla_fwd",
    )(*args)


def _merge_kernel(ya_ref, yb_ref, wa_ref, wb_ref, ga_ref, gb_ref, o_ref):
    ua = jnp.dot(ya_ref[...], wa_ref[...].astype(BF16), preferred_element_type=F32)
    ub = jnp.dot(yb_ref[...], wb_ref[...].astype(BF16), preferred_element_type=F32)
    ga = ga_ref[...].astype(F32)
    gb = gb_ref[...].astype(F32)
    merged = ua / (1.0 + jnp.exp(-ga)) + ub / (1.0 + jnp.exp(-gb))
    o_ref[...] = merged.astype(o_ref.dtype)


def _merge(ya, yb, wa, wb, proj, tm=1024, tn=1024):
    m = ya.shape[0]
    return pl.pallas_call(
        _merge_kernel,
        grid=(m // tm, D_MODEL // tn),
        in_specs=[pl.BlockSpec((tm, ATT_WIDTH), lambda i, j: (i, 0)),
                  pl.BlockSpec((tm, GLA_VAL), lambda i, j: (i, 0)),
                  pl.BlockSpec((ATT_WIDTH, tn), lambda i, j: (0, j)),
                  pl.BlockSpec((GLA_VAL, tn), lambda i, j: (0, j)),
                  pl.BlockSpec((tm, tn), lambda i, j: (i, OFF_MG // tn + j)),
                  pl.BlockSpec((tm, tn), lambda i, j: (i, (OFF_MG + D_MODEL) // tn + j))],
        out_specs=pl.BlockSpec((tm, tn), lambda i, j: (i, j)),
        out_shape=jax.ShapeDtypeStruct((m, D_MODEL), BF16),
        compiler_params=pltpu.CompilerParams(dimension_semantics=("parallel", "arbitrary"),
                                             vmem_limit_bytes=BIG_VMEM_LIMIT),
        name="merge",
    )(ya, yb, wa, wb, proj, proj)


def _out_kernel(x_ref, a_ref, w_ref, g_ref, o_ref):
    u = x_ref[...] + jnp.dot(a_ref[...], w_ref[...], preferred_element_type=F32)
    ms = jnp.mean(u * u, axis=-1, keepdims=True)
    o_ref[...] = (u * lax.rsqrt(ms + NORM_EPS)) * g_ref[...]


def _out_proj(x, merged, w_o, g, tm=256):
    m = x.shape[0]
    return pl.pallas_call(
        _out_kernel,
        grid=(m // tm,),
        in_specs=[pl.BlockSpec((tm, D_MODEL), lambda i: (i, 0)),
                  pl.BlockSpec((tm, D_MODEL), lambda i: (i, 0)),
                  pl.BlockSpec((D_MODEL, D_MODEL), lambda i: (0, 0),
                               pipeline_mode=pl.Buffered(1)),
                  pl.BlockSpec((1, D_MODEL), lambda i: (0, 0))],
        out_specs=pl.BlockSpec((tm, D_MODEL), lambda i: (i, 0)),
        out_shape=jax.ShapeDtypeStruct((m, D_MODEL), F32),
        compiler_params=pltpu.CompilerParams(dimension_semantics=("parallel",),
                                             vmem_limit_bytes=BIG_VMEM_LIMIT),
        name="out_proj",
    )(x, merged, w_o, g.reshape(1, D_MODEL))


def _layer(x, w, final_g):
    bs, seq, d = x.shape
    m = bs * seq
    x2 = x.reshape(m, d)
    h, h4, h16, lr3 = _rmsnorm(x, w["norm_g"], w["wt_lr"])
    tn = IN_PROJ_TN
    first_tiles, main_tiles = GROUP_COLS // tn, OFF_MG // tn

    def token_order_rows(j):
        return jnp.where(j < first_tiles, j * tn,
                         jnp.where(j < main_tiles, j * tn + (N_GROUPS - 1) * GROUP_COLS,
                                   W_OFF_MG + (j - main_tiles) * tn))

    proj = _matmul_nt(h.reshape(m, d), w["wt"], N_PROJ, token_order_rows, "in_proj")
    proj3 = proj.reshape(bs, seq, N_PROJ)
    cls = []
    for g, (dilation, hc) in enumerate(((4, h4), (16, h16)), start=1):
        p = _matmul_nt(hc.reshape(m, d), w["wt"], GROUP_COLS,
                       lambda j, g=g: j * tn + g * GROUP_COLS, f"in_proj_d{dilation}")
        cls.append(p.reshape(bs, dilation, seq // dilation, GROUP_COLS))
    ya = _attention(w["slopes"], proj3, *cls)

    o_f = _gla_direction(proj3, lr3, w["w2_f"], w["b_f"], False)
    yb = _gla_direction(proj3, lr3, w["w2_b"], w["b_b"], True, o_f, w["gla_norm_g"])

    merged = _merge(ya.reshape(m, ATT_WIDTH), yb.reshape(m, GLA_VAL),
                    w["w_up_a"], w["w_up_b"], proj)
    y = _out_proj(x2, merged, w["w_o"], final_g)
    return y.reshape(bs, seq, d)


def _pad_rows(w2, row0):
    return jnp.zeros((LR_PAD, GLA_KEY), BF16).at[row0:row0 + GLA_RANK].set(w2.astype(BF16))


def kernel(x_prompt, x_sample, norm_g, w_in, gla_w2_f, gla_b_f, gla_w2_b, gla_b_b,
           gla_norm_g, w_up_a, w_up_b, w_o, final_norm_g):
    assert norm_g.shape[0] == 1, "single-layer kernel"
    wt = w_in[0].T
    w = {
        "norm_g": norm_g[0],
        "wt": wt,
        "wt_lr": jnp.pad(wt[W_OFF_LR:W_OFF_MG],
                         ((0, LR_PAD - 2 * GLA_RANK), (0, 0))).astype(BF16),
        "w2_f": _pad_rows(gla_w2_f[0], 0),
        "w2_b": _pad_rows(gla_w2_b[0], GLA_RANK),
        "b_f": gla_b_f[0].reshape(1, GLA_KEY),
        "b_b": gla_b_b[0].reshape(1, GLA_KEY),
        "gla_norm_g": gla_norm_g[0].reshape(1, GLA_DV),
        "w_up_a": w_up_a[0],
        "w_up_b": w_up_b[0],
        "w_o": w_o[0].astype(BF16),
        "slopes": jnp.exp2(-8.0 * (jnp.arange(ATT_HEADS, dtype=F32) + 1.0) / ATT_HEADS),
    }
    return (_layer(x_prompt, w, final_norm_g), _layer(x_sample, w, final_norm_g))
```

```python
import functools

import jax
import jax.numpy as jnp
import numpy as np
from jax import lax
from jax.experimental import pallas as pl
from jax.experimental.pallas import tpu as pltpu

F32 = jnp.float32
BF16 = jnp.bfloat16

D_MODEL = 4096
ATT_DILATIONS = (1, 4, 16)
N_GROUPS = 3
ATT_HEAD_DIM = 128
ATT_HEADS = 8
ATT_WIDTH = ATT_HEADS * ATT_HEAD_DIM
ATT_RADIUS = 64
GLA_HEADS = 4
GLA_KEY = 1024
GLA_VAL = 2048
GLA_DK = GLA_KEY // GLA_HEADS
GLA_DV = GLA_VAL // GLA_HEADS
GLA_RANK = 16
GLA_TAU = 16.0
GLA_CHUNK = 64
NORM_EPS = 1e-6
NEG_INF = -1e30

GROUP_COLS = 3 * ATT_WIDTH
W_OFF_AZ = N_GROUPS * GROUP_COLS
W_OFF_LR = W_OFF_AZ + ATT_WIDTH + 2 * GLA_KEY + 2 * GLA_VAL
W_OFF_MG = W_OFF_LR + 2 * GLA_RANK
OFF_AZ = GROUP_COLS
OFF_GQ = OFF_AZ + ATT_WIDTH
OFF_GK = OFF_GQ + GLA_KEY
OFF_GV = OFF_GK + GLA_KEY
OFF_GZ = OFF_GV + GLA_VAL
OFF_MG = OFF_GZ + GLA_VAL
N_PROJ = OFF_MG + 2 * D_MODEL
LR_PAD = 128

VMEM_LIMIT = 56 * 1024 * 1024
BIG_VMEM_LIMIT = 62 * 1024 * 1024


def _params(sem):
    return pltpu.CompilerParams(dimension_semantics=sem, vmem_limit_bytes=VMEM_LIMIT)


NORM_ROWS = 256
NT_DIMS = (((1,), (1,)), ((), ()))


def _class_major_permutation():
    perm = np.zeros((2 * NORM_ROWS, NORM_ROWS), np.float32)
    for block, dilation in enumerate((4, 16)):
        per_class = NORM_ROWS // dilation
        for r in range(dilation):
            for u in range(per_class):
                perm[block * NORM_ROWS + r * per_class + u, u * dilation + r] = 1.0
    return jnp.asarray(perm, BF16)


def _rmsnorm_kernel(x_ref, g_ref, wlr_ref, perm_ref, h_ref, h4_ref, h16_ref, lr_ref):
    x = x_ref[0]
    ms = jnp.mean(x * x, axis=-1, keepdims=True)
    h = ((x * lax.rsqrt(ms + NORM_EPS)) * g_ref[...]).astype(h_ref.dtype)
    h_ref[0] = h
    lr_ref[0] = lax.dot_general(h, wlr_ref[...], NT_DIMS,
                                preferred_element_type=F32).astype(lr_ref.dtype)
    hp = jnp.dot(perm_ref[...], h, preferred_element_type=F32).astype(h_ref.dtype)
    for block, (out_ref, dilation) in enumerate(((h4_ref, 4), (h16_ref, 16))):
        per_class = NORM_ROWS // dilation
        for r in range(dilation):
            row0 = block * NORM_ROWS + r * per_class
            out_ref[0, r] = hp[row0:row0 + per_class]


def _rmsnorm(x, g, wt_lr):
    bs, seq, d = x.shape
    tm = NORM_ROWS

    def cls_spec(dilation):
        return pl.BlockSpec((1, dilation, tm // dilation, d), lambda b, i: (b, 0, i, 0))

    def cls_shape(dilation):
        return jax.ShapeDtypeStruct((bs, dilation, seq // dilation, d), BF16)

    return pl.pallas_call(
        _rmsnorm_kernel,
        grid=(bs, seq // tm),
        in_specs=[pl.BlockSpec((1, tm, d), lambda b, i: (b, i, 0)),
                  pl.BlockSpec((1, d), lambda b, i: (0, 0)),
                  pl.BlockSpec((LR_PAD, d), lambda b, i: (0, 0)),
                  pl.BlockSpec((2 * tm, tm), lambda b, i: (0, 0))],
        out_specs=[pl.BlockSpec((1, tm, d), lambda b, i: (b, i, 0)),
                   cls_spec(4), cls_spec(16),
                   pl.BlockSpec((1, tm, LR_PAD), lambda b, i: (b, i, 0))],
        out_shape=[jax.ShapeDtypeStruct((bs, seq, d), BF16), cls_shape(4), cls_shape(16),
                   jax.ShapeDtypeStruct((bs, seq, LR_PAD), BF16)],
        compiler_params=_params(("parallel", "parallel")),
        name="rmsnorm",
    )(x, g.reshape(1, d), wt_lr, _class_major_permutation())


IN_PROJ_TN = 1024


def _matmul_nt_kernel(a_ref, bt_ref, o_ref):
    o_ref[...] = lax.dot_general(a_ref[...], bt_ref[...].astype(BF16), NT_DIMS,
                                 preferred_element_type=F32).astype(o_ref.dtype)


def _matmul_nt(a, bt, n_out, row_start, name, tm=1024, tn=IN_PROJ_TN):
    m, k = a.shape
    return pl.pallas_call(
        _matmul_nt_kernel,
        grid=(n_out // tn, m // tm),
        in_specs=[pl.BlockSpec((tm, k), lambda j, i: (i, 0)),
                  pl.BlockSpec((pl.Element(tn), pl.Element(k)),
                               lambda j, i: (pl.multiple_of(row_start(j), 8), 0))],
        out_specs=pl.BlockSpec((tm, tn), lambda j, i: (i, j)),
        out_shape=jax.ShapeDtypeStruct((m, n_out), BF16),
        compiler_params=pltpu.CompilerParams(dimension_semantics=("parallel", "arbitrary"),
                                             vmem_limit_bytes=BIG_VMEM_LIMIT),
        name=name,
    )(a, bt)


ATT_TILE = 2048
ATT_QB = 128
ATT_KW = ATT_QB + 2 * ATT_RADIUS
ATT_WINDOW_OFFSETS = (-ATT_RADIUS, 0, -2 * ATT_RADIUS)


def _attn_kernel(slopes_ref, q1_ref, k1_ref, v1_ref, q2_ref, k2_ref, v2_ref,
                 q3_ref, k3_ref, v3_ref, z_ref, y_ref, o_scr, l_scr, bias_scr, *, seq):
    h = pl.program_id(1)
    tile = pl.program_id(2)
    slope = slopes_ref[h]
    scale = ATT_HEAD_DIM ** -0.5

    @pl.when(tile == 0)
    def _():
        col_minus_row = (lax.broadcasted_iota(jnp.int32, (ATT_QB, ATT_KW), 1)
                         - lax.broadcasted_iota(jnp.int32, (ATT_QB, ATT_KW), 0))
        col = lax.broadcasted_iota(jnp.int32, (ATT_QB, ATT_KW), 1)
        for g, dilation in enumerate(ATT_DILATIONS):
            if seq // dilation >= ATT_KW:
                valid = [jnp.abs(col_minus_row + off) for off in ATT_WINDOW_OFFSETS]
            else:
                valid = [jnp.where(col // ATT_QB == c, jnp.abs(col_minus_row - c * ATT_QB),
                                   ATT_KW) for c in range(2)]
            for v, dist in enumerate(valid):
                bias_scr[g * len(ATT_WINDOW_OFFSETS) + v] = jnp.where(
                    dist <= ATT_RADIUS, -(slope * float(dilation)) * dist.astype(F32),
                    NEG_INF)

    for g, (q_ref, k_ref, v_ref) in enumerate(((q1_ref, k1_ref, v1_ref),
                                               (q2_ref, k2_ref, v2_ref),
                                               (q3_ref, k3_ref, v3_ref))):
        dilation = ATT_DILATIONS[g]
        length = seq // dilation
        per_class = ATT_TILE // dilation // ATT_QB
        paired = length < ATT_KW
        for r in range(dilation):
            for j in range(per_class):
                row0 = j * ATT_QB
                if paired:
                    kv_class, variant, w0 = r // 2, r % 2, 0
                else:
                    p0 = tile * (ATT_TILE // dilation) + row0
                    w0 = pl.multiple_of(jnp.clip(p0 - ATT_RADIUS, 0, length - ATT_KW),
                                        ATT_RADIUS)
                    off = w0 - p0
                    variant = jnp.where(off == ATT_WINDOW_OFFSETS[1], 1,
                                        jnp.where(off == ATT_WINDOW_OFFSETS[2], 2, 0))
                    kv_class = r
                bias = bias_scr[g * len(ATT_WINDOW_OFFSETS) + variant]
                q = q_ref[0, r, row0:row0 + ATT_QB, :]
                s = lax.dot_general(q, k_ref[0, kv_class, pl.ds(w0, ATT_KW), :], NT_DIMS,
                                    preferred_element_type=F32) * scale + bias
                m = jnp.max(s, axis=-1, keepdims=True)
                e = jnp.exp(s - m)
                l = jnp.sum(e, axis=-1, keepdims=True)
                o = jnp.dot(e.astype(BF16), v_ref[0, kv_class, pl.ds(w0, ATT_KW), :],
                            preferred_element_type=F32) / l
                lse = jnp.broadcast_to(m + jnp.log(l), (ATT_QB, ATT_HEAD_DIM))
                if dilation == 1:
                    rows = pl.ds(row0, ATT_QB)
                else:
                    rows = pl.ds(row0 * dilation + r, ATT_QB, stride=dilation)
                o_scr[g, rows, :] = o
                l_scr[g, rows, :] = lse

    la, lb, lc = l_scr[0], l_scr[1], l_scr[2]
    m = jnp.maximum(jnp.maximum(la, lb), lc)
    ea, eb, ec = jnp.exp(la - m), jnp.exp(lb - m), jnp.exp(lc - m)
    mix = (ea * o_scr[0] + eb * o_scr[1] + ec * o_scr[2]) / (ea + eb + ec)
    z = z_ref[0].astype(F32)
    y_ref[0] = (mix * (z / (1.0 + jnp.exp(-z)))).astype(y_ref.dtype)


def _attention(slopes, proj3, cls4, cls16):
    bs, seq, _ = proj3.shape
    n_tiles = seq // ATT_TILE
    nat = proj3.reshape(bs, 1, seq, N_PROJ)

    def kv_shape(dilation):
        length = seq // dilation
        return (dilation, length) if length >= ATT_KW else (dilation // 2, 2 * length)

    def specs(dilation):
        tq = ATT_TILE // dilation
        q = pl.BlockSpec((1, dilation, tq, ATT_HEAD_DIM), lambda b, h, t: (b, 0, t, h))
        k = pl.BlockSpec((1,) + kv_shape(dilation) + (ATT_HEAD_DIM,),
                         lambda b, h, t: (b, 0, 0, ATT_HEADS + h))
        v = pl.BlockSpec((1,) + kv_shape(dilation) + (ATT_HEAD_DIM,),
                         lambda b, h, t: (b, 0, 0, 2 * ATT_HEADS + h))
        return [q, k, v]

    def kv_view(t, dilation):
        return t.reshape((bs,) + kv_shape(dilation) + (t.shape[-1],))

    tok = pl.BlockSpec((1, ATT_TILE, ATT_HEAD_DIM),
                       lambda b, h, t: (b, t, OFF_AZ // ATT_HEAD_DIM + h))
    return pl.pallas_call(
        functools.partial(_attn_kernel, seq=seq),
        grid=(bs, ATT_HEADS, n_tiles),
        in_specs=([pl.BlockSpec(memory_space=pltpu.SMEM)]
                  + specs(1) + specs(4) + specs(16) + [tok]),
        out_specs=pl.BlockSpec((1, ATT_TILE, ATT_HEAD_DIM), lambda b, h, t: (b, t, h)),
        out_shape=jax.ShapeDtypeStruct((bs, seq, ATT_WIDTH), BF16),
        scratch_shapes=[pltpu.VMEM((N_GROUPS, ATT_TILE, ATT_HEAD_DIM), F32),
                        pltpu.VMEM((N_GROUPS, ATT_TILE, ATT_HEAD_DIM), F32),
                        pltpu.VMEM((N_GROUPS * len(ATT_WINDOW_OFFSETS), ATT_QB, ATT_KW),
                                   F32)],
        compiler_params=_params(("parallel", "parallel", "arbitrary")),
        name="attention",
    )(slopes, nat, nat, nat, cls4, kv_view(cls4, 4), kv_view(cls4, 4),
      cls16, kv_view(cls16, 16), kv_view(cls16, 16), proj3)


GLA_BLOCK_CHUNKS = 4
GLA_BLOCK = GLA_BLOCK_CHUNKS * GLA_CHUNK
GLA_TILE_BLOCKS = 2
GLA_TILE = GLA_TILE_BLOCKS * GLA_BLOCK
GLA_STEP_HEADS = 4


def _gla_kernel(*refs, reverse):
    if reverse:
        (q_ref, k_ref, v_ref, z_ref, w2_ref, b2_ref, of_ref, gz_ref, g_ref,
         y_ref, st_ref, qe_scr, kd_scr, ke_scr) = refs
    else:
        (q_ref, k_ref, v_ref, z_ref, w2_ref, b2_ref,
         y_ref, st_ref, qe_scr, kd_scr, ke_scr) = refs

    @pl.when(pl.program_id(2) == 0)
    def _():
        st_ref[...] = jnp.zeros_like(st_ref)

    nc, ch = GLA_BLOCK_CHUNKS, GLA_CHUNK
    heads = range(GLA_STEP_HEADS)
    ri = lax.broadcasted_iota(jnp.int32, (GLA_BLOCK, GLA_BLOCK), 0)
    ci = lax.broadcasted_iota(jnp.int32, (GLA_BLOCK, GLA_BLOCK), 1)
    same_chunk = (ri // ch) == (ci // ch)
    if reverse:
        cum = (same_chunk & (ci >= ri)).astype(BF16)
        keep = (ci // ch > ri // ch) | (same_chunk & (ci > ri))
        end_row = 0
        order = list(range(nc - 1, -1, -1))
    else:
        cum = (same_chunk & (ci <= ri)).astype(BF16)
        keep = (ci // ch < ri // ch) | (same_chunk & (ci <= ri))
        end_row = ch - 1
        order = list(range(nc))
    pos = {c: p for p, c in enumerate(order)}

    def chunk(c):
        return slice(c * ch, (c + 1) * ch)

    blocks = range(GLA_TILE_BLOCKS)
    for blk in (reversed(blocks) if reverse else blocks):
        r0 = blk * GLA_BLOCK
        rows = slice(r0, r0 + GLA_BLOCK)
        x = jnp.dot(z_ref[0, rows, :], w2_ref[...],
                    preferred_element_type=F32) + b2_ref[...]
        la = (jnp.minimum(x, 0.0) - jnp.log(1.0 + jnp.exp(-jnp.abs(x)))) * (1.0 / GLA_TAU)
        hi = la.astype(BF16)
        lo = (la - hi.astype(F32)).astype(BF16)
        b_all = (jnp.dot(cum, hi, preferred_element_type=F32)
                 + jnp.dot(cum, lo, preferred_element_type=F32))
        total = []
        for hd in heads:
            cols = slice(hd * GLA_DK, (hd + 1) * GLA_DK)
            b = b_all[:, cols]
            ends = [b[c * ch + end_row:c * ch + end_row + 1, :] for c in range(nc)]
            tot = [jnp.zeros_like(ends[0])]
            for p in range(nc):
                tot.append(tot[-1] + ends[order[p]])
            total.append(tot)
            e_end = jnp.concatenate(
                [jnp.broadcast_to(jnp.exp(e), (ch, GLA_DK)) for e in ends], axis=0)
            ke = k_ref[0, rows, cols].astype(F32) * jnp.exp(-b)
            qe_scr[hd, blk] = (q_ref[0, rows, cols].astype(F32) * (GLA_DK ** -0.5)
                               * jnp.exp(b))
            ke_scr[hd, blk] = ke.astype(BF16)
            kd_scr[hd, blk] = ke * e_end
        att = []
        for hd in heads:
            tot = total[hd]
            att_rows = [None] * nc
            for p in range(nc):
                a = order[p]
                parts = []
                for c in range(nc):
                    if c == a:
                        parts.append(ke_scr[hd, blk, chunk(c), :])
                    elif pos[c] < p:
                        kd_c = kd_scr[hd, blk, chunk(c), :]
                        if pos[c] < p - 1:
                            kd_c = kd_c * jnp.exp(tot[p] - tot[pos[c] + 1])
                        parts.append(kd_c.astype(BF16))
                    else:
                        parts.append(jnp.zeros((ch, GLA_DK), BF16))
                att_rows[a] = lax.dot_general(
                    qe_scr[hd, blk, chunk(a), :].astype(BF16),
                    jnp.concatenate(parts, axis=0), NT_DIMS, preferred_element_type=F32)
            att.append(jnp.where(keep, jnp.concatenate(att_rows, axis=0), 0.0).astype(BF16))
        for hd in heads:
            tot = total[hd]
            vcols = slice(hd * GLA_DV, (hd + 1) * GLA_DV)
            st = st_ref[hd]
            qe_in = jnp.concatenate(
                [(qe_scr[hd, blk, chunk(c), :] * jnp.exp(tot[pos[c]])).astype(BF16)
                 for c in range(nc)], axis=0)
            v = v_ref[0, rows, vcols]
            o = (jnp.dot(att[hd], v, preferred_element_type=F32)
                 + lax.dot_general(qe_in, st.astype(BF16), NT_DIMS,
                                   preferred_element_type=F32))
            if reverse:
                o = o + of_ref[0, rows, vcols]
                ms = jnp.mean(o * o, axis=-1, keepdims=True)
                o = o * lax.rsqrt(ms + NORM_EPS) * g_ref[...]
                gz = gz_ref[0, rows, vcols].astype(F32)
                y_ref[0, rows, vcols] = (o * (gz / (1.0 + jnp.exp(-gz)))).astype(y_ref.dtype)
            else:
                y_ref[0, rows, vcols] = o
            k_all = jnp.concatenate(
                [(kd_scr[hd, blk, chunk(c), :]
                  * jnp.exp(tot[nc] - tot[pos[c] + 1])).astype(BF16)
                 for c in range(nc)], axis=0)
            vt = v.astype(F32).T.astype(BF16)
            st_ref[hd] = (st * jnp.exp(tot[nc])
                          + jnp.dot(vt, k_all, preferred_element_type=F32))


def _gla_direction(proj3, lr3, w2pad, bias, reverse, o_fwd=None, norm_g=None):
    bs, seq, _ = proj3.shape
    nt = seq // GLA_TILE
    nh = GLA_STEP_HEADS

    def tile(t):
        return nt - 1 - t if reverse else t

    def spec(width, off):
        return pl.BlockSpec((1, GLA_TILE, nh * width),
                            lambda b, h, t: (b, tile(t), off // (nh * width) + h))

    in_specs = [spec(GLA_DK, OFF_GQ), spec(GLA_DK, OFF_GK), spec(GLA_DV, OFF_GV),
                pl.BlockSpec((1, GLA_TILE, LR_PAD), lambda b, h, t: (b, tile(t), 0)),
                pl.BlockSpec((LR_PAD, nh * GLA_DK), lambda b, h, t: (0, h)),
                pl.BlockSpec((1, nh * GLA_DK), lambda b, h, t: (0, h))]
    args = [proj3, proj3, proj3, lr3, w2pad, bias]
    out_spec = pl.BlockSpec((1, GLA_TILE, nh * GLA_DV), lambda b, h, t: (b, tile(t), h))
    if reverse:
        in_specs += [out_spec, spec(GLA_DV, OFF_GZ),
                     pl.BlockSpec((1, GLA_DV), lambda b, h, t: (0, 0))]
        args += [o_fwd, proj3, norm_g]
    block_scratch = (nh, GLA_TILE_BLOCKS, GLA_BLOCK, GLA_DK)
    return pl.pallas_call(
        functools.partial(_gla_kernel, reverse=reverse),
        grid=(bs, GLA_HEADS // nh, nt),
        in_specs=in_specs,
        out_specs=out_spec,
        out_shape=jax.ShapeDtypeStruct((bs, seq, GLA_VAL), BF16 if reverse else F32),
        scratch_shapes=[pltpu.VMEM((nh, GLA_DV, GLA_DK), F32),
                        pltpu.VMEM(block_scratch, F32),
                        pltpu.VMEM(block_scratch, F32),
                        pltpu.VMEM(block_scratch, BF16)],
        compiler_params=_params(("parallel", "parallel", "arbitrary")),
        name="gla_bwd" if reverse else "gla_fwd",
    )(*args)


MERGE_COLS = 2048


def _merge_kernel(ya_ref, yb_ref, wa_ref, wb_ref, *refs):
    gate_refs, o_ref = refs[:-1], refs[-1]
    halves = len(gate_refs) // 2
    ya = ya_ref[...]
    yb = yb_ref[...]
    for c in range(halves):
        cols = slice(c * MERGE_COLS, (c + 1) * MERGE_COLS)
        ua = jnp.dot(ya, wa_ref[:, cols], preferred_element_type=F32)
        ub = jnp.dot(yb, wb_ref[:, cols], preferred_element_type=F32)
        ga = gate_refs[c][...].astype(F32)
        gb = gate_refs[halves + c][...].astype(F32)
        merged = ua / (1.0 + jnp.exp(-ga)) + ub / (1.0 + jnp.exp(-gb))
        o_ref[:, cols] = merged.astype(o_ref.dtype)


def _merge(ya, yb, wa, wb, proj, tm=256):
    m = ya.shape[0]
    first = OFF_MG // MERGE_COLS
    n_gate = 2 * D_MODEL // MERGE_COLS

    def gate_spec(k):
        return pl.BlockSpec((tm, MERGE_COLS), lambda i: (i, first + k))

    return pl.pallas_call(
        _merge_kernel,
        grid=(m // tm,),
        in_specs=[pl.BlockSpec((tm, ATT_WIDTH), lambda i: (i, 0)),
                  pl.BlockSpec((tm, GLA_VAL), lambda i: (i, 0)),
                  pl.BlockSpec((ATT_WIDTH, D_MODEL), lambda i: (0, 0),
                               pipeline_mode=pl.Buffered(1)),
                  pl.BlockSpec((GLA_VAL, D_MODEL), lambda i: (0, 0),
                               pipeline_mode=pl.Buffered(1))]
                 + [gate_spec(k) for k in range(n_gate)],
        out_specs=pl.BlockSpec((tm, D_MODEL), lambda i: (i, 0)),
        out_shape=jax.ShapeDtypeStruct((m, D_MODEL), BF16),
        compiler_params=pltpu.CompilerParams(dimension_semantics=("parallel",),
                                             vmem_limit_bytes=BIG_VMEM_LIMIT),
        name="merge",
    )(ya, yb, wa, wb, *([proj] * n_gate))


def _out_kernel(x_ref, a_ref, w_ref, g_ref, o_ref):
    u = x_ref[...] + jnp.dot(a_ref[...], w_ref[...], preferred_element_type=F32)
    ms = jnp.mean(u * u, axis=-1, keepdims=True)
    o_ref[...] = (u * lax.rsqrt(ms + NORM_EPS)) * g_ref[...]


def _out_proj(x, merged, w_o, g, tm=256):
    m = x.shape[0]
    return pl.pallas_call(
        _out_kernel,
        grid=(m // tm,),
        in_specs=[pl.BlockSpec((tm, D_MODEL), lambda i: (i, 0)),
                  pl.BlockSpec((tm, D_MODEL), lambda i: (i, 0)),
                  pl.BlockSpec((D_MODEL, D_MODEL), lambda i: (0, 0),
                               pipeline_mode=pl.Buffered(1)),
                  pl.BlockSpec((1, D_MODEL), lambda i: (0, 0))],
        out_specs=pl.BlockSpec((tm, D_MODEL), lambda i: (i, 0)),
        out_shape=jax.ShapeDtypeStruct((m, D_MODEL), F32),
        compiler_params=pltpu.CompilerParams(dimension_semantics=("parallel",),
                                             vmem_limit_bytes=BIG_VMEM_LIMIT),
        name="out_proj",
    )(x, merged, w_o, g.reshape(1, D_MODEL))


def _layer(x, w, final_g):
    bs, seq, d = x.shape
    m = bs * seq
    x2 = x.reshape(m, d)
    h, h4, h16, lr3 = _rmsnorm(x, w["norm_g"], w["wt_lr"])
    tn = IN_PROJ_TN
    first_tiles, main_tiles = GROUP_COLS // tn, OFF_MG // tn

    def token_order_rows(j):
        return jnp.where(j < first_tiles, j * tn,
                         jnp.where(j < main_tiles, j * tn + (N_GROUPS - 1) * GROUP_COLS,
                                   W_OFF_MG + (j - main_tiles) * tn))

    proj = _matmul_nt(h.reshape(m, d), w["wt"], N_PROJ, token_order_rows, "in_proj")
    proj3 = proj.reshape(bs, seq, N_PROJ)
    cls = []
    for g, (dilation, hc) in enumerate(((4, h4), (16, h16)), start=1):
        p = _matmul_nt(hc.reshape(m, d), w["wt"], GROUP_COLS,
                       lambda j, g=g: j * tn + g * GROUP_COLS, f"in_proj_d{dilation}")
        cls.append(p.reshape(bs, dilation, seq // dilation, GROUP_COLS))
    ya = _attention(w["slopes"], proj3, *cls)

    o_f = _gla_direction(proj3, lr3, w["w2_f"], w["b_f"], False)
    yb = _gla_direction(proj3, lr3, w["w2_b"], w["b_b"], True, o_f, w["gla_norm_g"])

    merged = _merge(ya.reshape(m, ATT_WIDTH), yb.reshape(m, GLA_VAL),
                    w["w_up_a"], w["w_up_b"], proj)
    y = _out_proj(x2, merged, w["w_o"], final_g)
    return y.reshape(bs, seq, d)


def _pad_rows(w2, row0):
    return jnp.zeros((LR_PAD, GLA_KEY), BF16).at[row0:row0 + GLA_RANK].set(w2.astype(BF16))


def kernel(x_prompt, x_sample, norm_g, w_in, gla_w2_f, gla_b_f, gla_w2_b, gla_b_b,
           gla_norm_g, w_up_a, w_up_b, w_o, final_norm_g):
    assert norm_g.shape[0] == 1, "single-layer kernel"
    wt = w_in[0].T
    w = {
        "norm_g": norm_g[0],
        "wt": wt,
        "wt_lr": jnp.pad(wt[W_OFF_LR:W_OFF_MG],
                         ((0, LR_PAD - 2 * GLA_RANK), (0, 0))).astype(BF16),
        "w2_f": _pad_rows(gla_w2_f[0], 0),
        "w2_b": _pad_rows(gla_w2_b[0], GLA_RANK),
        "b_f": gla_b_f[0].reshape(1, GLA_KEY),
        "b_b": gla_b_b[0].reshape(1, GLA_KEY),
        "gla_norm_g": gla_norm_g[0].reshape(1, GLA_DV),
        "w_up_a": w_up_a[0].astype(BF16),
        "w_up_b": w_up_b[0].astype(BF16),
        "w_o": w_o[0].astype(BF16),
        "slopes": jnp.exp2(-8.0 * (jnp.arange(ATT_HEADS, dtype=F32) + 1.0) / ATT_HEADS),
    }
    return (_layer(x_prompt, w, final_norm_g), _layer(x_sample, w, final_norm_g))
```

```python
import functools

import jax
import jax.numpy as jnp
import numpy as np
from jax import lax
from jax.experimental import pallas as pl
from jax.experimental.pallas import tpu as pltpu

F32 = jnp.float32
BF16 = jnp.bfloat16

D_MODEL = 4096
ATT_DILATIONS = (1, 4, 16)
N_GROUPS = 3
ATT_HEAD_DIM = 128
ATT_HEADS = 8
ATT_WIDTH = ATT_HEADS * ATT_HEAD_DIM
ATT_RADIUS = 64
GLA_HEADS = 4
GLA_KEY = 1024
GLA_VAL = 2048
GLA_DK = GLA_KEY // GLA_HEADS
GLA_DV = GLA_VAL // GLA_HEADS
GLA_RANK = 16
GLA_TAU = 16.0
GLA_CHUNK = 64
NORM_EPS = 1e-6
NEG_INF = -1e30

GROUP_COLS = 3 * ATT_WIDTH
W_OFF_AZ = N_GROUPS * GROUP_COLS
W_OFF_LR = W_OFF_AZ + ATT_WIDTH + 2 * GLA_KEY + 2 * GLA_VAL
W_OFF_MG = W_OFF_LR + 2 * GLA_RANK
OFF_AZ = GROUP_COLS
OFF_GQ = OFF_AZ + ATT_WIDTH
OFF_GK = OFF_GQ + GLA_KEY
OFF_GV = OFF_GK + GLA_KEY
OFF_GZ = OFF_GV + GLA_VAL
OFF_MG = OFF_GZ + GLA_VAL
N_PROJ = OFF_MG + 2 * D_MODEL
LR_PAD = 128

VMEM_LIMIT = 56 * 1024 * 1024
BIG_VMEM_LIMIT = 62 * 1024 * 1024


def _params(sem):
    return pltpu.CompilerParams(dimension_semantics=sem, vmem_limit_bytes=VMEM_LIMIT)


NORM_ROWS = 256
NT_DIMS = (((1,), (1,)), ((), ()))


def _class_major_permutation():
    perm = np.zeros((2 * NORM_ROWS, NORM_ROWS), np.float32)
    for block, dilation in enumerate((4, 16)):
        per_class = NORM_ROWS // dilation
        for r in range(dilation):
            for u in range(per_class):
                perm[block * NORM_ROWS + r * per_class + u, u * dilation + r] = 1.0
    return jnp.asarray(perm, BF16)


def _rmsnorm_kernel(x_ref, g_ref, wlr_ref, perm_ref, h_ref, h4_ref, h16_ref, lr_ref):
    x = x_ref[0]
    ms = jnp.mean(x * x, axis=-1, keepdims=True)
    h = ((x * lax.rsqrt(ms + NORM_EPS)) * g_ref[...]).astype(h_ref.dtype)
    h_ref[0] = h
    lr_ref[0] = lax.dot_general(h, wlr_ref[...], NT_DIMS,
                                preferred_element_type=F32).astype(lr_ref.dtype)
    hp = jnp.dot(perm_ref[...], h, preferred_element_type=F32).astype(h_ref.dtype)
    for block, (out_ref, dilation) in enumerate(((h4_ref, 4), (h16_ref, 16))):
        per_class = NORM_ROWS // dilation
        for r in range(dilation):
            row0 = block * NORM_ROWS + r * per_class
            out_ref[0, r] = hp[row0:row0 + per_class]


def _rmsnorm(x, g, wt_lr):
    bs, seq, d = x.shape
    tm = NORM_ROWS

    def cls_spec(dilation):
        return pl.BlockSpec((1, dilation, tm // dilation, d), lambda b, i: (b, 0, i, 0))

    def cls_shape(dilation):
        return jax.ShapeDtypeStruct((bs, dilation, seq // dilation, d), BF16)

    return pl.pallas_call(
        _rmsnorm_kernel,
        grid=(bs, seq // tm),
        in_specs=[pl.BlockSpec((1, tm, d), lambda b, i: (b, i, 0)),
                  pl.BlockSpec((1, d), lambda b, i: (0, 0)),
                  pl.BlockSpec((LR_PAD, d), lambda b, i: (0, 0)),
                  pl.BlockSpec((2 * tm, tm), lambda b, i: (0, 0))],
        out_specs=[pl.BlockSpec((1, tm, d), lambda b, i: (b, i, 0)),
                   cls_spec(4), cls_spec(16),
                   pl.BlockSpec((1, tm, LR_PAD), lambda b, i: (b, i, 0))],
        out_shape=[jax.ShapeDtypeStruct((bs, seq, d), BF16), cls_shape(4), cls_shape(16),
                   jax.ShapeDtypeStruct((bs, seq, LR_PAD), BF16)],
        compiler_params=_params(("parallel", "parallel")),
        name="rmsnorm",
    )(x, g.reshape(1, d), wt_lr, _class_major_permutation())


IN_PROJ_TN = 1024


def _matmul_nt_kernel(a_ref, bt_ref, o_ref):
    o_ref[...] = lax.dot_general(a_ref[...], bt_ref[...].astype(BF16), NT_DIMS,
                                 preferred_element_type=F32).astype(o_ref.dtype)


def _matmul_nt(a, bt, n_out, row_start, name, tm=1024, tn=IN_PROJ_TN):
    m, k = a.shape
    return pl.pallas_call(
        _matmul_nt_kernel,
        grid=(n_out // tn, m // tm),
        in_specs=[pl.BlockSpec((tm, k), lambda j, i: (i, 0)),
                  pl.BlockSpec((pl.Element(tn), pl.Element(k)),
                               lambda j, i: (pl.multiple_of(row_start(j), 8), 0))],
        out_specs=pl.BlockSpec((tm, tn), lambda j, i: (i, j)),
        out_shape=jax.ShapeDtypeStruct((m, n_out), BF16),
        compiler_params=pltpu.CompilerParams(dimension_semantics=("parallel", "arbitrary"),
                                             vmem_limit_bytes=BIG_VMEM_LIMIT),
        name=name,
    )(a, bt)


ATT_TILE = 2048
ATT_QB = 128
ATT_KW = ATT_QB + 2 * ATT_RADIUS
ATT_WINDOW_OFFSETS = (-ATT_RADIUS, 0, -2 * ATT_RADIUS)


def _attn_kernel(slopes_ref, q1_ref, k1_ref, v1_ref, q2_ref, k2_ref, v2_ref,
                 q3_ref, k3_ref, v3_ref, z_ref, y_ref, o_scr, l_scr, bias_scr, *, seq):
    h = pl.program_id(1)
    tile = pl.program_id(2) if seq > ATT_TILE else 0
    slope = slopes_ref[h]
    scale = ATT_HEAD_DIM ** -0.5

    @pl.when(tile == 0)
    def _():
        col_minus_row = (lax.broadcasted_iota(jnp.int32, (ATT_QB, ATT_KW), 1)
                         - lax.broadcasted_iota(jnp.int32, (ATT_QB, ATT_KW), 0))
        col = lax.broadcasted_iota(jnp.int32, (ATT_QB, ATT_KW), 1)
        for g, dilation in enumerate(ATT_DILATIONS):
            if seq // dilation >= ATT_KW:
                valid = [jnp.abs(col_minus_row + off) for off in ATT_WINDOW_OFFSETS]
            else:
                valid = [jnp.where(col // ATT_QB == c, jnp.abs(col_minus_row - c * ATT_QB),
                                   ATT_KW) for c in range(2)]
            for v, dist in enumerate(valid):
                bias_scr[g * len(ATT_WINDOW_OFFSETS) + v] = jnp.where(
                    dist <= ATT_RADIUS, -(slope * float(dilation)) * dist.astype(F32),
                    NEG_INF)

    for g, (q_ref, k_ref, v_ref) in enumerate(((q1_ref, k1_ref, v1_ref),
                                               (q2_ref, k2_ref, v2_ref),
                                               (q3_ref, k3_ref, v3_ref))):
        dilation = ATT_DILATIONS[g]
        length = seq // dilation
        per_class = ATT_TILE // dilation // ATT_QB
        paired = length < ATT_KW
        for r in range(dilation):
            for j in range(per_class):
                row0 = j * ATT_QB
                if paired:
                    kv_class, variant, w0 = r // 2, r % 2, 0
                else:
                    p0 = tile * (ATT_TILE // dilation) + row0
                    if isinstance(p0, int):
                        w0 = min(max(p0 - ATT_RADIUS, 0), length - ATT_KW)
                        variant = ATT_WINDOW_OFFSETS.index(w0 - p0)
                    else:
                        w0 = pl.multiple_of(jnp.clip(p0 - ATT_RADIUS, 0, length - ATT_KW),
                                            ATT_RADIUS)
                        off = w0 - p0
                        variant = jnp.where(off == ATT_WINDOW_OFFSETS[1], 1,
                                            jnp.where(off == ATT_WINDOW_OFFSETS[2], 2, 0))
                    kv_class = r
                bias = bias_scr[g * len(ATT_WINDOW_OFFSETS) + variant]
                q = q_ref[0, r, row0:row0 + ATT_QB, :]
                s = lax.dot_general(q, k_ref[0, kv_class, pl.ds(w0, ATT_KW), :], NT_DIMS,
                                    preferred_element_type=F32) * scale + bias
                m = jnp.max(s, axis=-1, keepdims=True)
                e = jnp.exp(s - m)
                l = jnp.sum(e, axis=-1, keepdims=True)
                o = jnp.dot(e.astype(BF16), v_ref[0, kv_class, pl.ds(w0, ATT_KW), :],
                            preferred_element_type=F32) / l
                lse = jnp.broadcast_to(m + jnp.log(l), (ATT_QB, ATT_HEAD_DIM))
                if dilation == 1:
                    rows = pl.ds(row0, ATT_QB)
                else:
                    rows = pl.ds(row0 * dilation + r, ATT_QB, stride=dilation)
                o_scr[g, rows, :] = o
                l_scr[g, rows, :] = lse

    la, lb, lc = l_scr[0], l_scr[1], l_scr[2]
    m = jnp.maximum(jnp.maximum(la, lb), lc)
    ea, eb, ec = jnp.exp(la - m), jnp.exp(lb - m), jnp.exp(lc - m)
    mix = (ea * o_scr[0] + eb * o_scr[1] + ec * o_scr[2]) / (ea + eb + ec)
    z = z_ref[0].astype(F32)
    y_ref[0] = (mix * (z / (1.0 + jnp.exp(-z)))).astype(y_ref.dtype)


def _attention(slopes, proj3, cls4, cls16):
    bs, seq, _ = proj3.shape
    n_tiles = seq // ATT_TILE
    nat = proj3.reshape(bs, 1, seq, N_PROJ)

    def kv_shape(dilation):
        length = seq // dilation
        return (dilation, length) if length >= ATT_KW else (dilation // 2, 2 * length)

    def specs(dilation):
        tq = ATT_TILE // dilation
        q = pl.BlockSpec((1, dilation, tq, ATT_HEAD_DIM), lambda b, h, t: (b, 0, t, h))
        k = pl.BlockSpec((1,) + kv_shape(dilation) + (ATT_HEAD_DIM,),
                         lambda b, h, t: (b, 0, 0, ATT_HEADS + h))
        v = pl.BlockSpec((1,) + kv_shape(dilation) + (ATT_HEAD_DIM,),
                         lambda b, h, t: (b, 0, 0, 2 * ATT_HEADS + h))
        return [q, k, v]

    def kv_view(t, dilation):
        return t.reshape((bs,) + kv_shape(dilation) + (t.shape[-1],))

    tok = pl.BlockSpec((1, ATT_TILE, ATT_HEAD_DIM),
                       lambda b, h, t: (b, t, OFF_AZ // ATT_HEAD_DIM + h))
    return pl.pallas_call(
        functools.partial(_attn_kernel, seq=seq),
        grid=(bs, ATT_HEADS, n_tiles),
        in_specs=([pl.BlockSpec(memory_space=pltpu.SMEM)]
                  + specs(1) + specs(4) + specs(16) + [tok]),
        out_specs=pl.BlockSpec((1, ATT_TILE, ATT_HEAD_DIM), lambda b, h, t: (b, t, h)),
        out_shape=jax.ShapeDtypeStruct((bs, seq, ATT_WIDTH), BF16),
        scratch_shapes=[pltpu.VMEM((N_GROUPS, ATT_TILE, ATT_HEAD_DIM), F32),
                        pltpu.VMEM((N_GROUPS, ATT_TILE, ATT_HEAD_DIM), F32),
                        pltpu.VMEM((N_GROUPS * len(ATT_WINDOW_OFFSETS), ATT_QB, ATT_KW),
                                   F32)],
        compiler_params=_params(("parallel", "parallel", "arbitrary")),
        name="attention",
    )(slopes, nat, nat, nat, cls4, kv_view(cls4, 4), kv_view(cls4, 4),
      cls16, kv_view(cls16, 16), kv_view(cls16, 16), proj3)


GLA_BLOCK_CHUNKS = 4
GLA_BLOCK = GLA_BLOCK_CHUNKS * GLA_CHUNK
GLA_TILE_BLOCKS = 2
GLA_TILE = GLA_TILE_BLOCKS * GLA_BLOCK
GLA_STEP_HEADS = 4


def _gla_kernel(*refs, reverse):
    if reverse:
        (q_ref, k_ref, v_ref, z_ref, w2_ref, b2_ref, of_ref, gz_ref, g_ref,
         y_ref, st_ref, qe_scr, kd_scr, ke_scr) = refs
    else:
        (q_ref, k_ref, v_ref, z_ref, w2_ref, b2_ref,
         y_ref, st_ref, qe_scr, kd_scr, ke_scr) = refs

    @pl.when(pl.program_id(2) == 0)
    def _():
        st_ref[...] = jnp.zeros_like(st_ref)

    nc, ch = GLA_BLOCK_CHUNKS, GLA_CHUNK
    heads = range(GLA_STEP_HEADS)
    ri = lax.broadcasted_iota(jnp.int32, (GLA_BLOCK, GLA_BLOCK), 0)
    ci = lax.broadcasted_iota(jnp.int32, (GLA_BLOCK, GLA_BLOCK), 1)
    same_chunk = (ri // ch) == (ci // ch)
    if reverse:
        cum = (same_chunk & (ci >= ri)).astype(BF16)
        keep = (ci // ch > ri // ch) | (same_chunk & (ci > ri))
        end_row = 0
        order = list(range(nc - 1, -1, -1))
    else:
        cum = (same_chunk & (ci <= ri)).astype(BF16)
        keep = (ci // ch < ri // ch) | (same_chunk & (ci <= ri))
        end_row = ch - 1
        order = list(range(nc))
    pos = {c: p for p, c in enumerate(order)}

    def chunk(c):
        return slice(c * ch, (c + 1) * ch)

    blocks = range(GLA_TILE_BLOCKS)
    for blk in (reversed(blocks) if reverse else blocks):
        r0 = blk * GLA_BLOCK
        rows = slice(r0, r0 + GLA_BLOCK)
        x = jnp.dot(z_ref[0, rows, :], w2_ref[...],
                    preferred_element_type=F32) + b2_ref[...]
        la = (jnp.minimum(x, 0.0) - jnp.log(1.0 + jnp.exp(-jnp.abs(x)))) * (1.0 / GLA_TAU)
        hi = la.astype(BF16)
        lo = (la - hi.astype(F32)).astype(BF16)
        b_all = (jnp.dot(cum, hi, preferred_element_type=F32)
                 + jnp.dot(cum, lo, preferred_element_type=F32))
        total = []
        for hd in heads:
            cols = slice(hd * GLA_DK, (hd + 1) * GLA_DK)
            b = b_all[:, cols]
            ends = [b[c * ch + end_row:c * ch + end_row + 1, :] for c in range(nc)]
            tot = [jnp.zeros_like(ends[0])]
            for p in range(nc):
                tot.append(tot[-1] + ends[order[p]])
            total.append(tot)
            e_end = jnp.concatenate(
                [jnp.broadcast_to(jnp.exp(e), (ch, GLA_DK)) for e in ends], axis=0)
            ke = k_ref[0, rows, cols].astype(F32) * jnp.exp(-b)
            qe_scr[hd, blk] = (q_ref[0, rows, cols].astype(F32) * (GLA_DK ** -0.5)
                               * jnp.exp(b))
            ke_scr[hd, blk] = ke.astype(BF16)
            kd_scr[hd, blk] = ke * e_end
        att = []
        for hd in heads:
            tot = total[hd]
            att_rows = [None] * nc
            for p in range(nc):
                a = order[p]
                parts = []
                for c in range(nc):
                    if c == a:
                        parts.append(ke_scr[hd, blk, chunk(c), :])
                    elif pos[c] < p:
                        kd_c = kd_scr[hd, blk, chunk(c), :]
                        if pos[c] < p - 1:
                            kd_c = kd_c * jnp.exp(tot[p] - tot[pos[c] + 1])
                        parts.append(kd_c.astype(BF16))
                    else:
                        parts.append(jnp.zeros((ch, GLA_DK), BF16))
                att_rows[a] = lax.dot_general(
                    qe_scr[hd, blk, chunk(a), :].astype(BF16),
                    jnp.concatenate(parts, axis=0), NT_DIMS, preferred_element_type=F32)
            att.append(jnp.where(keep, jnp.concatenate(att_rows, axis=0), 0.0).astype(BF16))
        for hd in heads:
            tot = total[hd]
            vcols = slice(hd * GLA_DV, (hd + 1) * GLA_DV)
            st = st_ref[hd]
            qe_in = jnp.concatenate(
                [(qe_scr[hd, blk, chunk(c), :] * jnp.exp(tot[pos[c]])).astype(BF16)
                 for c in range(nc)], axis=0)
            v = v_ref[0, rows, vcols]
            o = (jnp.dot(att[hd], v, preferred_element_type=F32)
                 + lax.dot_general(qe_in, st.astype(BF16), NT_DIMS,
                                   preferred_element_type=F32))
            if reverse:
                o = o + of_ref[0, rows, vcols]
                ms = jnp.mean(o * o, axis=-1, keepdims=True)
                o = o * lax.rsqrt(ms + NORM_EPS) * g_ref[...]
                gz = gz_ref[0, rows, vcols].astype(F32)
                y_ref[0, rows, vcols] = (o * (gz / (1.0 + jnp.exp(-gz)))).astype(y_ref.dtype)
            else:
                y_ref[0, rows, vcols] = o
            k_all = jnp.concatenate(
                [(kd_scr[hd, blk, chunk(c), :]
                  * jnp.exp(tot[nc] - tot[pos[c] + 1])).astype(BF16)
                 for c in range(nc)], axis=0)
            vt = v.astype(F32).T.astype(BF16)
            st_ref[hd] = (st * jnp.exp(tot[nc])
                          + jnp.dot(vt, k_all, preferred_element_type=F32))


def _gla_direction(proj3, lr3, w2pad, bias, reverse, o_fwd=None, norm_g=None):
    bs, seq, _ = proj3.shape
    nt = seq // GLA_TILE
    nh = GLA_STEP_HEADS

    def tile(t):
        return nt - 1 - t if reverse else t

    def spec(width, off):
        return pl.BlockSpec((1, GLA_TILE, nh * width),
                            lambda b, h, t: (b, tile(t), off // (nh * width) + h))

    in_specs = [spec(GLA_DK, OFF_GQ), spec(GLA_DK, OFF_GK), spec(GLA_DV, OFF_GV),
                pl.BlockSpec((1, GLA_TILE, LR_PAD), lambda b, h, t: (b, tile(t), 0)),
                pl.BlockSpec((LR_PAD, nh * GLA_DK), lambda b, h, t: (0, h)),
                pl.BlockSpec((1, nh * GLA_DK), lambda b, h, t: (0, h))]
    args = [proj3, proj3, proj3, lr3, w2pad, bias]
    out_spec = pl.BlockSpec((1, GLA_TILE, nh * GLA_DV), lambda b, h, t: (b, tile(t), h))
    if reverse:
        in_specs += [out_spec, spec(GLA_DV, OFF_GZ),
                     pl.BlockSpec((1, GLA_DV), lambda b, h, t: (0, 0))]
        args += [o_fwd, proj3, norm_g]
    block_scratch = (nh, GLA_TILE_BLOCKS, GLA_BLOCK, GLA_DK)
    return pl.pallas_call(
        functools.partial(_gla_kernel, reverse=reverse),
        grid=(bs, GLA_HEADS // nh, nt),
        in_specs=in_specs,
        out_specs=out_spec,
        out_shape=jax.ShapeDtypeStruct((bs, seq, GLA_VAL), BF16 if reverse else F32),
        scratch_shapes=[pltpu.VMEM((nh, GLA_DV, GLA_DK), F32),
                        pltpu.VMEM(block_scratch, F32),
                        pltpu.VMEM(block_scratch, F32),
                        pltpu.VMEM(block_scratch, BF16)],
        compiler_params=_params(("parallel", "parallel", "arbitrary")),
        name="gla_bwd" if reverse else "gla_fwd",
    )(*args)


def _merge_kernel(ya_ref, yb_ref, wa_ref, wb_ref, ga_ref, gb_ref, o_ref):
    ua = jnp.dot(ya_ref[...], wa_ref[...].astype(BF16), preferred_element_type=F32)
    ub = jnp.dot(yb_ref[...], wb_ref[...].astype(BF16), preferred_element_type=F32)
    ga = ga_ref[...].astype(F32)
    gb = gb_ref[...].astype(F32)
    merged = ua / (1.0 + jnp.exp(-ga)) + ub / (1.0 + jnp.exp(-gb))
    o_ref[...] = merged.astype(o_ref.dtype)


def _merge(ya, yb, wa, wb, proj, tm=1024, tn=1024):
    m = ya.shape[0]
    return pl.pallas_call(
        _merge_kernel,
        grid=(m // tm, D_MODEL // tn),
        in_specs=[pl.BlockSpec((tm, ATT_WIDTH), lambda i, j: (i, 0)),
                  pl.BlockSpec((tm, GLA_VAL), lambda i, j: (i, 0)),
                  pl.BlockSpec((ATT_WIDTH, tn), lambda i, j: (0, j)),
                  pl.BlockSpec((GLA_VAL, tn), lambda i, j: (0, j)),
                  pl.BlockSpec((tm, tn), lambda i, j: (i, OFF_MG // tn + j)),
                  pl.BlockSpec((tm, tn), lambda i, j: (i, (OFF_MG + D_MODEL) // tn + j))],
        out_specs=pl.BlockSpec((tm, tn), lambda i, j: (i, j)),
        out_shape=jax.ShapeDtypeStruct((m, D_MODEL), BF16),
        compiler_params=pltpu.CompilerParams(dimension_semantics=("parallel", "arbitrary"),
                                             vmem_limit_bytes=BIG_VMEM_LIMIT),
        name="merge",
    )(ya, yb, wa, wb, proj, proj)


def _out_kernel(x_ref, a_ref, w_ref, g_ref, o_ref):
    u = x_ref[...] + jnp.dot(a_ref[...], w_ref[...], preferred_element_type=F32)
    ms = jnp.mean(u * u, axis=-1, keepdims=True)
    o_ref[...] = (u * lax.rsqrt(ms + NORM_EPS)) * g_ref[...]


def _out_proj(x, merged, w_o, g, tm=256):
    m = x.shape[0]
    return pl.pallas_call(
        _out_kernel,
        grid=(m // tm,),
        in_specs=[pl.BlockSpec((tm, D_MODEL), lambda i: (i, 0)),
                  pl.BlockSpec((tm, D_MODEL), lambda i: (i, 0)),
                  pl.BlockSpec((D_MODEL, D_MODEL), lambda i: (0, 0),
                               pipeline_mode=pl.Buffered(1)),
                  pl.BlockSpec((1, D_MODEL), lambda i: (0, 0))],
        out_specs=pl.BlockSpec((tm, D_MODEL), lambda i: (i, 0)),
        out_shape=jax.ShapeDtypeStruct((m, D_MODEL), F32),
        compiler_params=pltpu.CompilerParams(dimension_semantics=("parallel",),
                                             vmem_limit_bytes=BIG_VMEM_LIMIT),
        name="out_proj",
    )(x, merged, w_o, g.reshape(1, D_MODEL))


def _layer(x, w, final_g):
    bs, seq, d = x.shape
    m = bs * seq
    x2 = x.reshape(m, d)
    h, h4, h16, lr3 = _rmsnorm(x, w["norm_g"], w["wt_lr"])
    tn = IN_PROJ_TN
    first_tiles, main_tiles = GROUP_COLS // tn, OFF_MG // tn

    def token_order_rows(j):
        return jnp.where(j < first_tiles, j * tn,
                         jnp.where(j < main_tiles, j * tn + (N_GROUPS - 1) * GROUP_COLS,
                                   W_OFF_MG + (j - main_tiles) * tn))

    proj = _matmul_nt(h.reshape(m, d), w["wt"], N_PROJ, token_order_rows, "in_proj")
    proj3 = proj.reshape(bs, seq, N_PROJ)
    cls = []
    for g, (dilation, hc) in enumerate(((4, h4), (16, h16)), start=1):
        p = _matmul_nt(hc.reshape(m, d), w["wt"], GROUP_COLS,
                       lambda j, g=g: j * tn + g * GROUP_COLS, f"in_proj_d{dilation}")
        cls.append(p.reshape(bs, dilation, seq // dilation, GROUP_COLS))
    ya = _attention(w["slopes"], proj3, *cls)

    o_f = _gla_direction(proj3, lr3, w["w2_f"], w["b_f"], False)
    yb = _gla_direction(proj3, lr3, w["w2_b"], w["b_b"], True, o_f, w["gla_norm_g"])

    merged = _merge(ya.reshape(m, ATT_WIDTH), yb.reshape(m, GLA_VAL),
                    w["w_up_a"], w["w_up_b"], proj)
    y = _out_proj(x2, merged, w["w_o"], final_g)
    return y.reshape(bs, seq, d)


def _pad_rows(w2, row0):
    return jnp.zeros((LR_PAD, GLA_KEY), BF16).at[row0:row0 + GLA_RANK].set(w2.astype(BF16))


def kernel(x_prompt, x_sample, norm_g, w_in, gla_w2_f, gla_b_f, gla_w2_b, gla_b_b,
           gla_norm_g, w_up_a, w_up_b, w_o, final_norm_g):
    assert norm_g.shape[0] == 1, "single-layer kernel"
    wt = w_in[0].T
    w = {
        "norm_g": norm_g[0],
        "wt": wt,
        "wt_lr": jnp.pad(wt[W_OFF_LR:W_OFF_MG],
                         ((0, LR_PAD - 2 * GLA_RANK), (0, 0))).astype(BF16),
        "w2_f": _pad_rows(gla_w2_f[0], 0),
        "w2_b": _pad_rows(gla_w2_b[0], GLA_RANK),
        "b_f": gla_b_f[0].reshape(1, GLA_KEY),
        "b_b": gla_b_b[0].reshape(1, GLA_KEY),
        "gla_norm_g": gla_norm_g[0].reshape(1, GLA_DV),
        "w_up_a": w_up_a[0],
        "w_up_b": w_up_b[0],
        "w_o": w_o[0].astype(BF16),
        "slopes": jnp.exp2(-8.0 * (jnp.arange(ATT_HEADS, dtype=F32) + 1.0) / ATT_HEADS),
    }
    return (_layer(x_prompt, w, final_norm_g), _layer(x_sample, w, final_norm_g))
```

```python
import functools

import jax
import jax.numpy as jnp
import numpy as np
from jax import lax
from jax.experimental import pallas as pl
from jax.experimental.pallas import tpu as pltpu

F32 = jnp.float32
BF16 = jnp.bfloat16

D_MODEL = 4096
ATT_DILATIONS = (1, 4, 16)
N_GROUPS = 3
ATT_HEAD_DIM = 128
ATT_HEADS = 8
ATT_WIDTH = ATT_HEADS * ATT_HEAD_DIM
ATT_RADIUS = 64
GLA_HEADS = 4
GLA_KEY = 1024
GLA_VAL = 2048
GLA_DK = GLA_KEY // GLA_HEADS
GLA_DV = GLA_VAL // GLA_HEADS
GLA_RANK = 16
GLA_TAU = 16.0
GLA_CHUNK = 64
NORM_EPS = 1e-6
NEG_INF = -1e30

GROUP_COLS = 3 * ATT_WIDTH
W_OFF_AZ = N_GROUPS * GROUP_COLS
W_OFF_LR = W_OFF_AZ + ATT_WIDTH + 2 * GLA_KEY + 2 * GLA_VAL
W_OFF_MG = W_OFF_LR + 2 * GLA_RANK
OFF_AZ = GROUP_COLS
OFF_GQ = OFF_AZ + ATT_WIDTH
OFF_GK = OFF_GQ + GLA_KEY
OFF_GV = OFF_GK + GLA_KEY
OFF_GZ = OFF_GV + GLA_VAL
OFF_MG = OFF_GZ + GLA_VAL
N_PROJ = OFF_MG + 2 * D_MODEL
LR_PAD = 128

VMEM_LIMIT = 56 * 1024 * 1024
BIG_VMEM_LIMIT = 62 * 1024 * 1024


def _params(sem):
    return pltpu.CompilerParams(dimension_semantics=sem, vmem_limit_bytes=VMEM_LIMIT)


NORM_ROWS = 256
NT_DIMS = (((1,), (1,)), ((), ()))


def _class_major_permutation():
    perm = np.zeros((2 * NORM_ROWS, NORM_ROWS), np.float32)
    for block, dilation in enumerate((4, 16)):
        per_class = NORM_ROWS // dilation
        for r in range(dilation):
            for u in range(per_class):
                perm[block * NORM_ROWS + r * per_class + u, u * dilation + r] = 1.0
    return jnp.asarray(perm, BF16)


def _rmsnorm_kernel(x_ref, g_ref, wlr_ref, perm_ref, h_ref, h4_ref, h16_ref, lr_ref):
    x = x_ref[0]
    ms = jnp.mean(x * x, axis=-1, keepdims=True)
    h = ((x * lax.rsqrt(ms + NORM_EPS)) * g_ref[...]).astype(h_ref.dtype)
    h_ref[0] = h
    lr_ref[0] = lax.dot_general(h, wlr_ref[...], NT_DIMS,
                                preferred_element_type=F32).astype(lr_ref.dtype)
    hp = jnp.dot(perm_ref[...], h, preferred_element_type=F32).astype(h_ref.dtype)
    for block, (out_ref, dilation) in enumerate(((h4_ref, 4), (h16_ref, 16))):
        per_class = NORM_ROWS // dilation
        for r in range(dilation):
            row0 = block * NORM_ROWS + r * per_class
            out_ref[0, r] = hp[row0:row0 + per_class]


def _rmsnorm(x, g, wt_lr):
    bs, seq, d = x.shape
    tm = NORM_ROWS

    def cls_spec(dilation):
        return pl.BlockSpec((1, dilation, tm // dilation, d), lambda b, i: (b, 0, i, 0))

    def cls_shape(dilation):
        return jax.ShapeDtypeStruct((bs, dilation, seq // dilation, d), BF16)

    return pl.pallas_call(
        _rmsnorm_kernel,
        grid=(bs, seq // tm),
        in_specs=[pl.BlockSpec((1, tm, d), lambda b, i: (b, i, 0)),
                  pl.BlockSpec((1, d), lambda b, i: (0, 0)),
                  pl.BlockSpec((LR_PAD, d), lambda b, i: (0, 0)),
                  pl.BlockSpec((2 * tm, tm), lambda b, i: (0, 0))],
        out_specs=[pl.BlockSpec((1, tm, d), lambda b, i: (b, i, 0)),
                   cls_spec(4), cls_spec(16),
                   pl.BlockSpec((1, tm, LR_PAD), lambda b, i: (b, i, 0))],
        out_shape=[jax.ShapeDtypeStruct((bs, seq, d), BF16), cls_shape(4), cls_shape(16),
                   jax.ShapeDtypeStruct((bs, seq, LR_PAD), BF16)],
        compiler_params=_params(("parallel", "parallel")),
        name="rmsnorm",
    )(x, g.reshape(1, d), wt_lr, _class_major_permutation())


IN_PROJ_TN = 1024


def _matmul_nt_kernel(a_ref, bt_ref, o_ref):
    o_ref[...] = lax.dot_general(a_ref[...], bt_ref[...].astype(BF16), NT_DIMS,
                                 preferred_element_type=F32).astype(o_ref.dtype)


def _matmul_nt_cast_kernel(a_ref, bt_ref, c_ref, o_ref, co_ref):
    _matmul_nt_kernel(a_ref, bt_ref, o_ref)
    co_ref[...] = c_ref[...].astype(co_ref.dtype)


CAST_ROWS = 32


def _matmul_nt(a, bt, n_out, row_start, name, tm=1024, tn=IN_PROJ_TN, cast=None):
    m, k = a.shape
    n_i = m // tm
    in_specs = [pl.BlockSpec((tm, k), lambda j, i: (i, 0)),
                pl.BlockSpec((pl.Element(tn), pl.Element(k)),
                             lambda j, i: (pl.multiple_of(row_start(j), 8), 0))]
    out_specs = pl.BlockSpec((tm, tn), lambda j, i: (i, j))
    out_shape = jax.ShapeDtypeStruct((m, n_out), BF16)
    body, operands = _matmul_nt_kernel, (a, bt)
    if cast is not None:
        n_blocks = cast.shape[0] // CAST_ROWS
        assert n_blocks <= (n_out // tn) * n_i, "not enough grid steps to cast every row"

        def cast_block(j, i):
            return jnp.minimum(j * n_i + i, n_blocks - 1), 0

        cast_spec = pl.BlockSpec((CAST_ROWS, cast.shape[1]), cast_block)
        in_specs.append(cast_spec)
        out_specs = [out_specs, cast_spec]
        out_shape = [out_shape, jax.ShapeDtypeStruct(cast.shape, BF16)]
        body, operands = _matmul_nt_cast_kernel, (a, bt, cast)
    return pl.pallas_call(
        body,
        grid=(n_out // tn, n_i),
        in_specs=in_specs,
        out_specs=out_specs,
        out_shape=out_shape,
        compiler_params=pltpu.CompilerParams(dimension_semantics=("parallel", "arbitrary"),
                                             vmem_limit_bytes=BIG_VMEM_LIMIT),
        name=name,
    )(*operands)


ATT_TILE = 2048
ATT_QB = 128
ATT_KW = ATT_QB + 2 * ATT_RADIUS
ATT_WINDOW_OFFSETS = (-ATT_RADIUS, 0, -2 * ATT_RADIUS)


def _attn_kernel(slopes_ref, q1_ref, k1_ref, v1_ref, q2_ref, k2_ref, v2_ref,
                 q3_ref, k3_ref, v3_ref, z_ref, y_ref, o_scr, l_scr, bias_scr, *, seq):
    h = pl.program_id(1)
    tile = pl.program_id(2)
    slope = slopes_ref[h]
    scale = ATT_HEAD_DIM ** -0.5

    @pl.when(tile == 0)
    def _():
        col_minus_row = (lax.broadcasted_iota(jnp.int32, (ATT_QB, ATT_KW), 1)
                         - lax.broadcasted_iota(jnp.int32, (ATT_QB, ATT_KW), 0))
        col = lax.broadcasted_iota(jnp.int32, (ATT_QB, ATT_KW), 1)
        for g, dilation in enumerate(ATT_DILATIONS):
            if seq // dilation >= ATT_KW:
                valid = [jnp.abs(col_minus_row + off) for off in ATT_WINDOW_OFFSETS]
            else:
                valid = [jnp.where(col // ATT_QB == c, jnp.abs(col_minus_row - c * ATT_QB),
                                   ATT_KW) for c in range(2)]
            for v, dist in enumerate(valid):
                bias_scr[g * len(ATT_WINDOW_OFFSETS) + v] = jnp.where(
                    dist <= ATT_RADIUS, -(slope * float(dilation)) * dist.astype(F32),
                    NEG_INF)

    for g, (q_ref, k_ref, v_ref) in enumerate(((q1_ref, k1_ref, v1_ref),
                                               (q2_ref, k2_ref, v2_ref),
                                               (q3_ref, k3_ref, v3_ref))):
        dilation = ATT_DILATIONS[g]
        length = seq // dilation
        per_class = ATT_TILE // dilation // ATT_QB
        paired = length < ATT_KW
        for r in range(dilation):
            for j in range(per_class):
                row0 = j * ATT_QB
                if paired:
                    kv_class, variant, w0 = r // 2, r % 2, 0
                else:
                    p0 = tile * (ATT_TILE // dilation) + row0
                    w0 = pl.multiple_of(jnp.clip(p0 - ATT_RADIUS, 0, length - ATT_KW),
                                        ATT_RADIUS)
                    off = w0 - p0
                    variant = jnp.where(off == ATT_WINDOW_OFFSETS[1], 1,
                                        jnp.where(off == ATT_WINDOW_OFFSETS[2], 2, 0))
                    kv_class = r
                bias = bias_scr[g * len(ATT_WINDOW_OFFSETS) + variant]
                q = q_ref[0, r, row0:row0 + ATT_QB, :]
                s = lax.dot_general(q, k_ref[0, kv_class, pl.ds(w0, ATT_KW), :], NT_DIMS,
                                    preferred_element_type=F32) * scale + bias
                m = jnp.max(s, axis=-1, keepdims=True)
                e = jnp.exp(s - m)
                l = jnp.sum(e, axis=-1, keepdims=True)
                o = jnp.dot(e.astype(BF16), v_ref[0, kv_class, pl.ds(w0, ATT_KW), :],
                            preferred_element_type=F32) / l
                lse = jnp.broadcast_to(m + jnp.log(l), (ATT_QB, ATT_HEAD_DIM))
                if dilation == 1:
                    rows = pl.ds(row0, ATT_QB)
                else:
                    rows = pl.ds(row0 * dilation + r, ATT_QB, stride=dilation)
                o_scr[g, rows, :] = o
                l_scr[g, rows, :] = lse

    la, lb, lc = l_scr[0], l_scr[1], l_scr[2]
    m = jnp.maximum(jnp.maximum(la, lb), lc)
    ea, eb, ec = jnp.exp(la - m), jnp.exp(lb - m), jnp.exp(lc - m)
    mix = (ea * o_scr[0] + eb * o_scr[1] + ec * o_scr[2]) / (ea + eb + ec)
    z = z_ref[0].astype(F32)
    y_ref[0] = (mix * (z / (1.0 + jnp.exp(-z)))).astype(y_ref.dtype)


def _attention(slopes, proj3, cls4, cls16):
    bs, seq, _ = proj3.shape
    n_tiles = seq // ATT_TILE
    nat = proj3.reshape(bs, 1, seq, N_PROJ)

    def kv_shape(dilation):
        length = seq // dilation
        return (dilation, length) if length >= ATT_KW else (dilation // 2, 2 * length)

    def specs(dilation):
        tq = ATT_TILE // dilation
        q = pl.BlockSpec((1, dilation, tq, ATT_HEAD_DIM), lambda b, h, t: (b, 0, t, h))
        k = pl.BlockSpec((1,) + kv_shape(dilation) + (ATT_HEAD_DIM,),
                         lambda b, h, t: (b, 0, 0, ATT_HEADS + h))
        v = pl.BlockSpec((1,) + kv_shape(dilation) + (ATT_HEAD_DIM,),
                         lambda b, h, t: (b, 0, 0, 2 * ATT_HEADS + h))
        return [q, k, v]

    def kv_view(t, dilation):
        return t.reshape((bs,) + kv_shape(dilation) + (t.shape[-1],))

    tok = pl.BlockSpec((1, ATT_TILE, ATT_HEAD_DIM),
                       lambda b, h, t: (b, t, OFF_AZ // ATT_HEAD_DIM + h))
    return pl.pallas_call(
        functools.partial(_attn_kernel, seq=seq),
        grid=(bs, ATT_HEADS, n_tiles),
        in_specs=([pl.BlockSpec(memory_space=pltpu.SMEM)]
                  + specs(1) + specs(4) + specs(16) + [tok]),
        out_specs=pl.BlockSpec((1, ATT_TILE, ATT_HEAD_DIM), lambda b, h, t: (b, t, h)),
        out_shape=jax.ShapeDtypeStruct((bs, seq, ATT_WIDTH), BF16),
        scratch_shapes=[pltpu.VMEM((N_GROUPS, ATT_TILE, ATT_HEAD_DIM), F32),
                        pltpu.VMEM((N_GROUPS, ATT_TILE, ATT_HEAD_DIM), F32),
                        pltpu.VMEM((N_GROUPS * len(ATT_WINDOW_OFFSETS), ATT_QB, ATT_KW),
                                   F32)],
        compiler_params=_params(("parallel", "parallel", "arbitrary")),
        name="attention",
    )(slopes, nat, nat, nat, cls4, kv_view(cls4, 4), kv_view(cls4, 4),
      cls16, kv_view(cls16, 16), kv_view(cls16, 16), proj3)


GLA_BLOCK_CHUNKS = 4
GLA_BLOCK = GLA_BLOCK_CHUNKS * GLA_CHUNK
GLA_TILE_BLOCKS = 2
GLA_TILE = GLA_TILE_BLOCKS * GLA_BLOCK
GLA_STEP_HEADS = 4


def _gla_kernel(*refs, reverse):
    if reverse:
        (q_ref, k_ref, v_ref, z_ref, w2_ref, b2_ref, of_ref, gz_ref, g_ref,
         y_ref, st_ref, qe_scr, kd_scr, ke_scr) = refs
    else:
        (q_ref, k_ref, v_ref, z_ref, w2_ref, b2_ref,
         y_ref, st_ref, qe_scr, kd_scr, ke_scr) = refs

    @pl.when(pl.program_id(2) == 0)
    def _():
        st_ref[...] = jnp.zeros_like(st_ref)

    nc, ch = GLA_BLOCK_CHUNKS, GLA_CHUNK
    heads = range(GLA_STEP_HEADS)
    ri = lax.broadcasted_iota(jnp.int32, (GLA_BLOCK, GLA_BLOCK), 0)
    ci = lax.broadcasted_iota(jnp.int32, (GLA_BLOCK, GLA_BLOCK), 1)
    same_chunk = (ri // ch) == (ci // ch)
    if reverse:
        cum = (same_chunk & (ci >= ri)).astype(BF16)
        keep = (ci // ch > ri // ch) | (same_chunk & (ci > ri))
        end_row = 0
        order = list(range(nc - 1, -1, -1))
    else:
        cum = (same_chunk & (ci <= ri)).astype(BF16)
        keep = (ci // ch < ri // ch) | (same_chunk & (ci <= ri))
        end_row = ch - 1
        order = list(range(nc))
    pos = {c: p for p, c in enumerate(order)}

    def chunk(c):
        return slice(c * ch, (c + 1) * ch)

    blocks = range(GLA_TILE_BLOCKS)
    for blk in (reversed(blocks) if reverse else blocks):
        r0 = blk * GLA_BLOCK
        rows = slice(r0, r0 + GLA_BLOCK)
        x = jnp.dot(z_ref[0, rows, :], w2_ref[...],
                    preferred_element_type=F32) + b2_ref[...]
        la = (jnp.minimum(x, 0.0) - jnp.log(1.0 + jnp.exp(-jnp.abs(x)))) * (1.0 / GLA_TAU)
        hi = la.astype(BF16)
        lo = (la - hi.astype(F32)).astype(BF16)
        b_all = (jnp.dot(cum, hi, preferred_element_type=F32)
                 + jnp.dot(cum, lo, preferred_element_type=F32))
        total = []
        for hd in heads:
            cols = slice(hd * GLA_DK, (hd + 1) * GLA_DK)
            b = b_all[:, cols]
            ends = [b[c * ch + end_row:c * ch + end_row + 1, :] for c in range(nc)]
            tot = [jnp.zeros_like(ends[0])]
            for p in range(nc):
                tot.append(tot[-1] + ends[order[p]])
            total.append(tot)
            e_end = jnp.concatenate(
                [jnp.broadcast_to(jnp.exp(e), (ch, GLA_DK)) for e in ends], axis=0)
            ke = k_ref[0, rows, cols].astype(F32) * jnp.exp(-b)
            qe_scr[hd, blk] = (q_ref[0, rows, cols].astype(F32) * (GLA_DK ** -0.5)
                               * jnp.exp(b))
            ke_scr[hd, blk] = ke.astype(BF16)
            kd_scr[hd, blk] = ke * e_end
        att = []
        for hd in heads:
            tot = total[hd]
            att_rows = [None] * nc
            for p in range(nc):
                a = order[p]
                parts = []
                for c in range(nc):
                    if c == a:
                        parts.append(ke_scr[hd, blk, chunk(c), :])
                    elif pos[c] < p:
                        kd_c = kd_scr[hd, blk, chunk(c), :]
                        if pos[c] < p - 1:
                            kd_c = kd_c * jnp.exp(tot[p] - tot[pos[c] + 1])
                        parts.append(kd_c.astype(BF16))
                    else:
                        parts.append(jnp.zeros((ch, GLA_DK), BF16))
                att_rows[a] = lax.dot_general(
                    qe_scr[hd, blk, chunk(a), :].astype(BF16),
                    jnp.concatenate(parts, axis=0), NT_DIMS, preferred_element_type=F32)
            att.append(jnp.where(keep, jnp.concatenate(att_rows, axis=0), 0.0).astype(BF16))
        for hd in heads:
            tot = total[hd]
            vcols = slice(hd * GLA_DV, (hd + 1) * GLA_DV)
            st = st_ref[hd]
            qe_in = jnp.concatenate(
                [(qe_scr[hd, blk, chunk(c), :] * jnp.exp(tot[pos[c]])).astype(BF16)
                 for c in range(nc)], axis=0)
            v = v_ref[0, rows, vcols]
            o = (jnp.dot(att[hd], v, preferred_element_type=F32)
                 + lax.dot_general(qe_in, st.astype(BF16), NT_DIMS,
                                   preferred_element_type=F32))
            if reverse:
                o = o + of_ref[0, rows, vcols]
                ms = jnp.mean(o * o, axis=-1, keepdims=True)
                o = o * lax.rsqrt(ms + NORM_EPS) * g_ref[...]
                gz = gz_ref[0, rows, vcols].astype(F32)
                y_ref[0, rows, vcols] = (o * (gz / (1.0 + jnp.exp(-gz)))).astype(y_ref.dtype)
            else:
                y_ref[0, rows, vcols] = o
            k_all = jnp.concatenate(
                [(kd_scr[hd, blk, chunk(c), :]
                  * jnp.exp(tot[nc] - tot[pos[c] + 1])).astype(BF16)
                 for c in range(nc)], axis=0)
            vt = v.astype(F32).T.astype(BF16)
            st_ref[hd] = (st * jnp.exp(tot[nc])
                          + jnp.dot(vt, k_all, preferred_element_type=F32))


def _gla_direction(proj3, lr3, w2pad, bias, reverse, o_fwd=None, norm_g=None):
    bs, seq, _ = proj3.shape
    nt = seq // GLA_TILE
    nh = GLA_STEP_HEADS

    def tile(t):
        return nt - 1 - t if reverse else t

    def spec(width, off):
        return pl.BlockSpec((1, GLA_TILE, nh * width),
                            lambda b, h, t: (b, tile(t), off // (nh * width) + h))

    in_specs = [spec(GLA_DK, OFF_GQ), spec(GLA_DK, OFF_GK), spec(GLA_DV, OFF_GV),
                pl.BlockSpec((1, GLA_TILE, LR_PAD), lambda b, h, t: (b, tile(t), 0)),
                pl.BlockSpec((LR_PAD, nh * GLA_DK), lambda b, h, t: (0, h)),
                pl.BlockSpec((1, nh * GLA_DK), lambda b, h, t: (0, h))]
    args = [proj3, proj3, proj3, lr3, w2pad, bias]
    out_spec = pl.BlockSpec((1, GLA_TILE, nh * GLA_DV), lambda b, h, t: (b, tile(t), h))
    if reverse:
        in_specs += [out_spec, spec(GLA_DV, OFF_GZ),
                     pl.BlockSpec((1, GLA_DV), lambda b, h, t: (0, 0))]
        args += [o_fwd, proj3, norm_g]
    block_scratch = (nh, GLA_TILE_BLOCKS, GLA_BLOCK, GLA_DK)
    return pl.pallas_call(
        functools.partial(_gla_kernel, reverse=reverse),
        grid=(bs, GLA_HEADS // nh, nt),
        in_specs=in_specs,
        out_specs=out_spec,
        out_shape=jax.ShapeDtypeStruct((bs, seq, GLA_VAL), BF16 if reverse else F32),
        scratch_shapes=[pltpu.VMEM((nh, GLA_DV, GLA_DK), F32),
                        pltpu.VMEM(block_scratch, F32),
                        pltpu.VMEM(block_scratch, F32),
                        pltpu.VMEM(block_scratch, BF16)],
        compiler_params=_params(("parallel", "parallel", "arbitrary")),
        name="gla_bwd" if reverse else "gla_fwd",
    )(*args)


def _merge_kernel(ya_ref, yb_ref, wa_ref, wb_ref, ga_ref, gb_ref, o_ref):
    ua = jnp.dot(ya_ref[...], wa_ref[...].astype(BF16), preferred_element_type=F32)
    ub = jnp.dot(yb_ref[...], wb_ref[...].astype(BF16), preferred_element_type=F32)
    ga = ga_ref[...].astype(F32)
    gb = gb_ref[...].astype(F32)
    merged = ua / (1.0 + jnp.exp(-ga)) + ub / (1.0 + jnp.exp(-gb))
    o_ref[...] = merged.astype(o_ref.dtype)


def _merge(ya, yb, wa, wb, proj, tm=1024, tn=1024):
    m = ya.shape[0]
    return pl.pallas_call(
        _merge_kernel,
        grid=(m // tm, D_MODEL // tn),
        in_specs=[pl.BlockSpec((tm, ATT_WIDTH), lambda i, j: (i, 0)),
                  pl.BlockSpec((tm, GLA_VAL), lambda i, j: (i, 0)),
                  pl.BlockSpec((ATT_WIDTH, tn), lambda i, j: (0, j)),
                  pl.BlockSpec((GLA_VAL, tn), lambda i, j: (0, j)),
                  pl.BlockSpec((tm, tn), lambda i, j: (i, OFF_MG // tn + j)),
                  pl.BlockSpec((tm, tn), lambda i, j: (i, (OFF_MG + D_MODEL) // tn + j))],
        out_specs=pl.BlockSpec((tm, tn), lambda i, j: (i, j)),
        out_shape=jax.ShapeDtypeStruct((m, D_MODEL), BF16),
        compiler_params=pltpu.CompilerParams(dimension_semantics=("parallel", "arbitrary"),
                                             vmem_limit_bytes=BIG_VMEM_LIMIT),
        name="merge",
    )(ya, yb, wa, wb, proj, proj)


def _out_kernel(x_ref, a_ref, w_ref, g_ref, o_ref):
    u = x_ref[...] + jnp.dot(a_ref[...], w_ref[...], preferred_element_type=F32)
    ms = jnp.mean(u * u, axis=-1, keepdims=True)
    o_ref[...] = (u * lax.rsqrt(ms + NORM_EPS)) * g_ref[...]


def _out_proj(x, merged, w_o, g, tm=256):
    m = x.shape[0]
    return pl.pallas_call(
        _out_kernel,
        grid=(m // tm,),
        in_specs=[pl.BlockSpec((tm, D_MODEL), lambda i: (i, 0)),
                  pl.BlockSpec((tm, D_MODEL), lambda i: (i, 0)),
                  pl.BlockSpec((D_MODEL, D_MODEL), lambda i: (0, 0),
                               pipeline_mode=pl.Buffered(1)),
                  pl.BlockSpec((1, D_MODEL), lambda i: (0, 0))],
        out_specs=pl.BlockSpec((tm, D_MODEL), lambda i: (i, 0)),
        out_shape=jax.ShapeDtypeStruct((m, D_MODEL), F32),
        compiler_params=pltpu.CompilerParams(dimension_semantics=("parallel",),
                                             vmem_limit_bytes=BIG_VMEM_LIMIT),
        name="out_proj",
    )(x, merged, w_o, g.reshape(1, D_MODEL))


def _layer(x, w, final_g, w_o):
    bs, seq, d = x.shape
    m = bs * seq
    x2 = x.reshape(m, d)
    h, h4, h16, lr3 = _rmsnorm(x, w["norm_g"], w["wt_lr"])
    tn = IN_PROJ_TN
    first_tiles, main_tiles = GROUP_COLS // tn, OFF_MG // tn

    def token_order_rows(j):
        return jnp.where(j < first_tiles, j * tn,
                         jnp.where(j < main_tiles, j * tn + (N_GROUPS - 1) * GROUP_COLS,
                                   W_OFF_MG + (j - main_tiles) * tn))

    if w_o.dtype == BF16:
        proj = _matmul_nt(h.reshape(m, d), w["wt"], N_PROJ, token_order_rows, "in_proj")
    else:
        proj, w_o = _matmul_nt(h.reshape(m, d), w["wt"], N_PROJ, token_order_rows,
                               "in_proj", cast=w_o)
    proj3 = proj.reshape(bs, seq, N_PROJ)
    cls = []
    for g, (dilation, hc) in enumerate(((4, h4), (16, h16)), start=1):
        p = _matmul_nt(hc.reshape(m, d), w["wt"], GROUP_COLS,
                       lambda j, g=g: j * tn + g * GROUP_COLS, f"in_proj_d{dilation}")
        cls.append(p.reshape(bs, dilation, seq // dilation, GROUP_COLS))
    ya = _attention(w["slopes"], proj3, *cls)

    o_f = _gla_direction(proj3, lr3, w["w2_f"], w["b_f"], False)
    yb = _gla_direction(proj3, lr3, w["w2_b"], w["b_b"], True, o_f, w["gla_norm_g"])

    merged = _merge(ya.reshape(m, ATT_WIDTH), yb.reshape(m, GLA_VAL),
                    w["w_up_a"], w["w_up_b"], proj)
    y = _out_proj(x2, merged, w_o, final_g)
    return y.reshape(bs, seq, d), w_o


def _pad_rows(w2, row0):
    return jnp.zeros((LR_PAD, GLA_KEY), BF16).at[row0:row0 + GLA_RANK].set(w2.astype(BF16))


def kernel(x_prompt, x_sample, norm_g, w_in, gla_w2_f, gla_b_f, gla_w2_b, gla_b_b,
           gla_norm_g, w_up_a, w_up_b, w_o, final_norm_g):
    assert norm_g.shape[0] == 1, "single-layer kernel"
    wt = w_in[0].T
    w = {
        "norm_g": norm_g[0],
        "wt": wt,
        "wt_lr": jnp.pad(wt[W_OFF_LR:W_OFF_MG],
                         ((0, LR_PAD - 2 * GLA_RANK), (0, 0))).astype(BF16),
        "w2_f": _pad_rows(gla_w2_f[0], 0),
        "w2_b": _pad_rows(gla_w2_b[0], GLA_RANK),
        "b_f": gla_b_f[0].reshape(1, GLA_KEY),
        "b_b": gla_b_b[0].reshape(1, GLA_KEY),
        "gla_norm_g": gla_norm_g[0].reshape(1, GLA_DV),
        "w_up_a": w_up_a[0],
        "w_up_b": w_up_b[0],
        "slopes": jnp.exp2(-8.0 * (jnp.arange(ATT_HEADS, dtype=F32) + 1.0) / ATT_HEADS),
    }
    y_prompt, w_o_bf16 = _layer(x_prompt, w, final_norm_g, w_o[0])
    y_sample, _ = _layer(x_sample, w, final_norm_g, w_o_bf16)
    return (y_prompt, y_sample)
```

```python
import functools

import jax
import jax.numpy as jnp
import numpy as np
from jax import lax
from jax.experimental import pallas as pl
from jax.experimental.pallas import tpu as pltpu

F32 = jnp.float32
BF16 = jnp.bfloat16

D_MODEL = 4096
ATT_DILATIONS = (1, 4, 16)
N_GROUPS = 3
ATT_HEAD_DIM = 128
ATT_HEADS = 8
ATT_WIDTH = ATT_HEADS * ATT_HEAD_DIM
ATT_RADIUS = 64
GLA_HEADS = 4
GLA_KEY = 1024
GLA_VAL = 2048
GLA_DK = GLA_KEY // GLA_HEADS
GLA_DV = GLA_VAL // GLA_HEADS
GLA_RANK = 16
GLA_TAU = 16.0
GLA_CHUNK = 64
NORM_EPS = 1e-6
NEG_INF = -1e30

GROUP_COLS = 3 * ATT_WIDTH
W_OFF_AZ = N_GROUPS * GROUP_COLS
W_OFF_LR = W_OFF_AZ + ATT_WIDTH + 2 * GLA_KEY + 2 * GLA_VAL
W_OFF_MG = W_OFF_LR + 2 * GLA_RANK
OFF_AZ = GROUP_COLS
OFF_GQ = OFF_AZ + ATT_WIDTH
OFF_GK = OFF_GQ + GLA_KEY
OFF_GV = OFF_GK + GLA_KEY
OFF_GZ = OFF_GV + GLA_VAL
OFF_MG = OFF_GZ + GLA_VAL
N_PROJ = OFF_MG + 2 * D_MODEL
LR_PAD = 128

VMEM_LIMIT = 56 * 1024 * 1024
BIG_VMEM_LIMIT = 62 * 1024 * 1024


def _params(sem):
    return pltpu.CompilerParams(dimension_semantics=sem, vmem_limit_bytes=VMEM_LIMIT)


NORM_ROWS = 256
NT_DIMS = (((1,), (1,)), ((), ()))


def _class_major_permutation():
    perm = np.zeros((2 * NORM_ROWS, NORM_ROWS), np.float32)
    for block, dilation in enumerate((4, 16)):
        per_class = NORM_ROWS // dilation
        for r in range(dilation):
            for u in range(per_class):
                perm[block * NORM_ROWS + r * per_class + u, u * dilation + r] = 1.0
    return jnp.asarray(perm, BF16)


def _rmsnorm_kernel(x_ref, g_ref, wlr_ref, perm_ref, h_ref, h4_ref, h16_ref, lr_ref):
    x = x_ref[0]
    ms = jnp.mean(x * x, axis=-1, keepdims=True)
    h = ((x * lax.rsqrt(ms + NORM_EPS)) * g_ref[...]).astype(h_ref.dtype)
    h_ref[0] = h
    lr_ref[0] = lax.dot_general(h, wlr_ref[...], NT_DIMS,
                                preferred_element_type=F32).astype(lr_ref.dtype)
    hp = jnp.dot(perm_ref[...], h, preferred_element_type=F32).astype(h_ref.dtype)
    for block, (out_ref, dilation) in enumerate(((h4_ref, 4), (h16_ref, 16))):
        per_class = NORM_ROWS // dilation
        for r in range(dilation):
            row0 = block * NORM_ROWS + r * per_class
            out_ref[0, r] = hp[row0:row0 + per_class]


def _rmsnorm(x, g, wt_lr):
    bs, seq, d = x.shape
    tm = NORM_ROWS

    def cls_spec(dilation):
        return pl.BlockSpec((1, dilation, tm // dilation, d), lambda b, i: (b, 0, i, 0))

    def cls_shape(dilation):
        return jax.ShapeDtypeStruct((bs, dilation, seq // dilation, d), BF16)

    return pl.pallas_call(
        _rmsnorm_kernel,
        grid=(bs, seq // tm),
        in_specs=[pl.BlockSpec((1, tm, d), lambda b, i: (b, i, 0)),
                  pl.BlockSpec((1, d), lambda b, i: (0, 0)),
                  pl.BlockSpec((LR_PAD, d), lambda b, i: (0, 0)),
                  pl.BlockSpec((2 * tm, tm), lambda b, i: (0, 0))],
        out_specs=[pl.BlockSpec((1, tm, d), lambda b, i: (b, i, 0)),
                   cls_spec(4), cls_spec(16),
                   pl.BlockSpec((1, tm, LR_PAD), lambda b, i: (b, i, 0))],
        out_shape=[jax.ShapeDtypeStruct((bs, seq, d), BF16), cls_shape(4), cls_shape(16),
                   jax.ShapeDtypeStruct((bs, seq, LR_PAD), BF16)],
        compiler_params=_params(("parallel", "parallel")),
        name="rmsnorm",
    )(x, g.reshape(1, d), wt_lr, _class_major_permutation())


IN_PROJ_TN = 1024


def _matmul_nt_kernel(a_ref, bt_ref, o_ref):
    o_ref[...] = lax.dot_general(a_ref[...], bt_ref[...].astype(BF16), NT_DIMS,
                                 preferred_element_type=F32).astype(o_ref.dtype)


def _matmul_nt_cast_kernel(a_ref, bt_ref, c_ref, o_ref, co_ref):
    _matmul_nt_kernel(a_ref, bt_ref, o_ref)
    co_ref[...] = c_ref[...].astype(co_ref.dtype)


CAST_ROWS = 32


def _matmul_nt(a, bt, n_out, row_start, name, tm=1024, tn=IN_PROJ_TN, cast=None):
    m, k = a.shape
    n_i = m // tm
    in_specs = [pl.BlockSpec((tm, k), lambda j, i: (i, 0)),
                pl.BlockSpec((pl.Element(tn), pl.Element(k)),
                             lambda j, i: (pl.multiple_of(row_start(j), 8), 0))]
    out_specs = pl.BlockSpec((tm, tn), lambda j, i: (i, j))
    out_shape = jax.ShapeDtypeStruct((m, n_out), BF16)
    body, operands = _matmul_nt_kernel, (a, bt)
    if cast is not None:
        n_blocks = cast.shape[0] // CAST_ROWS
        assert n_blocks <= (n_out // tn) * n_i, "not enough grid steps to cast every row"

        def cast_block(j, i):
            return jnp.minimum(j * n_i + i, n_blocks - 1), 0

        cast_spec = pl.BlockSpec((CAST_ROWS, cast.shape[1]), cast_block)
        in_specs.append(cast_spec)
        out_specs = [out_specs, cast_spec]
        out_shape = [out_shape, jax.ShapeDtypeStruct(cast.shape, BF16)]
        body, operands = _matmul_nt_cast_kernel, (a, bt, cast)
    return pl.pallas_call(
        body,
        grid=(n_out // tn, n_i),
        in_specs=in_specs,
        out_specs=out_specs,
        out_shape=out_shape,
        compiler_params=pltpu.CompilerParams(dimension_semantics=("parallel", "arbitrary"),
                                             vmem_limit_bytes=BIG_VMEM_LIMIT),
        name=name,
    )(*operands)


ATT_TILE = 2048
ATT_QB = 128
ATT_KW = ATT_QB + 2 * ATT_RADIUS
ATT_WINDOW_OFFSETS = (-ATT_RADIUS, 0, -2 * ATT_RADIUS)
ATT_MIX_ROWS = 64


def _attn_kernel(slopes_ref, q1_ref, k1_ref, v1_ref, q2_ref, k2_ref, v2_ref,
                 q3_ref, k3_ref, v3_ref, z_ref, y_ref, o_scr, l_scr, bias_scr, *, seq):
    h = pl.program_id(1)
    tile = pl.program_id(2)
    slope = slopes_ref[h]
    scale = ATT_HEAD_DIM ** -0.5

    @pl.when(tile == 0)
    def _():
        col_minus_row = (lax.broadcasted_iota(jnp.int32, (ATT_QB, ATT_KW), 1)
                         - lax.broadcasted_iota(jnp.int32, (ATT_QB, ATT_KW), 0))
        col = lax.broadcasted_iota(jnp.int32, (ATT_QB, ATT_KW), 1)
        for g, dilation in enumerate(ATT_DILATIONS):
            if seq // dilation >= ATT_KW:
                valid = [jnp.abs(col_minus_row + off) for off in ATT_WINDOW_OFFSETS]
            else:
                valid = [jnp.where(col // ATT_QB == c, jnp.abs(col_minus_row - c * ATT_QB),
                                   ATT_KW) for c in range(2)]
            for v, dist in enumerate(valid):
                bias_scr[g * len(ATT_WINDOW_OFFSETS) + v] = jnp.where(
                    dist <= ATT_RADIUS, -(slope * float(dilation)) * dist.astype(F32),
                    NEG_INF)

    for g, (q_ref, k_ref, v_ref) in enumerate(((q1_ref, k1_ref, v1_ref),
                                               (q2_ref, k2_ref, v2_ref),
                                               (q3_ref, k3_ref, v3_ref))):
        dilation = ATT_DILATIONS[g]
        length = seq // dilation
        per_class = ATT_TILE // dilation // ATT_QB
        paired = length < ATT_KW
        for r in range(dilation):
            for j in range(per_class):
                row0 = j * ATT_QB
                if paired:
                    kv_class, variant, w0 = r // 2, r % 2, 0
                else:
                    p0 = tile * (ATT_TILE // dilation) + row0
                    w0 = pl.multiple_of(jnp.clip(p0 - ATT_RADIUS, 0, length - ATT_KW),
                                        ATT_RADIUS)
                    off = w0 - p0
                    variant = jnp.where(off == ATT_WINDOW_OFFSETS[1], 1,
                                        jnp.where(off == ATT_WINDOW_OFFSETS[2], 2, 0))
                    kv_class = r
                bias = bias_scr[g * len(ATT_WINDOW_OFFSETS) + variant]
                q = q_ref[0, r, row0:row0 + ATT_QB, :]
                s = lax.dot_general(q, k_ref[0, kv_class, pl.ds(w0, ATT_KW), :], NT_DIMS,
                                    preferred_element_type=F32) * scale + bias
                m = jnp.max(s, axis=-1, keepdims=True)
                e = jnp.exp(s - m)
                l = jnp.sum(e, axis=-1, keepdims=True)
                o = jnp.dot(e.astype(BF16), v_ref[0, kv_class, pl.ds(w0, ATT_KW), :],
                            preferred_element_type=F32) / l
                lse = jnp.broadcast_to(m + jnp.log(l), (ATT_QB, ATT_HEAD_DIM))
                if dilation == 1:
                    rows = pl.ds(row0, ATT_QB)
                else:
                    rows = pl.ds(row0 * dilation + r, ATT_QB, stride=dilation)
                o_scr[g, rows, :] = o
                l_scr[g, rows, :] = lse

    for c in range(ATT_TILE // ATT_MIX_ROWS):
        rows = slice(c * ATT_MIX_ROWS, (c + 1) * ATT_MIX_ROWS)
        la, lb, lc = l_scr[0, rows, :], l_scr[1, rows, :], l_scr[2, rows, :]
        m = jnp.maximum(jnp.maximum(la, lb), lc)
        ea, eb, ec = jnp.exp(la - m), jnp.exp(lb - m), jnp.exp(lc - m)
        mix = ((ea * o_scr[0, rows, :] + eb * o_scr[1, rows, :] + ec * o_scr[2, rows, :])
               / (ea + eb + ec))
        z = z_ref[0, rows, :].astype(F32)
        y_ref[0, rows, :] = (mix * (z / (1.0 + jnp.exp(-z)))).astype(y_ref.dtype)


def _attention(slopes, proj3, cls4, cls16):
    bs, seq, _ = proj3.shape
    n_tiles = seq // ATT_TILE
    nat = proj3.reshape(bs, 1, seq, N_PROJ)

    def kv_shape(dilation):
        length = seq // dilation
        return (dilation, length) if length >= ATT_KW else (dilation // 2, 2 * length)

    def specs(dilation):
        tq = ATT_TILE // dilation
        q = pl.BlockSpec((1, dilation, tq, ATT_HEAD_DIM), lambda b, h, t: (b, 0, t, h))
        k = pl.BlockSpec((1,) + kv_shape(dilation) + (ATT_HEAD_DIM,),
                         lambda b, h, t: (b, 0, 0, ATT_HEADS + h))
        v = pl.BlockSpec((1,) + kv_shape(dilation) + (ATT_HEAD_DIM,),
                         lambda b, h, t: (b, 0, 0, 2 * ATT_HEADS + h))
        return [q, k, v]

    def kv_view(t, dilation):
        return t.reshape((bs,) + kv_shape(dilation) + (t.shape[-1],))

    tok = pl.BlockSpec((1, ATT_TILE, ATT_HEAD_DIM),
                       lambda b, h, t: (b, t, OFF_AZ // ATT_HEAD_DIM + h))
    return pl.pallas_call(
        functools.partial(_attn_kernel, seq=seq),
        grid=(bs, ATT_HEADS, n_tiles),
        in_specs=([pl.BlockSpec(memory_space=pltpu.SMEM)]
                  + specs(1) + specs(4) + specs(16) + [tok]),
        out_specs=pl.BlockSpec((1, ATT_TILE, ATT_HEAD_DIM), lambda b, h, t: (b, t, h)),
        out_shape=jax.ShapeDtypeStruct((bs, seq, ATT_WIDTH), BF16),
        scratch_shapes=[pltpu.VMEM((N_GROUPS, ATT_TILE, ATT_HEAD_DIM), F32),
                        pltpu.VMEM((N_GROUPS, ATT_TILE, ATT_HEAD_DIM), F32),
                        pltpu.VMEM((N_GROUPS * len(ATT_WINDOW_OFFSETS), ATT_QB, ATT_KW),
                                   F32)],
        compiler_params=_params(("parallel", "parallel", "arbitrary")),
        name="attention",
    )(slopes, nat, nat, nat, cls4, kv_view(cls4, 4), kv_view(cls4, 4),
      cls16, kv_view(cls16, 16), kv_view(cls16, 16), proj3)


GLA_BLOCK_CHUNKS = 4
GLA_BLOCK = GLA_BLOCK_CHUNKS * GLA_CHUNK
GLA_TILE_BLOCKS = 2
GLA_TILE = GLA_TILE_BLOCKS * GLA_BLOCK
GLA_STEP_HEADS = 4


def _gla_kernel(*refs, reverse):
    if reverse:
        (q_ref, k_ref, v_ref, z_ref, w2_ref, b2_ref, of_ref, gz_ref, g_ref,
         y_ref, st_ref, qe_scr, kd_scr, ke_scr) = refs
    else:
        (q_ref, k_ref, v_ref, z_ref, w2_ref, b2_ref,
         y_ref, st_ref, qe_scr, kd_scr, ke_scr) = refs

    @pl.when(pl.program_id(2) == 0)
    def _():
        st_ref[...] = jnp.zeros_like(st_ref)

    nc, ch = GLA_BLOCK_CHUNKS, GLA_CHUNK
    heads = range(GLA_STEP_HEADS)
    ri = lax.broadcasted_iota(jnp.int32, (GLA_BLOCK, GLA_BLOCK), 0)
    ci = lax.broadcasted_iota(jnp.int32, (GLA_BLOCK, GLA_BLOCK), 1)
    same_chunk = (ri // ch) == (ci // ch)
    if reverse:
        cum = (same_chunk & (ci >= ri)).astype(BF16)
        keep = (ci // ch > ri // ch) | (same_chunk & (ci > ri))
        end_row = 0
        order = list(range(nc - 1, -1, -1))
    else:
        cum = (same_chunk & (ci <= ri)).astype(BF16)
        keep = (ci // ch < ri // ch) | (same_chunk & (ci <= ri))
        end_row = ch - 1
        order = list(range(nc))
    pos = {c: p for p, c in enumerate(order)}

    def chunk(c):
        return slice(c * ch, (c + 1) * ch)

    blocks = range(GLA_TILE_BLOCKS)
    for blk in (reversed(blocks) if reverse else blocks):
        r0 = blk * GLA_BLOCK
        rows = slice(r0, r0 + GLA_BLOCK)
        x = jnp.dot(z_ref[0, rows, :], w2_ref[...],
                    preferred_element_type=F32) + b2_ref[...]
        la = (jnp.minimum(x, 0.0) - jnp.log(1.0 + jnp.exp(-jnp.abs(x)))) * (1.0 / GLA_TAU)
        hi = la.astype(BF16)
        lo = (la - hi.astype(F32)).astype(BF16)
        b_all = (jnp.dot(cum, hi, preferred_element_type=F32)
                 + jnp.dot(cum, lo, preferred_element_type=F32))
        total = []
        for hd in heads:
            cols = slice(hd * GLA_DK, (hd + 1) * GLA_DK)
            b = b_all[:, cols]
            ends = [b[c * ch + end_row:c * ch + end_row + 1, :] for c in range(nc)]
            tot = [jnp.zeros_like(ends[0])]
            for p in range(nc):
                tot.append(tot[-1] + ends[order[p]])
            total.append(tot)
            e_end = jnp.concatenate(
                [jnp.broadcast_to(jnp.exp(e), (ch, GLA_DK)) for e in ends], axis=0)
            ke = k_ref[0, rows, cols].astype(F32) * jnp.exp(-b)
            qe_scr[hd, blk] = (q_ref[0, rows, cols].astype(F32) * (GLA_DK ** -0.5)
                               * jnp.exp(b))
            ke_scr[hd, blk] = ke.astype(BF16)
            kd_scr[hd, blk] = ke * e_end
        att = []
        for hd in heads:
            tot = total[hd]
            att_rows = [None] * nc
            for p in range(nc):
                a = order[p]
                parts = []
                for c in range(nc):
                    if c == a:
                        parts.append(ke_scr[hd, blk, chunk(c), :])
                    elif pos[c] < p:
                        kd_c = kd_scr[hd, blk, chunk(c), :]
                        if pos[c] < p - 1:
                            kd_c = kd_c * jnp.exp(tot[p] - tot[pos[c] + 1])
                        parts.append(kd_c.astype(BF16))
                    else:
                        parts.append(jnp.zeros((ch, GLA_DK), BF16))
                att_rows[a] = lax.dot_general(
                    qe_scr[hd, blk, chunk(a), :].astype(BF16),
                    jnp.concatenate(parts, axis=0), NT_DIMS, preferred_element_type=F32)
            att.append(jnp.where(keep, jnp.concatenate(att_rows, axis=0), 0.0).astype(BF16))
        for hd in heads:
            tot = total[hd]
            vcols = slice(hd * GLA_DV, (hd + 1) * GLA_DV)
            st = st_ref[hd]
            qe_in = jnp.concatenate(
                [(qe_scr[hd, blk, chunk(c), :] * jnp.exp(tot[pos[c]])).astype(BF16)
                 for c in range(nc)], axis=0)
            v = v_ref[0, rows, vcols]
            o = (jnp.dot(att[hd], v, preferred_element_type=F32)
                 + lax.dot_general(qe_in, st.astype(BF16), NT_DIMS,
                                   preferred_element_type=F32))
            if reverse:
                o = o + of_ref[0, rows, vcols]
                ms = jnp.mean(o * o, axis=-1, keepdims=True)
                o = o * lax.rsqrt(ms + NORM_EPS) * g_ref[...]
                gz = gz_ref[0, rows, vcols].astype(F32)
                y_ref[0, rows, vcols] = (o * (gz / (1.0 + jnp.exp(-gz)))).astype(y_ref.dtype)
            else:
                y_ref[0, rows, vcols] = o
            k_all = jnp.concatenate(
                [(kd_scr[hd, blk, chunk(c), :]
                  * jnp.exp(tot[nc] - tot[pos[c] + 1])).astype(BF16)
                 for c in range(nc)], axis=0)
            vt = v.astype(F32).T.astype(BF16)
            st_ref[hd] = (st * jnp.exp(tot[nc])
                          + jnp.dot(vt, k_all, preferred_element_type=F32))


def _gla_direction(proj3, lr3, w2pad, bias, reverse, o_fwd=None, norm_g=None):
    bs, seq, _ = proj3.shape
    nt = seq // GLA_TILE
    nh = GLA_STEP_HEADS

    def tile(t):
        return nt - 1 - t if reverse else t

    def spec(width, off):
        return pl.BlockSpec((1, GLA_TILE, nh * width),
                            lambda b, h, t: (b, tile(t), off // (nh * width) + h))

    in_specs = [spec(GLA_DK, OFF_GQ), spec(GLA_DK, OFF_GK), spec(GLA_DV, OFF_GV),
                pl.BlockSpec((1, GLA_TILE, LR_PAD), lambda b, h, t: (b, tile(t), 0)),
                pl.BlockSpec((LR_PAD, nh * GLA_DK), lambda b, h, t: (0, h)),
                pl.BlockSpec((1, nh * GLA_DK), lambda b, h, t: (0, h))]
    args = [proj3, proj3, proj3, lr3, w2pad, bias]
    out_spec = pl.BlockSpec((1, GLA_TILE, nh * GLA_DV), lambda b, h, t: (b, tile(t), h))
    if reverse:
        in_specs += [out_spec, spec(GLA_DV, OFF_GZ),
                     pl.BlockSpec((1, GLA_DV), lambda b, h, t: (0, 0))]
        args += [o_fwd, proj3, norm_g]
    block_scratch = (nh, GLA_TILE_BLOCKS, GLA_BLOCK, GLA_DK)
    return pl.pallas_call(
        functools.partial(_gla_kernel, reverse=reverse),
        grid=(bs, GLA_HEADS // nh, nt),
        in_specs=in_specs,
        out_specs=out_spec,
        out_shape=jax.ShapeDtypeStruct((bs, seq, GLA_VAL), BF16 if reverse else F32),
        scratch_shapes=[pltpu.VMEM((nh, GLA_DV, GLA_DK), F32),
                        pltpu.VMEM(block_scratch, F32),
                        pltpu.VMEM(block_scratch, F32),
                        pltpu.VMEM(block_scratch, BF16)],
        compiler_params=_params(("parallel", "parallel", "arbitrary")),
        name="gla_bwd" if reverse else "gla_fwd",
    )(*args)


def _merge_kernel(ya_ref, yb_ref, wa_ref, wb_ref, ga_ref, gb_ref, o_ref):
    ua = jnp.dot(ya_ref[...], wa_ref[...].astype(BF16), preferred_element_type=F32)
    ub = jnp.dot(yb_ref[...], wb_ref[...].astype(BF16), preferred_element_type=F32)
    ga = ga_ref[...].astype(F32)
    gb = gb_ref[...].astype(F32)
    merged = ua / (1.0 + jnp.exp(-ga)) + ub / (1.0 + jnp.exp(-gb))
    o_ref[...] = merged.astype(o_ref.dtype)


def _merge(ya, yb, wa, wb, proj, tm=1024, tn=1024):
    m = ya.shape[0]
    return pl.pallas_call(
        _merge_kernel,
        grid=(m // tm, D_MODEL // tn),
        in_specs=[pl.BlockSpec((tm, ATT_WIDTH), lambda i, j: (i, 0)),
                  pl.BlockSpec((tm, GLA_VAL), lambda i, j: (i, 0)),
                  pl.BlockSpec((ATT_WIDTH, tn), lambda i, j: (0, j)),
                  pl.BlockSpec((GLA_VAL, tn), lambda i, j: (0, j)),
                  pl.BlockSpec((tm, tn), lambda i, j: (i, OFF_MG // tn + j)),
                  pl.BlockSpec((tm, tn), lambda i, j: (i, (OFF_MG + D_MODEL) // tn + j))],
        out_specs=pl.BlockSpec((tm, tn), lambda i, j: (i, j)),
        out_shape=jax.ShapeDtypeStruct((m, D_MODEL), BF16),
        compiler_params=pltpu.CompilerParams(dimension_semantics=("parallel", "arbitrary"),
                                             vmem_limit_bytes=BIG_VMEM_LIMIT),
        name="merge",
    )(ya, yb, wa, wb, proj, proj)


def _out_kernel(x_ref, a_ref, w_ref, g_ref, o_ref):
    u = x_ref[...] + jnp.dot(a_ref[...], w_ref[...], preferred_element_type=F32)
    ms = jnp.mean(u * u, axis=-1, keepdims=True)
    o_ref[...] = (u * lax.rsqrt(ms + NORM_EPS)) * g_ref[...]


def _out_proj(x, merged, w_o, g, tm=256):
    m = x.shape[0]
    return pl.pallas_call(
        _out_kernel,
        grid=(m // tm,),
        in_specs=[pl.BlockSpec((tm, D_MODEL), lambda i: (i, 0)),
                  pl.BlockSpec((tm, D_MODEL), lambda i: (i, 0)),
                  pl.BlockSpec((D_MODEL, D_MODEL), lambda i: (0, 0),
                               pipeline_mode=pl.Buffered(1)),
                  pl.BlockSpec((1, D_MODEL), lambda i: (0, 0))],
        out_specs=pl.BlockSpec((tm, D_MODEL), lambda i: (i, 0)),
        out_shape=jax.ShapeDtypeStruct((m, D_MODEL), F32),
        compiler_params=pltpu.CompilerParams(dimension_semantics=("parallel",),
                                             vmem_limit_bytes=BIG_VMEM_LIMIT),
        name="out_proj",
    )(x, merged, w_o, g.reshape(1, D_MODEL))


def _layer(x, w, final_g, w_o):
    bs, seq, d = x.shape
    m = bs * seq
    x2 = x.reshape(m, d)
    h, h4, h16, lr3 = _rmsnorm(x, w["norm_g"], w["wt_lr"])
    tn = IN_PROJ_TN
    first_tiles, main_tiles = GROUP_COLS // tn, OFF_MG // tn

    def token_order_rows(j):
        return jnp.where(j < first_tiles, j * tn,
                         jnp.where(j < main_tiles, j * tn + (N_GROUPS - 1) * GROUP_COLS,
                                   W_OFF_MG + (j - main_tiles) * tn))

    if w_o.dtype == BF16:
        proj = _matmul_nt(h.reshape(m, d), w["wt"], N_PROJ, token_order_rows, "in_proj")
    else:
        proj, w_o = _matmul_nt(h.reshape(m, d), w["wt"], N_PROJ, token_order_rows,
                               "in_proj", cast=w_o)
    proj3 = proj.reshape(bs, seq, N_PROJ)
    cls = []
    for g, (dilation, hc) in enumerate(((4, h4), (16, h16)), start=1):
        p = _matmul_nt(hc.reshape(m, d), w["wt"], GROUP_COLS,
                       lambda j, g=g: j * tn + g * GROUP_COLS, f"in_proj_d{dilation}")
        cls.append(p.reshape(bs, dilation, seq // dilation, GROUP_COLS))
    ya = _attention(w["slopes"], proj3, *cls)

    o_f = _gla_direction(proj3, lr3, w["w2_f"], w["b_f"], False)
    yb = _gla_direction(proj3, lr3, w["w2_b"], w["b_b"], True, o_f, w["gla_norm_g"])

    merged = _merge(ya.reshape(m, ATT_WIDTH), yb.reshape(m, GLA_VAL),
                    w["w_up_a"], w["w_up_b"], proj)
    y = _out_proj(x2, merged, w_o, final_g)
    return y.reshape(bs, seq, d), w_o


def _pad_rows(w2, row0):
    return jnp.zeros((LR_PAD, GLA_KEY), BF16).at[row0:row0 + GLA_RANK].set(w2.astype(BF16))


def kernel(x_prompt, x_sample, norm_g, w_in, gla_w2_f, gla_b_f, gla_w2_b, gla_b_b,
           gla_norm_g, w_up_a, w_up_b, w_o, final_norm_g):
    assert norm_g.shape[0] == 1, "single-layer kernel"
    wt = w_in[0].T
    w = {
        "norm_g": norm_g[0],
        "wt": wt,
        "wt_lr": jnp.pad(wt[W_OFF_LR:W_OFF_MG],
                         ((0, LR_PAD - 2 * GLA_RANK), (0, 0))).astype(BF16),
        "w2_f": _pad_rows(gla_w2_f[0], 0),
        "w2_b": _pad_rows(gla_w2_b[0], GLA_RANK),
        "b_f": gla_b_f[0].reshape(1, GLA_KEY),
        "b_b": gla_b_b[0].reshape(1, GLA_KEY),
        "gla_norm_g": gla_norm_g[0].reshape(1, GLA_DV),
        "w_up_a": w_up_a[0],
        "w_up_b": w_up_b[0],
        "slopes": jnp.exp2(-8.0 * (jnp.arange(ATT_HEADS, dtype=F32) + 1.0) / ATT_HEADS),
    }
    y_prompt, w_o_bf16 = _layer(x_prompt, w, final_norm_g, w_o[0])
    y_sample, _ = _layer(x_sample, w, final_norm_g, w_o_bf16)
    return (y_prompt, y_sample)
```
